```python
import jax
import jax.numpy as jnp
from jax import lax
import numpy as np

D_MODEL = 1024
BATCH = 2
SEQ = 8192
DEPTH = 1
DEC_BATCH = 32
DEC_SEQ = 4
PAST_LEN = 16384
PAGE_SIZE = 128

N_HEADS = 16
HEAD_DIM = D_MODEL // N_HEADS
N_KV_HEADS = 4
GROUP = N_HEADS // N_KV_HEADS
Q_DIM = N_HEADS * HEAD_DIM
KV_DIM = N_KV_HEADS * HEAD_DIM
N_BRANCH = 3
ROT_DIM = HEAD_DIM // 4
ROPE_THETA = 500000.0
CMP_LEN = 32
CMP_STRIDE = 16
CMP_HIDDEN = HEAD_DIM
SLC_LEN = 64
N_SEL = 16
N_LOCAL = 2
WINDOW = 512
Q_BLOCK = 128
CONV_W = 3
CONV_DIM = D_MODEL
D_FF = 2816
NORM_EPS = 1e-6
FORCED_SCORE = 1e9
PROJ_SIZES = (CONV_DIM, CONV_DIM, CONV_DIM, Q_DIM, N_BRANCH * 2 * KV_DIM, N_BRANCH * N_HEADS, D_MODEL, D_MODEL)
PROJ_DIM = 3 * CONV_DIM + Q_DIM + N_BRANCH * 2 * KV_DIM + N_BRANCH * N_HEADS + 2 * D_MODEL

kernel_name = 'nsa_shortconv_gated_hybrid_step'


def rms_norm(x, g):
    x32 = x.astype(jnp.float32)
    y = x32 * lax.rsqrt(jnp.mean(x32 * x32, axis=-1, keepdims=True) + NORM_EPS)
    return (y * g.astype(jnp.float32)).astype(x.dtype)


def rope(x, pos):
    half = ROT_DIM // 2
    inv_freq = jnp.power(ROPE_THETA, -jnp.arange(half, dtype=jnp.float32) * (2.0 / ROT_DIM))
    ang = pos.astype(jnp.float32)[:, None] * inv_freq[None, :]
    shape = (1, pos.shape[0]) + (1,) * (x.ndim - 3) + (half,)
    cos = jnp.cos(ang).reshape(shape)
    sin = jnp.sin(ang).reshape(shape)
    x1 = x[..., :half].astype(jnp.float32)
    x2 = x[..., half:ROT_DIM].astype(jnp.float32)
    rot = jnp.concatenate([x1 * cos - x2 * sin, x2 * cos + x1 * sin], axis=-1).astype(x.dtype)
    return jnp.concatenate([rot, x[..., ROT_DIM:]], axis=-1)


def causal_dwconv(x, prev, w):
    T = x.shape[1]
    xp = jnp.concatenate([prev, x], axis=1)
    y = xp[:, 0:T] * w[0]
    for j in range(1, CONV_W):
        y = y + xp[:, j:j + T] * w[j]
    return y, xp[:, -(CONV_W - 1):]


def split_projection(z):
    parts, off = [], 0
    for n in PROJ_SIZES:
        parts.append(z[..., off:off + n])
        off += n
    return parts


def masked_softmax(s, mask):
    s = jnp.where(mask, s, -jnp.inf)
    m = jnp.max(s, axis=-1, keepdims=True)
    m = jnp.where(jnp.isfinite(m), m, 0.0)
    e = jnp.where(mask, jnp.exp(s - m), 0.0)
    return e / jnp.maximum(jnp.sum(e, axis=-1, keepdims=True), 1e-30)


def compress(rows, pe, w1, b1, w2, b2):
    B, N = rows.shape[:2]
    R = CMP_LEN // CMP_STRIDE
    n_chunk = N // CMP_STRIDE
    n_cmp = n_chunk - R + 1
    ch = rows[:, :n_chunk * CMP_STRIDE].reshape(B, n_chunk, CMP_STRIDE, N_KV_HEADS, HEAD_DIM)
    pe = pe.reshape(R, CMP_STRIDE, HEAD_DIM)
    w1 = w1.reshape(R, CMP_STRIDE, HEAD_DIM, CMP_HIDDEN)
    hid = b1
    for r in range(R):
        hid = hid + jnp.einsum('bcsgd,sde->bcge', ch[:, r:r + n_cmp] + pe[r][:, None, :], w1[r])
    return jnp.einsum('bcge,ed->bcgd', jax.nn.silu(hid), w2) + b2


def block_overlap(n_cmp, n_slc):
    cs = jnp.arange(n_cmp)[:, None] * CMP_STRIDE
    ss = jnp.arange(n_slc)[None, :] * SLC_LEN
    return ((cs < ss + SLC_LEN) & (cs + CMP_LEN > ss)).astype(jnp.float32)


def nsa_attend(q, gates, t_pos, kc, vc, ks, vs, kw, vw, kw_pos):
    f32 = jnp.float32
    B, Tq = q.shape[:2]
    scale = HEAD_DIM ** -0.5
    qg = q.reshape(B, Tq, N_KV_HEADS, GROUP, HEAD_DIM)
    n_cmp = kc.shape[1]
    c_end = jnp.arange(n_cmp) * CMP_STRIDE + (CMP_LEN - 1)
    c_mask = c_end[None, :] <= t_pos[:, None]
    s_c = jnp.einsum('btgrd,bcgd->bgrtc', qg, kc, preferred_element_type=f32) * scale
    p_c = masked_softmax(s_c, c_mask)
    o_c = jnp.einsum('bgrtc,bcgd->btgrd', p_c.astype(vc.dtype), vc, preferred_element_type=f32)
    n_slc = ks.shape[1]
    n_pick = min(N_SEL, n_slc)
    imp = jnp.einsum('bgrtc,cs->bgts', p_c, block_overlap(n_cmp, n_slc))
    blk = jnp.arange(n_slc)[None, :]
    cur = (t_pos // SLC_LEN)[:, None]
    valid = blk <= cur
    forced = (blk == 0) | (valid & (blk > cur - N_LOCAL))
    imp = jnp.where(forced, FORCED_SCORE, jnp.where(valid, imp, -FORCED_SCORE))
    _, idx = lax.top_k(imp, n_pick)
    take = jax.vmap(jax.vmap(lambda blocks, ix: blocks[ix]))
    k_sel = take(ks.transpose(0, 3, 1, 2, 4), idx).reshape(B, N_KV_HEADS, Tq, n_pick * SLC_LEN, HEAD_DIM)
    v_sel = take(vs.transpose(0, 3, 1, 2, 4), idx).reshape(B, N_KV_HEADS, Tq, n_pick * SLC_LEN, HEAD_DIM)
    sel_pos = (idx[..., None] * SLC_LEN + jnp.arange(SLC_LEN)).reshape(B, N_KV_HEADS, Tq, n_pick * SLC_LEN)
    s_mask = (sel_pos <= t_pos[:, None])[:, :, None]
    s_s = jnp.einsum('btgrd,bgtkd->bgrtk', qg, k_sel, preferred_element_type=f32) * scale
    p_s = masked_softmax(s_s, s_mask)
    o_s = jnp.einsum('bgrtk,bgtkd->btgrd', p_s.astype(v_sel.dtype), v_sel, preferred_element_type=f32)
    w_mask = (kw_pos[None, :] <= t_pos[:, None]) & (kw_pos[None, :] > t_pos[:, None] - WINDOW) & (kw_pos[None, :] >= 0)
    s_w = jnp.einsum('btgrd,bkgd->bgrtk', qg, kw, preferred_element_type=f32) * scale
    p_w = masked_softmax(s_w, w_mask)
    o_w = jnp.einsum('bgrtk,bkgd->btgrd', p_w.astype(vw.dtype), vw, preferred_element_type=f32)
    g = gates.reshape(B, Tq, N_KV_HEADS, GROUP, N_BRANCH).astype(f32)
    o = g[..., 0:1] * o_c + g[..., 1:2] * o_s + g[..., 2:3] * o_w
    return o.reshape(B, Tq, Q_DIM).astype(q.dtype)


def nsa_prompt(q, kv, gates, pe, w1, b1, w2, b2):
    B, S = q.shape[:2]
    kc = compress(kv[:, :, 0, 0], pe[0], w1[0], b1[0], w2[0], b2[0])
    vc = compress(kv[:, :, 0, 1], pe[1], w1[1], b1[1], w2[1], b2[1])
    ks = kv[:, :, 1, 0].reshape(B, S // SLC_LEN, SLC_LEN, N_KV_HEADS, HEAD_DIM)
    vs = kv[:, :, 1, 1].reshape(B, S // SLC_LEN, SLC_LEN, N_KV_HEADS, HEAD_DIM)
    pad = ((0, 0), (WINDOW, 0), (0, 0), (0, 0))
    kw_pad = jnp.pad(kv[:, :, 2, 0], pad)
    vw_pad = jnp.pad(kv[:, :, 2, 1], pad)

    def block(i):
        s0 = i * Q_BLOCK
        qb = lax.dynamic_slice_in_dim(q, s0, Q_BLOCK, axis=1)
        gb = lax.dynamic_slice_in_dim(gates, s0, Q_BLOCK, axis=1)
        kwb = lax.dynamic_slice_in_dim(kw_pad, s0, WINDOW + Q_BLOCK, axis=1)
        vwb = lax.dynamic_slice_in_dim(vw_pad, s0, WINDOW + Q_BLOCK, axis=1)
        t_pos = s0 + jnp.arange(Q_BLOCK)
        kw_pos = s0 - WINDOW + jnp.arange(WINDOW + Q_BLOCK)
        return nsa_attend(qb, gb, t_pos, kc, vc, ks, vs, kwb, vwb, kw_pos)

    o = lax.map(block, jnp.arange(S // Q_BLOCK))
    return o.transpose(1, 0, 2, 3).reshape(B, S, Q_DIM)


def nsa_sample(q, kv, gates, past_cmp, past_slc, win_buf, pe, w1, b1, w2, b2):
    B, T = q.shape[:2]
    past = past_cmp.shape[1]
    t_pos = past + jnp.arange(T)
    cmp_all = jnp.concatenate([past_cmp, kv[:, :, 0]], axis=1)
    kc = compress(cmp_all[:, :, 0], pe[0], w1[0], b1[0], w2[0], b2[0])
    vc = compress(cmp_all[:, :, 1], pe[1], w1[1], b1[1], w2[1], b2[1])
    n_tot = past + T
    n_slc = -(-n_tot // SLC_LEN)
    slc_all = jnp.concatenate([past_slc, kv[:, :, 1]], axis=1)
    slc_all = jnp.pad(slc_all, ((0, 0), (0, n_slc * SLC_LEN - n_tot), (0, 0), (0, 0), (0, 0)))
    ks = slc_all[:, :, 0].reshape(B, n_slc, SLC_LEN, N_KV_HEADS, HEAD_DIM)
    vs = slc_all[:, :, 1].reshape(B, n_slc, SLC_LEN, N_KV_HEADS, HEAD_DIM)
    wb = win_buf.shape[1]
    win_all = jnp.concatenate([win_buf, kv[:, :, 2]], axis=1)
    kw_pos = past - wb + jnp.arange(wb + T)
    o = nsa_attend(q, gates, t_pos, kc, vc, ks, vs, win_all[:, :, 0], win_all[:, :, 1], kw_pos)
    return o, win_all[:, -wb:]


def gather_pages(pool, page_table):
    g = pool[page_table]
    return g.reshape((page_table.shape[0], page_table.shape[1] * pool.shape[1]) + pool.shape[2:])


def mixer_in(x, pos, conv_prev, g_norm, w_in, conv_w, w_conv_out):
    B, T, _ = x.shape
    u = rms_norm(x, g_norm)
    z = jnp.einsum('btd,dn->btn', u, w_in)
    x_in, c_gate, b_gate, q, kv, nsa_logit, gate_a, gate_b = split_projection(z)
    yc, conv_state = causal_dwconv(c_gate * x_in, conv_prev, conv_w)
    y_a = jnp.einsum('btc,cd->btd', b_gate * yc, w_conv_out)
    q = rope(q.reshape(B, T, N_HEADS, HEAD_DIM), pos)
    kv = kv.reshape(B, T, N_BRANCH, 2, N_KV_HEADS, HEAD_DIM)
    kv = jnp.stack([rope(kv[:, :, :, 0], pos), kv[:, :, :, 1]], axis=3)
    gates = jax.nn.sigmoid(nsa_logit.reshape(B, T, N_HEADS, N_BRANCH))
    return y_a, q, kv, gates, gate_a, gate_b, conv_state


def mixer_out(x, y_a, o, gate_a, gate_b, w_attn_out, w_out):
    y_b = jnp.einsum('btq,qd->btd', o, w_attn_out)
    m = jax.nn.sigmoid(gate_a) * y_a + jax.nn.sigmoid(gate_b) * y_b
    return x + jnp.einsum('btd,de->bte', m, w_out)


def conv_ffn(h, prev, g_norm, w_ff_in, ff_conv_w, w_ff_down):
    v = rms_norm(h, g_norm)
    a, g = jnp.split(jnp.einsum('btd,df->btf', v, w_ff_in), 2, axis=-1)
    ac, state = causal_dwconv(a, prev, ff_conv_w)
    return h + jnp.einsum('btf,fd->btd', jax.nn.silu(ac) * g, w_ff_down), state


def setup_inputs(seed: int = 0) -> dict:
    key = jax.random.key(seed)
    keys = jax.random.split(key, 24)
    f32 = jnp.float32
    n_pages = PAST_LEN // PAGE_SIZE
    n_used = DEC_BATCH * n_pages
    n_pool = n_used + (n_used + 3) // 4
    win_buf = min(WINDOW, PAST_LEN)

    def nrm(k, shape, scale):
        return jax.random.normal(k, shape, f32) * scale

    page_table = jax.random.permutation(keys[0], n_pool)[:n_used].reshape(DEC_BATCH, n_pages).astype(jnp.int32)
    return {
        'x_prompt': nrm(keys[1], (BATCH, SEQ, D_MODEL), 1.0),
        'x_sample': nrm(keys[2], (DEC_BATCH, DEC_SEQ, D_MODEL), 1.0),
        'cache_cmp_kv': nrm(keys[3], (DEPTH, n_pool, PAGE_SIZE, 2, N_KV_HEADS, HEAD_DIM), 1.0),
        'cache_slc_kv': nrm(keys[4], (DEPTH, n_pool, PAGE_SIZE, 2, N_KV_HEADS, HEAD_DIM), 1.0),
        'cache_win_kv': nrm(keys[5], (DEPTH, DEC_BATCH, win_buf, 2, N_KV_HEADS, HEAD_DIM), 1.0),
        'state_conv_mix': nrm(keys[6], (DEPTH, DEC_BATCH, CONV_W - 1, CONV_DIM), 1.0),
        'state_conv_ffn': nrm(keys[7], (DEPTH, DEC_BATCH, CONV_W - 1, D_FF), 1.0),
        'page_table': page_table,
        'norm_mix_g': 1.0 + nrm(keys[8], (DEPTH, D_MODEL), 0.02),
        'w_in': nrm(keys[9], (DEPTH, D_MODEL, PROJ_DIM), D_MODEL ** -0.5),
        'conv_mix_w': nrm(keys[10], (DEPTH, CONV_W, CONV_DIM), CONV_W ** -0.5),
        'w_conv_out': nrm(keys[11], (DEPTH, CONV_DIM, D_MODEL), CONV_DIM ** -0.5),
        'cmp_pe': nrm(keys[12], (DEPTH, 2, CMP_LEN, HEAD_DIM), 0.1),
        'cmp_w1': nrm(keys[13], (DEPTH, 2, CMP_LEN * HEAD_DIM, CMP_HIDDEN), (CMP_LEN * HEAD_DIM) ** -0.5),
        'cmp_b1': nrm(keys[14], (DEPTH, 2, CMP_HIDDEN), 0.02),
        'cmp_w2': nrm(keys[15], (DEPTH, 2, CMP_HIDDEN, HEAD_DIM), CMP_HIDDEN ** -0.5),
        'cmp_b2': nrm(keys[16], (DEPTH, 2, HEAD_DIM), 0.02),
        'w_attn_out': nrm(keys[17], (DEPTH, Q_DIM, D_MODEL), Q_DIM ** -0.5),
        'w_out': nrm(keys[18], (DEPTH, D_MODEL, D_MODEL), D_MODEL ** -0.5),
        'norm_ffn_g': 1.0 + nrm(keys[19], (DEPTH, D_MODEL), 0.02),
        'w_ff_in': nrm(keys[20], (DEPTH, D_MODEL, 2 * D_FF), D_MODEL ** -0.5),
        'ff_conv_w': nrm(keys[21], (DEPTH, CONV_W, D_FF), CONV_W ** -0.5),
        'w_ff_down': nrm(keys[22], (DEPTH, D_FF, D_MODEL), D_FF ** -0.5),
        'norm_final_g': 1.0 + nrm(keys[23], (D_MODEL,), 0.02),
    }


def reference(x_prompt, x_sample, cache_cmp_kv, cache_slc_kv, cache_win_kv, state_conv_mix, state_conv_ffn,
              page_table, norm_mix_g, w_in, conv_mix_w, w_conv_out, cmp_pe, cmp_w1, cmp_b1, cmp_w2, cmp_b2,
              w_attn_out, w_out, norm_ffn_g, w_ff_in, ff_conv_w, w_ff_down, norm_final_g):
    B, S, _ = x_prompt.shape
    DB, T, _ = x_sample.shape
    past_len = page_table.shape[1] * PAGE_SIZE
    pos_p = jnp.arange(S, dtype=jnp.int32)
    pos_s = past_len + jnp.arange(T, dtype=jnp.int32)
    wb_p = min(WINDOW, S)
    h_p, h_s = x_prompt, x_sample
    p_cmp, p_slc, p_win, p_cm, p_cf = [], [], [], [], []
    s_cmp, s_slc, s_win, s_cm, s_cf = [], [], [], [], []
    for l in range(DEPTH):
        cmp_p = (cmp_pe[l], cmp_w1[l], cmp_b1[l], cmp_w2[l], cmp_b2[l])
        y_a, q, kv, gates, ga, gb, cm_p = mixer_in(h_p, pos_p, jnp.zeros((B, CONV_W - 1, CONV_DIM), h_p.dtype),
                                                  norm_mix_g[l], w_in[l], conv_mix_w[l], w_conv_out[l])
        o = nsa_prompt(q, kv, gates, *cmp_p)
        h_p = mixer_out(h_p, y_a, o, ga, gb, w_attn_out[l], w_out[l])
        h_p, cf_p = conv_ffn(h_p, jnp.zeros((B, CONV_W - 1, D_FF), h_p.dtype),
                             norm_ffn_g[l], w_ff_in[l], ff_conv_w[l], w_ff_down[l])
        p_cmp.append(kv[:, :, 0])
        p_slc.append(kv[:, :, 1])
        p_win.append(kv[:, S - wb_p:, 2])
        p_cm.append(cm_p)
        p_cf.append(cf_p)
        y_a, q, kv, gates, ga, gb, cm_s = mixer_in(h_s, pos_s, state_conv_mix[l],
                                                  norm_mix_g[l], w_in[l], conv_mix_w[l], w_conv_out[l])
        o, win_s = nsa_sample(q, kv, gates, gather_pages(cache_cmp_kv[l], page_table),
                              gather_pages(cache_slc_kv[l], page_table), cache_win_kv[l], *cmp_p)
        h_s = mixer_out(h_s, y_a, o, ga, gb, w_attn_out[l], w_out[l])
        h_s, cf_s = conv_ffn(h_s, state_conv_ffn[l], norm_ffn_g[l], w_ff_in[l], ff_conv_w[l], w_ff_down[l])
        s_cmp.append(kv[:, :, 0])
        s_slc.append(kv[:, :, 1])
        s_win.append(win_s)
        s_cm.append(cm_s)
        s_cf.append(cf_s)
    y_prompt = rms_norm(h_p, norm_final_g)
    y_sample = rms_norm(h_s, norm_final_g)
    new_cmp_prompt = jnp.stack(p_cmp)
    new_slc_prompt = jnp.stack(p_slc)
    new_win_prompt = jnp.stack(p_win)
    new_conv_mix_prompt = jnp.stack(p_cm)
    new_conv_ffn_prompt = jnp.stack(p_cf)
    new_cmp_sample = jnp.stack(s_cmp)
    new_slc_sample = jnp.stack(s_slc)
    new_win_sample = jnp.stack(s_win)
    new_conv_mix_sample = jnp.stack(s_cm)
    new_conv_ffn_sample = jnp.stack(s_cf)
    return (y_prompt, y_sample, new_cmp_prompt, new_slc_prompt, new_win_prompt, new_conv_mix_prompt,
            new_conv_ffn_prompt, new_cmp_sample, new_slc_sample, new_win_sample, new_conv_mix_sample,
            new_conv_ffn_sample)
```

```python
import functools

import numpy as np
import jax
import jax.numpy as jnp
from jax import lax
from jax.experimental import pallas as pl
from jax.experimental.pallas import tpu as pltpu

F32 = jnp.float32
BF16 = jnp.bfloat16

N_HEADS = 16
HEAD_DIM = 64
N_KV_HEADS = 4
N_BRANCH = 3
ROT_DIM = 16
ROPE_THETA = 500000.0
CMP_LEN = 32
CMP_STRIDE = 16
SLC_LEN = 64
N_SEL = 16
N_LOCAL = 2
WINDOW = 512
CONV_W = 3
NORM_EPS = 1e-6
FORCED_SCORE = 1e9

LANES = 128
SUBLANES = 8
KV_W = 2 * N_KV_HEADS * HEAD_DIM
MASK_BIG = 2.0 ** 100
VMEM_LIMIT = 56 * 1024 * 1024

ROW_TILE = 256
CMP_ROWS = 2048
Q_TILE = 128
K_TILE = 512
SAMPLE_ROWS = 8
PAGES_PER_STEP = 32


def _cparams(sem):
    return pltpu.CompilerParams(dimension_semantics=sem, vmem_limit_bytes=VMEM_LIMIT)


def _const_spec(shape):
    nd = len(shape)
    return pl.BlockSpec(shape, lambda *_: (0,) * nd, pipeline_mode=pl.Buffered(1))


def _dot(a, b):
    return jnp.dot(a, b, preferred_element_type=F32)


def _dot_nt(a, b):
    return lax.dot_general(a, b, (((1,), (1,)), ((), ())), preferred_element_type=F32)


def _rms(x, g):
    y = x * lax.rsqrt(jnp.mean(x * x, axis=-1, keepdims=True) + NORM_EPS)
    return y * g


def _rope_block(x, cos, sa, sb):
    return x * cos + pltpu.roll(x, 8, 1) * sa + pltpu.roll(x, LANES - 8, 1) * sb


def _shifted_conv(buf, p, w_ref, rows):
    buf[SUBLANES:SUBLANES + rows, :] = p
    p1 = buf[SUBLANES - 1:SUBLANES - 1 + rows, :]
    p2 = buf[SUBLANES - 2:SUBLANES - 2 + rows, :]
    return p2 * w_ref[0:1, :] + p1 * w_ref[1:2, :] + p * w_ref[2:3, :]


def _front_kernel(sample, tiles_per_seq, *refs):
    if sample:
        (x_ref, g_ref, cos_ref, sa_ref, sb_ref, wc_ref, wq_ref, wkv_ref, wgl_ref, wgab_ref, cw_ref, wco_ref,
         st_ref, ma_ref, sgb_ref, q_ref, cmp_ref, slc_ref, win_ref, kvb_ref, gl_ref, pst_ref, pbuf) = refs
    else:
        (x_ref, g_ref, cos_ref, sa_ref, sb_ref, wc_ref, wq_ref, wkv_ref, wgl_ref, wgab_ref, cw_ref, wco_ref,
         ma_ref, sgb_ref, q_ref, cmp_ref, slc_ref, win_ref, kvb_ref, gl_ref, pst_ref, pbuf) = refs
    rows, d = x_ref.shape
    u = _rms(x_ref[...], g_ref[...]).astype(BF16)

    zc = _dot(u, wc_ref[...])
    p = zc[:, d:2 * d] * zc[:, 0:d]
    if sample:
        row = lax.broadcasted_iota(jnp.int32, (rows, 1), 0) % SAMPLE_ROWS
        is_state = (row >= SAMPLE_ROWS - 4 - (CONV_W - 1)) & (row < SAMPLE_ROWS - 4)
        p = jnp.where(is_state, st_ref[...], p)
        pbuf[0:SUBLANES, :] = jnp.zeros((SUBLANES, d), F32)
    else:
        @pl.when(pl.program_id(0) % tiles_per_seq == 0)
        def _():
            pbuf[0:SUBLANES, :] = jnp.zeros((SUBLANES, d), F32)
    yc = _shifted_conv(pbuf, p, cw_ref, rows)
    tail = pbuf[rows:rows + SUBLANES, :]
    if sample:
        pst_ref[...] = p
    else:
        pst_ref[0] = tail
        pbuf[0:SUBLANES, :] = tail
    ya = _dot((zc[:, 2 * d:3 * d] * yc).astype(BF16), wco_ref[...])

    zg = _dot(u, wgab_ref[...])
    ma_ref[...] = jax.nn.sigmoid(zg[:, 0:d]) * ya
    sgb_ref[...] = jax.nn.sigmoid(zg[:, d:2 * d])

    cos, sa, sb = cos_ref[...], sa_ref[...], sb_ref[...]
    zq = _dot(u, wq_ref[...])
    scale = HEAD_DIM ** -0.5
    for j in range(zq.shape[1] // LANES):
        blk = _rope_block(zq[:, j * LANES:(j + 1) * LANES], cos, sa, sb)
        q_ref[:, j * LANES:(j + 1) * LANES] = (blk * scale).astype(BF16)

    zkv = _dot(u, wkv_ref[...])
    per_branch = KV_W // LANES
    outs = (cmp_ref, slc_ref, win_ref)
    for j in range(zkv.shape[1] // LANES):
        br, jj = divmod(j, per_branch)
        blk = zkv[:, j * LANES:(j + 1) * LANES]
        if jj < per_branch // 2:
            blk = _rope_block(blk, cos, sa, sb)
        outs[br][:, jj * LANES:(jj + 1) * LANES] = blk
        if br > 0:
            kvb_ref[:, (j - per_branch) * LANES:(j - per_branch + 1) * LANES] = blk.astype(BF16)

    gl_ref[...] = jax.nn.sigmoid(_dot(u, wgl_ref[...]))


def _front(x, g, tabs, w, conv_w, state, seq_rows):
    n, d = x.shape
    sample = state is not None
    tm = n if sample else ROW_TILE
    tiles_per_seq = seq_rows // tm
    nt = n // tm
    n_seq = n // seq_rows
    row = lambda w_: pl.BlockSpec((tm, w_), lambda i: (i, 0))
    tab = pl.BlockSpec((tm, LANES), lambda i: (i % tiles_per_seq, 0))
    args = [x, g, *tabs, *w[:5], conv_w, w[5]]
    in_specs = [row(d), _const_spec((1, d)), tab, tab, tab] + [_const_spec(a.shape) for a in w[:5]] + [
        _const_spec(conv_w.shape), _const_spec(w[5].shape)]
    if sample:
        args.append(state)
        in_specs.append(row(d))
        pst_shape, pst_spec = jax.ShapeDtypeStruct((n, d), F32), row(d)
    else:
        pst_shape = jax.ShapeDtypeStruct((n_seq, SUBLANES, d), F32)
        pst_spec = pl.BlockSpec((1, SUBLANES, d), lambda i: (i // tiles_per_seq, 0, 0))
    out_shape = [jax.ShapeDtypeStruct((n, d), F32), jax.ShapeDtypeStruct((n, d), F32),
                 jax.ShapeDtypeStruct((n, N_HEADS * HEAD_DIM), BF16),
                 jax.ShapeDtypeStruct((n, KV_W), F32), jax.ShapeDtypeStruct((n, KV_W), F32),
                 jax.ShapeDtypeStruct((n, KV_W), F32), jax.ShapeDtypeStruct((n, 2 * KV_W), BF16),
                 jax.ShapeDtypeStruct((n, 2 * LANES), F32), pst_shape]
    out_specs = [row(d), row(d), row(N_HEADS * HEAD_DIM), row(KV_W), row(KV_W), row(KV_W), row(2 * KV_W),
                 row(2 * LANES), pst_spec]
    return pl.pallas_call(
        functools.partial(_front_kernel, sample, tiles_per_seq),
        grid=(nt,), in_specs=in_specs, out_specs=out_specs, out_shape=out_shape,
        scratch_shapes=[pltpu.VMEM((tm + SUBLANES, d), F32)],
        compiler_params=_cparams(("arbitrary",)), name="front_sample" if sample else "front_prompt",
    )(*args)


def _cmp_bias_kernel(pe_ref, w1_ref, b1_ref, o_ref):
    for kv in range(2):
        a = _dot(pe_ref[kv], w1_ref[kv])
        o_ref[kv] = jnp.broadcast_to(b1_ref[kv] + a[0:1, 0:LANES] + a[1:2, LANES:2 * LANES], (SUBLANES, LANES))


def _compress_tile(rows_refs, w1_ref, hb_ref, w2_ref, b2_ref, carry_ref, sh_ref, out_ref):
    nck = rows_refs[0].shape[0] // CMP_STRIDE
    for j, rows_ref in enumerate(rows_refs):
        kv = j // 2
        lhs = jnp.concatenate(
            [rows_ref[pl.ds(s, nck, stride=CMP_STRIDE), :].astype(BF16) for s in range(CMP_STRIDE)],
            axis=1)
        a = _dot(lhs, w1_ref[kv])
        a0 = a[:, 0:LANES]
        sh_ref[SUBLANES:SUBLANES + nck, :] = a0
        sh_ref[0:SUBLANES, :] = carry_ref[j]
        hid = sh_ref[SUBLANES - 1:SUBLANES - 1 + nck, :] + a[:, LANES:2 * LANES] + hb_ref[kv][0:1, :]
        carry_ref[j] = sh_ref[nck:nck + SUBLANES, :]
        o = _dot(jax.nn.silu(hid).astype(BF16), w2_ref[kv]) + b2_ref[kv]
        out_ref[:, j * LANES:(j + 1) * LANES] = o.astype(out_ref.dtype)


def _compress_prompt_kernel(r0, r1, r2, r3, w1_ref, hb_ref, w2_ref, b2_ref, out_ref, carry_ref, sh_ref):
    @pl.when(pl.program_id(1) == 0)
    def _():
        carry_ref[...] = jnp.zeros(carry_ref.shape, F32)
    _compress_tile((r0, r1, r2, r3), w1_ref, hb_ref, w2_ref, b2_ref, carry_ref, sh_ref, out_ref.at[0])


def _page_copies(pt_ref, cache_ref, buf_ref, sem_ref, slot, b, grp, n_pages):
    page_rows = cache_ref.shape[1]
    copies = []
    for p in range(n_pages):
        page = pt_ref[b, grp * n_pages + p]
        dst_rows = pl.ds(p * page_rows, page_rows)
        if len(buf_ref.shape) == 3:
            copies.append(pltpu.make_async_copy(cache_ref.at[page], buf_ref.at[slot, dst_rows, :], sem_ref.at[slot]))
        else:
            for j in range(buf_ref.shape[1]):
                copies.append(pltpu.make_async_copy(cache_ref.at[page, :, pl.ds(j * LANES, LANES)],
                                                    buf_ref.at[slot, j, dst_rows, :], sem_ref.at[slot]))
    return copies


def _gather_step(pt_ref, cache_ref, buf_ref, sem_ref, n_pages):
    b, g = pl.program_id(0), pl.program_id(1)
    nb, ng = pl.num_programs(0), pl.num_programs(1)
    step = b * ng + g
    slot = step % 2

    @pl.when(step == 0)
    def _():
        for c in _page_copies(pt_ref, cache_ref, buf_ref, sem_ref, 0, 0, 0, n_pages):
            c.start()

    @pl.when(step + 1 < nb * ng)
    def _():
        nxt = step + 1
        for c in _page_copies(pt_ref, cache_ref, buf_ref, sem_ref, 1 - slot, nxt // ng, nxt % ng, n_pages):
            c.start()

    for c in _page_copies(pt_ref, cache_ref, buf_ref, sem_ref, slot, b, g, n_pages):
        c.wait()
    return slot


def _compress_sample_kernel(pt_ref, cache_ref, w1_ref, hb_ref, w2_ref, b2_ref, out_ref,
                            buf_ref, sem_ref, carry_ref, sh_ref):
    n_pages = buf_ref.shape[2] // cache_ref.shape[1]
    slot = _gather_step(pt_ref, cache_ref, buf_ref, sem_ref, n_pages)

    @pl.when(pl.program_id(1) == 0)
    def _():
        carry_ref[...] = jnp.zeros(carry_ref.shape, F32)
    planes = tuple(buf_ref.at[slot, j] for j in range(buf_ref.shape[1]))
    _compress_tile(planes, w1_ref, hb_ref, w2_ref, b2_ref, carry_ref, sh_ref, out_ref.at[0])


def _compress_weights(cmp_pe, cmp_w1, cmp_b1, cmp_w2, cmp_b2):
    r = CMP_LEN // CMP_STRIDE
    eye2 = jnp.eye(2, dtype=F32)
    w1 = cmp_w1.reshape(2, r, CMP_STRIDE, HEAD_DIM, HEAD_DIM)
    w1bd = jnp.einsum("krsde,hg->kshdrge", w1, eye2).reshape(2, CMP_STRIDE * LANES, r * LANES).astype(BF16)
    w2bd = jnp.einsum("kde,hg->khdge", cmp_w2, eye2).reshape(2, LANES, LANES).astype(BF16)
    pe = cmp_pe.reshape(2, r, CMP_STRIDE, 1, HEAD_DIM)
    pe = jnp.broadcast_to(pe, (2, r, CMP_STRIDE, 2, HEAD_DIM)).reshape(2, r, CMP_STRIDE * LANES)
    pe = jnp.pad(pe, ((0, 0), (0, SUBLANES - r), (0, 0))).astype(BF16)
    b1 = jnp.tile(cmp_b1, (1, 2)).reshape(2, 1, LANES)
    b2 = jnp.tile(cmp_b2, (1, 2)).reshape(2, 1, LANES)
    hb = pl.pallas_call(
        _cmp_bias_kernel, out_shape=jax.ShapeDtypeStruct((2, SUBLANES, LANES), F32), name="compress_bias",
    )(pe, w1bd, b1)
    return w1bd, hb, w2bd, b2


def _compress_prompt(cmp_rows, cw, n_seq):
    n = cmp_rows.shape[0]
    s = n // n_seq
    nck = CMP_ROWS // CMP_STRIDE
    tiles = s // CMP_ROWS
    w1bd, hb, w2bd, b2 = cw
    return pl.pallas_call(
        _compress_prompt_kernel,
        grid=(n_seq, tiles),
        in_specs=[pl.BlockSpec((CMP_ROWS, LANES), functools.partial(lambda j, b, t: (b * tiles + t, j), j))
                  for j in range(KV_W // LANES)] + [
                  _const_spec(w1bd.shape), _const_spec(hb.shape), _const_spec(w2bd.shape), _const_spec(b2.shape)],
        out_specs=pl.BlockSpec((1, nck, KV_W), lambda b, t: (b, t, 0)),
        out_shape=jax.ShapeDtypeStruct((n_seq, s // CMP_STRIDE, KV_W), BF16),
        scratch_shapes=[pltpu.VMEM((KV_W // LANES, SUBLANES, LANES), F32),
                        pltpu.VMEM((nck + SUBLANES, LANES), F32)],
        compiler_params=_cparams(("arbitrary", "arbitrary")), name="compress_prompt",
    )(cmp_rows, cmp_rows, cmp_rows, cmp_rows, w1bd, hb, w2bd, b2)


def _compress_sample(page_table, cache, cw):
    db, n_pages = page_table.shape
    page_rows = cache.shape[1]
    pages_per = CMP_ROWS // page_rows
    groups = n_pages // pages_per
    nck = CMP_ROWS // CMP_STRIDE
    w1bd, hb, w2bd, b2 = cw
    cspec = lambda a: pl.BlockSpec(a.shape, lambda b, g, pt: (0,) * a.ndim, pipeline_mode=pl.Buffered(1))
    grid_spec = pltpu.PrefetchScalarGridSpec(
        num_scalar_prefetch=1, grid=(db, groups),
        in_specs=[pl.BlockSpec(memory_space=pl.ANY), cspec(w1bd), cspec(hb), cspec(w2bd), cspec(b2)],
        out_specs=pl.BlockSpec((1, nck, KV_W), lambda b, g, pt: (b, g, 0)),
        scratch_shapes=[pltpu.VMEM((2, KV_W // LANES, CMP_ROWS, LANES), F32), pltpu.SemaphoreType.DMA((2,)),
                        pltpu.VMEM((KV_W // LANES, SUBLANES, LANES), F32),
                        pltpu.VMEM((nck + SUBLANES, LANES), F32)])
    return pl.pallas_call(
        _compress_sample_kernel, grid_spec=grid_spec,
        out_shape=jax.ShapeDtypeStruct((db, n_pages * page_rows // CMP_STRIDE, KV_W), BF16),
        compiler_params=_cparams(("arbitrary", "arbitrary")), name="compress_sample",
    )(page_table, cache.reshape(cache.shape[0], page_rows, KV_W), w1bd, hb, w2bd, b2)


def _masked_softmax(s, mask):
    s = jnp.where(mask, s, -jnp.inf)
    m = jnp.max(s, axis=-1, keepdims=True)
    m = jnp.where(m == -jnp.inf, 0.0, m)
    e = jnp.where(mask, jnp.exp(s - m), 0.0)
    return e / jnp.maximum(jnp.sum(e, axis=-1, keepdims=True), 1e-30)


def _topk_not_selected(imp, n_pick):
    rows, width = imp.shape
    lane = lax.broadcasted_iota(jnp.int32, (rows, width), 1).astype(F32)

    def pick(_, carry):
        imp, notsel = carry
        m = jnp.max(imp, axis=-1, keepdims=True)
        first = jnp.min(jnp.where(imp == m, lane, float(width)), axis=-1, keepdims=True)
        hit = lane == first
        return jnp.where(hit, -jnp.inf, imp), jnp.where(hit, 0.0, notsel)

    _, notsel = lax.fori_loop(0, n_pick, pick, (imp, jnp.ones((rows, width), F32)))
    return notsel


def _flash_update(carry, s, v):
    m, l, acc = carry
    m_new = jnp.maximum(m, jnp.max(s, axis=-1, keepdims=True))
    alpha = jnp.exp(m - m_new)
    p = jnp.exp(s - m_new)
    l = alpha * l + jnp.sum(p, axis=-1, keepdims=True)
    acc = alpha * acc + _dot(p.astype(v.dtype), v)
    return m_new, l, acc


def _half_mask(shape):
    return lax.broadcasted_iota(jnp.int32, shape, len(shape) - 1) < HEAD_DIM


def _stack_heads(q_blk, nr):
    first = _half_mask((q_blk.shape[0], LANES))
    zero = jnp.zeros((), q_blk.dtype)
    parts = []
    for gl in range(2):
        for r in range(nr):
            blk = q_blk[:, r * LANES:(r + 1) * LANES]
            parts.append(jnp.where(first if gl == 0 else ~first, blk, zero))
    return jnp.concatenate(parts, axis=0)


def _merge_pair(o, nr, t):
    first = _half_mask((t, LANES))
    return [jnp.where(first, o[r * t:(r + 1) * t], o[(nr + r) * t:(nr + r + 1) * t]) for r in range(nr)]


def _gate_vec(g_ref, br, r, t):
    c = br * 2 * (N_HEADS // N_KV_HEADS) + r * 2
    return jnp.where(_half_mask((t, LANES)), g_ref[:, c:c + 1], g_ref[:, c + 1:c + 2])


def _forced_importance(imp, blk, t_pos):
    cur = t_pos // SLC_LEN
    valid = blk <= cur
    forced = (blk == 0) | (valid & (blk > cur - N_LOCAL))
    return jnp.where(forced, FORCED_SCORE, jnp.where(valid, imp, -FORCED_SCORE))


def _attend_prompt_kernel(q_ref, g_ref, kc_ref, vc_ref, ks_ref, vs_ref, kw_ref, vw_ref, e_ref, ovl_ref, o_ref):
    i = pl.program_id(2)
    nr = N_HEADS // N_KV_HEADS
    t = Q_TILE
    rows = 2 * nr * t
    t0 = i * t
    tq = t0 + lax.broadcasted_iota(jnp.int32, (t, 1), 0)
    tq_rows = jnp.concatenate([tq] * (2 * nr), axis=0)
    qs = _stack_heads(q_ref[...], nr)

    ncmp = kc_ref.shape[1]
    cidx = lax.broadcasted_iota(jnp.int32, (1, ncmp), 1)
    c_mask = (cidx >= 1) & (cidx * CMP_STRIDE + (CMP_LEN - CMP_STRIDE - 1) <= tq_rows)
    p_c = _masked_softmax(_dot_nt(qs, kc_ref[0]), c_mask)
    o_c = _dot(p_c.astype(BF16), vc_ref[0])

    nslc = ovl_ref.shape[1]
    blk = lax.broadcasted_iota(jnp.int32, (1, nslc), 1)
    notsel = []
    for gl in range(2):
        psum = p_c[gl * nr * t:gl * nr * t + t]
        for r in range(1, nr):
            psum = psum + p_c[(gl * nr + r) * t:(gl * nr + r + 1) * t]
        imp = jnp.dot(psum, ovl_ref[...], preferred_element_type=F32, precision=lax.Precision.HIGHEST)
        ns = _topk_not_selected(_forced_importance(imp, blk, tq), min(N_SEL, nslc)).astype(BF16)
        notsel += [ns] * nr
    qa = jnp.concatenate([qs, jnp.concatenate(notsel, axis=0)], axis=1)

    def scores(j):
        off = pl.multiple_of(j * K_TILE, K_TILE)
        ka = jnp.concatenate([ks_ref[pl.ds(off, K_TILE), :], e_ref[pl.ds(off, K_TILE), :]], axis=1)
        return _dot_nt(qa, ka), vs_ref[pl.ds(off, K_TILE), :]

    def body(j, carry):
        s, v = scores(j)
        return _flash_update(carry, s, v)

    last = (t0 + t - 1) // K_TILE
    init = (jnp.full((rows, 1), -jnp.inf, F32), jnp.zeros((rows, 1), F32), jnp.zeros((rows, LANES), F32))
    carry = lax.fori_loop(0, last, body, init)
    s, v = scores(last)
    kpos = last * K_TILE + lax.broadcasted_iota(jnp.int32, (1, K_TILE), 1)
    _, l, acc = _flash_update(carry, jnp.where(kpos <= tq_rows, s, -MASK_BIG), v)
    o_s = acc / l

    wlen = WINDOW + t
    ws = pl.multiple_of(jnp.maximum(t0 - WINDOW, 0), t)
    wpos = ws + lax.broadcasted_iota(jnp.int32, (1, wlen), 1)
    w_mask = (wpos <= tq_rows) & (wpos > tq_rows - WINDOW)
    p_w = _masked_softmax(_dot_nt(qs, kw_ref[pl.ds(ws, wlen), :]), w_mask)
    o_w = _dot(p_w.astype(BF16), vw_ref[pl.ds(ws, wlen), :])

    oc, os_, ow = _merge_pair(o_c, nr, t), _merge_pair(o_s, nr, t), _merge_pair(o_w, nr, t)
    for r in range(nr):
        o = _gate_vec(g_ref, 0, r, t) * oc[r] + _gate_vec(g_ref, 1, r, t) * os_[r] + _gate_vec(g_ref, 2, r, t) * ow[r]
        o_ref[:, r * LANES:(r + 1) * LANES] = o.astype(o_ref.dtype)


def _attend_prompt(q, gates, kcvc, kvb, econst, ovl, n_seq):
    n = q.shape[0]
    s = n // n_seq
    nq = s // Q_TILE
    nr = N_HEADS // N_KV_HEADS
    ncmp = kcvc.shape[1]
    qw = nr * LANES
    seq_blk = lambda c0: pl.BlockSpec((s, LANES), lambda b, k, i: (b, c0 + k))
    return pl.pallas_call(
        _attend_prompt_kernel,
        grid=(n_seq, 2, nq),
        in_specs=[pl.BlockSpec((Q_TILE, qw), lambda b, k, i: (b * nq + i, k)),
                  pl.BlockSpec((Q_TILE, LANES), lambda b, k, i: (b * nq + i, k)),
                  pl.BlockSpec((1, ncmp, LANES), lambda b, k, i: (b, 0, k)),
                  pl.BlockSpec((1, ncmp, LANES), lambda b, k, i: (b, 0, 2 + k)),
                  seq_blk(0), seq_blk(2), seq_blk(4), seq_blk(6),
                  pl.BlockSpec(econst.shape, lambda b, k, i: (0, 0), pipeline_mode=pl.Buffered(1)),
                  pl.BlockSpec(ovl.shape, lambda b, k, i: (0, 0), pipeline_mode=pl.Buffered(1))],
        out_specs=pl.BlockSpec((Q_TILE, qw), lambda b, k, i: (b * nq + i, k)),
        out_shape=jax.ShapeDtypeStruct((n, N_HEADS * HEAD_DIM), BF16),
        compiler_params=_cparams(("arbitrary", "arbitrary", "arbitrary")), name="attend_prompt",
    )(q, gates, kcvc, kcvc, kvb, kvb, kvb, kvb, econst, ovl)


def _attend_sample_kernel(past_len, pt_ref, q_ref, g_ref, kcvc_ref, cache_ref, slc_ref, wcache_ref, win_ref,
                          e_ref, ovl_ref, o_ref, buf_ref, sem_ref, ns_ref, oc_ref, m_ref, l_ref, acc_ref):
    g = pl.program_id(1)
    ng = pl.num_programs(1)
    nr = N_HEADS // N_KV_HEADS
    t = SAMPLE_ROWS
    rows = 2 * nr * t
    n_pages = buf_ref.shape[1] // cache_ref.shape[1]
    keys = buf_ref.shape[1]
    slot = _gather_step(pt_ref, cache_ref, buf_ref, sem_ref, n_pages)
    tq = past_len - (t - 4) + lax.broadcasted_iota(jnp.int32, (t, 1), 0)
    tq_rows = jnp.concatenate([tq] * (2 * nr), axis=0)
    qs = [_stack_heads(q_ref[:, k * nr * LANES:(k + 1) * nr * LANES], nr) for k in range(2)]

    @pl.when(g == 0)
    def _():
        ncmp = kcvc_ref.shape[1]
        cidx = lax.broadcasted_iota(jnp.int32, (1, ncmp), 1)
        c_mask = (cidx >= 1) & (cidx * CMP_STRIDE + (CMP_LEN - CMP_STRIDE - 1) <= tq_rows)
        width = ovl_ref.shape[1]
        lane = lax.broadcasted_iota(jnp.int32, (1, width), 1)
        per = keys // SLC_LEN
        blk = (lane // LANES) * per + lane % LANES
        real = (lane % LANES < per) & (blk <= past_len // SLC_LEN)
        for k in range(2):
            p_c = _masked_softmax(_dot_nt(qs[k], kcvc_ref[0, :, k * LANES:(k + 1) * LANES]), c_mask)
            oc_ref[k] = _dot(p_c.astype(BF16), kcvc_ref[0, :, (2 + k) * LANES:(3 + k) * LANES])
            for gl in range(2):
                psum = p_c[gl * nr * t:gl * nr * t + t]
                for r in range(1, nr):
                    psum = psum + p_c[(gl * nr + r) * t:(gl * nr + r + 1) * t]
                imp = jnp.dot(psum, ovl_ref[...], preferred_element_type=F32, precision=lax.Precision.HIGHEST)
                imp = jnp.where(real, _forced_importance(imp, blk, tq), -jnp.inf)
                ns = _topk_not_selected(imp, N_SEL)
                for r in range(nr):
                    ns_ref[k, (gl * nr + r) * t:(gl * nr + r + 1) * t, :] = ns
        m_ref[...] = jnp.full(m_ref.shape, -jnp.inf, F32)
        l_ref[...] = jnp.zeros(l_ref.shape, F32)
        acc_ref[...] = jnp.zeros(acc_ref.shape, F32)

    page = buf_ref.at[slot]
    e = e_ref[0:keys, :]
    for k in range(2):
        ns = ns_ref[k, :, pl.ds(pl.multiple_of(g * LANES, LANES), LANES)]
        qa = jnp.concatenate([qs[k], ns.astype(BF16)], axis=1)
        ka = jnp.concatenate([page[:, k * LANES:(k + 1) * LANES].astype(BF16), e], axis=1)
        v = page[:, (2 + k) * LANES:(3 + k) * LANES].astype(BF16)
        m_ref[k], l_ref[k], acc_ref[k] = _flash_update((m_ref[k], l_ref[k], acc_ref[k]), _dot_nt(qa, ka), v)

    @pl.when(g == ng - 1)
    def _():
        kidx = lax.broadcasted_iota(jnp.int32, (1, t), 1)
        kpos = past_len - (t - 4) + kidx
        new_ok = (kidx >= t - 4) & (kpos <= tq_rows)
        wb = wcache_ref.shape[1]
        wpos = past_len - wb + lax.broadcasted_iota(jnp.int32, (1, wb), 1)
        w_old = (wpos <= tq_rows) & (wpos > tq_rows - WINDOW) & (wpos >= 0)
        w_new = new_ok & (kpos > tq_rows - WINDOW)
        for k in range(2):
            kcol, vcol = slice(k * LANES, (k + 1) * LANES), slice((2 + k) * LANES, (3 + k) * LANES)
            qf = qs[k].astype(F32)
            s_new = jnp.where(new_ok, _dot_nt(qf, slc_ref[:, kcol]), -MASK_BIG)
            _, l, acc = _flash_update((m_ref[k], l_ref[k], acc_ref[k]), s_new, slc_ref[:, vcol])
            o_s = acc / l
            so = jnp.where(w_old, _dot_nt(qs[k], wcache_ref[0, :, kcol].astype(BF16)), -jnp.inf)
            sn = jnp.where(w_new, _dot_nt(qf, win_ref[:, kcol]), -jnp.inf)
            m = jnp.maximum(jnp.max(so, axis=-1, keepdims=True), jnp.max(sn, axis=-1, keepdims=True))
            m = jnp.where(m == -jnp.inf, 0.0, m)
            eo = jnp.where(w_old, jnp.exp(so - m), 0.0)
            en = jnp.where(w_new, jnp.exp(sn - m), 0.0)
            den = jnp.maximum(jnp.sum(eo, axis=-1, keepdims=True) + jnp.sum(en, axis=-1, keepdims=True), 1e-30)
            o_w = (_dot(eo.astype(BF16), wcache_ref[0, :, vcol].astype(BF16)) + _dot(en, win_ref[:, vcol])) / den
            oc, os_, ow = _merge_pair(oc_ref[k], nr, t), _merge_pair(o_s, nr, t), _merge_pair(o_w, nr, t)
            gk = g_ref.at[:, k * LANES:(k + 1) * LANES]
            for r in range(nr):
                o = _gate_vec(gk, 0, r, t) * oc[r] + _gate_vec(gk, 1, r, t) * os_[r] + _gate_vec(gk, 2, r, t) * ow[r]
                o_ref[:, (k * nr + r) * LANES:(k * nr + r + 1) * LANES] = o.astype(o_ref.dtype)


def _attend_sample(page_table, q, gates, kcvc, cache, slc_new, wcache, win_new, econst, ovl, past_len):
    db, n_pages = page_table.shape
    page_rows = cache.shape[1]
    groups = n_pages // PAGES_PER_STEP
    keys = PAGES_PER_STEP * page_rows
    nr = N_HEADS // N_KV_HEADS
    t = SAMPLE_ROWS
    rows = 2 * nr * t
    qw = N_HEADS * HEAD_DIM
    cmap = lambda nd: (lambda b, g, pt: (0,) * nd)
    grid_spec = pltpu.PrefetchScalarGridSpec(
        num_scalar_prefetch=1, grid=(db, groups),
        in_specs=[pl.BlockSpec((t, qw), lambda b, g, pt: (b, 0)),
                  pl.BlockSpec((t, 2 * LANES), lambda b, g, pt: (b, 0)),
                  pl.BlockSpec((1,) + kcvc.shape[1:], lambda b, g, pt: (b, 0, 0)),
                  pl.BlockSpec(memory_space=pl.ANY),
                  pl.BlockSpec((t, KV_W), lambda b, g, pt: (b, 0)),
                  pl.BlockSpec((1,) + wcache.shape[1:], lambda b, g, pt: (b, 0, 0)),
                  pl.BlockSpec((t, KV_W), lambda b, g, pt: (b, 0)),
                  pl.BlockSpec(econst.shape, cmap(2), pipeline_mode=pl.Buffered(1)),
                  pl.BlockSpec(ovl.shape, cmap(2), pipeline_mode=pl.Buffered(1))],
        out_specs=pl.BlockSpec((t, qw), lambda b, g, pt: (b, 0)),
        scratch_shapes=[pltpu.VMEM((2, keys, KV_W), F32), pltpu.SemaphoreType.DMA((2,)),
                        pltpu.VMEM((2, rows, ovl.shape[1]), F32), pltpu.VMEM((2, rows, LANES), F32),
                        pltpu.VMEM((2, rows, 1), F32), pltpu.VMEM((2, rows, 1), F32),
                        pltpu.VMEM((2, rows, LANES), F32)])
    return pl.pallas_call(
        functools.partial(_attend_sample_kernel, past_len), grid_spec=grid_spec,
        out_shape=jax.ShapeDtypeStruct((db * t, qw), BF16),
        compiler_params=_cparams(("arbitrary", "arbitrary")), name="attend_sample",
    )(page_table, q, gates, kcvc, cache.reshape(cache.shape[0], page_rows, KV_W), slc_new, wcache, win_new,
      econst, ovl)


def _post_kernel(sample, tiles_per_seq, *refs):
    if sample:
        (x_ref, o_ref, ma_ref, sgb_ref, wao_ref, wo_ref, gf_ref, wfi_ref, fcw_ref, wfd_ref, gfin_ref, st_ref,
         y_ref, ast_ref, abuf) = refs
    else:
        (x_ref, o_ref, ma_ref, sgb_ref, wao_ref, wo_ref, gf_ref, wfi_ref, fcw_ref, wfd_ref, gfin_ref,
         y_ref, ast_ref, abuf) = refs
    rows = x_ref.shape[0]
    dff = fcw_ref.shape[1]
    m = ma_ref[...] + sgb_ref[...] * _dot(o_ref[...], wao_ref[...])
    h = x_ref[...] + _dot(m.astype(BF16), wo_ref[...])
    ag = _dot(_rms(h, gf_ref[...]).astype(BF16), wfi_ref[...])
    a = ag[:, 0:dff]
    if sample:
        row = lax.broadcasted_iota(jnp.int32, (rows, 1), 0) % SAMPLE_ROWS
        is_state = (row >= SAMPLE_ROWS - 4 - (CONV_W - 1)) & (row < SAMPLE_ROWS - 4)
        a = jnp.where(is_state, st_ref[...], a)
        abuf[0:SUBLANES, :] = jnp.zeros((SUBLANES, dff), F32)
    else:
        @pl.when(pl.program_id(0) % tiles_per_seq == 0)
        def _():
            abuf[0:SUBLANES, :] = jnp.zeros((SUBLANES, dff), F32)
    ac = _shifted_conv(abuf, a, fcw_ref, rows)
    tail = abuf[rows:rows + SUBLANES, :]
    if sample:
        ast_ref[...] = a
    else:
        ast_ref[0] = tail
        abuf[0:SUBLANES, :] = tail
    hh = h + _dot((jax.nn.silu(ac) * ag[:, dff:2 * dff]).astype(BF16), wfd_ref[...])
    y_ref[...] = _rms(hh, gfin_ref[...])


def _post(x, o, ma, sgb, w, state, seq_rows):
    n, d = x.shape
    sample = state is not None
    tm = n if sample else ROW_TILE
    tiles_per_seq = seq_rows // tm
    n_seq = n // seq_rows
    wao, wo, gf, wfi, fcw, wfd, gfin = w
    dff = fcw.shape[1]
    row = lambda w_: pl.BlockSpec((tm, w_), lambda i: (i, 0))
    args = [x, o, ma, sgb, *w]
    in_specs = [row(d), row(o.shape[1]), row(d), row(d)] + [_const_spec(a.shape) for a in w]
    if sample:
        args.append(state)
        in_specs.append(row(dff))
        ast_shape, ast_spec = jax.ShapeDtypeStruct((n, dff), F32), row(dff)
    else:
        ast_shape = jax.ShapeDtypeStruct((n_seq, SUBLANES, dff), F32)
        ast_spec = pl.BlockSpec((1, SUBLANES, dff), lambda i: (i // tiles_per_seq, 0, 0))
    return pl.pallas_call(
        functools.partial(_post_kernel, sample, tiles_per_seq),
        grid=(n // tm,), in_specs=in_specs, out_specs=[row(d), ast_spec],
        out_shape=[jax.ShapeDtypeStruct((n, d), F32), ast_shape],
        scratch_shapes=[pltpu.VMEM((tm + SUBLANES, dff), F32)],
        compiler_params=_cparams(("arbitrary",)), name="post_sample" if sample else "post_prompt",
    )(*args)


def _pair_head_order():
    nr = N_HEADS // N_KV_HEADS
    return [2 * nr * k + nr * half + r for k in range(2) for r in range(nr) for half in range(2)]


def _rope_tables(pos):
    half = ROT_DIM // 2
    inv_freq = jnp.power(ROPE_THETA, -jnp.arange(half, dtype=F32) * (2.0 / ROT_DIM))
    ang = pos.astype(F32)[:, None] * inv_freq[None, :]
    cos, sin = jnp.cos(ang), jnp.sin(ang)
    n = pos.shape[0]
    one, zero = jnp.ones((n, HEAD_DIM - ROT_DIM), F32), jnp.zeros((n, HEAD_DIM - ROT_DIM), F32)
    zh = jnp.zeros((n, half), F32)
    tabs = (jnp.concatenate([cos, cos, one], 1), jnp.concatenate([zh, sin, zero], 1),
            jnp.concatenate([-sin, zh, zero], 1))
    return tuple(jnp.tile(a, (1, LANES // HEAD_DIM)) for a in tabs)


def _overlap_matrix(n_rows, n_cols, col_block):
    c = np.arange(n_rows)[:, None] - 1
    blk = col_block[None, :]
    cs, ss = c * CMP_STRIDE, blk * SLC_LEN
    return ((c >= 0) & (blk >= 0) & (cs < ss + SLC_LEN) & (cs + CMP_LEN > ss)).astype(np.float32)


def _block_onehot(n_keys):
    e = (np.arange(n_keys)[:, None] // SLC_LEN == np.arange(LANES)[None, :]).astype(np.float32)
    return jnp.asarray(-MASK_BIG * e, dtype=BF16)


def _split_w_in(w_in, d):
    conv_dim = d
    q_dim = N_HEADS * HEAD_DIM
    kv_dim = N_BRANCH * KV_W
    o0 = 3 * conv_dim
    wc = w_in[:, 0:o0]
    wq = w_in[:, o0:o0 + q_dim].reshape(d, N_HEADS, HEAD_DIM)[:, np.array(_pair_head_order())].reshape(d, q_dim)
    o1 = o0 + q_dim
    wkv = w_in[:, o1:o1 + kv_dim]
    o2 = o1 + kv_dim
    wl = w_in[:, o2:o2 + N_BRANCH * N_HEADS].reshape(d, N_HEADS, N_BRANCH)
    wl = wl[:, np.array(_pair_head_order())].reshape(d, 2, N_HEADS // 2, N_BRANCH).transpose(0, 1, 3, 2)
    wl = jnp.pad(wl.reshape(d, 2, N_BRANCH * N_HEADS // 2), ((0, 0), (0, 0), (0, LANES - N_BRANCH * N_HEADS // 2)))
    wl = wl.reshape(d, 2 * LANES)
    o3 = o2 + N_BRANCH * N_HEADS
    wgab = w_in[:, o3:o3 + 2 * d]
    return [a.astype(BF16) for a in (wc, wq, wkv, wl, wgab)]


def kernel(x_prompt, x_sample, cache_cmp_kv, cache_slc_kv, cache_win_kv, state_conv_mix, state_conv_ffn,
           page_table, norm_mix_g, w_in, conv_mix_w, w_conv_out, cmp_pe, cmp_w1, cmp_b1, cmp_w2, cmp_b2,
           w_attn_out, w_out, norm_ffn_g, w_ff_in, ff_conv_w, w_ff_down, norm_final_g):
    depth = w_in.shape[0]
    assert depth == 1, "single-layer step"
    b, s, d = x_prompt.shape
    db, t, _ = x_sample.shape
    page_rows = cache_cmp_kv.shape[2]
    past_len = page_table.shape[1] * page_rows
    assert t == 4 and s % K_TILE == 0 and s % CMP_ROWS == 0 and s >= WINDOW + Q_TILE
    assert past_len % (PAGES_PER_STEP * page_rows) == 0 and cache_win_kv.shape[2] == WINDOW
    assert N_SEL <= s // SLC_LEN <= LANES, "prompt selection blocks fit one lane block"
    dff = ff_conv_w.shape[2]
    l = 0

    order = np.array(_pair_head_order())
    front_w = _split_w_in(w_in[l], d) + [w_conv_out[l].astype(BF16)]
    post_w = [w_attn_out[l].reshape(N_HEADS, HEAD_DIM, d)[order].reshape(N_HEADS * HEAD_DIM, d).astype(BF16),
              w_out[l].astype(BF16), norm_ffn_g[l].reshape(1, d), w_ff_in[l].astype(BF16), ff_conv_w[l],
              w_ff_down[l].astype(BF16), norm_final_g.reshape(1, d)]
    g_mix = norm_mix_g[l].reshape(1, d)
    cw = _compress_weights(cmp_pe[l], cmp_w1[l], cmp_b1[l], cmp_w2[l], cmp_b2[l])
    econst = _block_onehot(max(s, PAGES_PER_STEP * page_rows))

    xp = x_prompt.reshape(b * s, d)
    ma, sgb, q, cmp_p, slc_p, win_p, kvb, gates, pst = _front(
        xp, g_mix, _rope_tables(jnp.arange(s, dtype=jnp.int32)), front_w, conv_mix_w[l], None, s)
    kcvc = _compress_prompt(cmp_p, cw, b)
    ovl_p = jnp.asarray(_overlap_matrix(s // CMP_STRIDE, LANES, np.arange(LANES)))
    o = _attend_prompt(q, gates, kcvc, kvb, econst, ovl_p, b)
    y_p, ast = _post(xp, o, ma, sgb, post_w, None, s)
    kv_shape = (2, N_KV_HEADS, HEAD_DIM)
    wb_p = min(WINDOW, s)
    out_prompt = (
        y_p.reshape(b, s, d),
        cmp_p.reshape((1, b, s) + kv_shape), slc_p.reshape((1, b, s) + kv_shape),
        win_p.reshape((b, s) + kv_shape)[None, :, s - wb_p:],
        pst[None, :, SUBLANES - (CONV_W - 1):], ast[None, :, SUBLANES - (CONV_W - 1):])

    r8 = SAMPLE_ROWS
    pad_rows = lambda a, lo: jnp.pad(a, ((0, 0), (lo, r8 - lo - a.shape[1]), (0, 0))).reshape(db * r8, a.shape[2])
    xs = pad_rows(x_sample, r8 - t)
    st_mix = pad_rows(state_conv_mix[l], r8 - t - (CONV_W - 1))
    st_ffn = pad_rows(state_conv_ffn[l], r8 - t - (CONV_W - 1))
    pos_s = past_len - (r8 - t) + jnp.arange(r8, dtype=jnp.int32)
    tabs_s = tuple(jnp.tile(a, (db, 1)) for a in _rope_tables(pos_s))
    ma_s, sgb_s, q_s, cmp_s, slc_s, win_s, _, gates_s, p_s = _front(
        xs, g_mix, tabs_s, front_w, conv_mix_w[l], st_mix, db * r8)
    kcvc_s = _compress_sample(page_table, cache_cmp_kv[l], cw)
    keys_per_step = PAGES_PER_STEP * page_rows
    per = keys_per_step // SLC_LEN
    n_slots = past_len // keys_per_step + 1
    lane = np.arange(n_slots * LANES)
    col_block = np.where(lane % LANES < per, (lane // LANES) * per + lane % LANES, -1)
    col_block = np.where(col_block <= past_len // SLC_LEN, col_block, -1)
    ovl_s = jnp.asarray(_overlap_matrix(past_len // CMP_STRIDE, n_slots * LANES, col_block))
    wcache = cache_win_kv[l].reshape(db, WINDOW, KV_W)
    o_s = _attend_sample(page_table, q_s, gates_s, kcvc_s, cache_slc_kv[l], slc_s, wcache, win_s, econst, ovl_s,
                         past_len)
    y_s, a_s = _post(xs, o_s, ma_s, sgb_s, post_w, st_ffn, db * r8)
    tok = lambda a: a.reshape(db, r8, -1)[:, r8 - t:]
    win_new = jnp.concatenate([wcache, tok(win_s)], axis=1)[:, -WINDOW:]
    out_sample = (
        tok(y_s),
        tok(cmp_s).reshape((1, db, t) + kv_shape), tok(slc_s).reshape((1, db, t) + kv_shape),
        win_new.reshape((1, db, WINDOW) + kv_shape),
        p_s.reshape(db, r8, d)[None, :, r8 - (CONV_W - 1):], a_s.reshape(db, r8, dff)[None, :, r8 - (CONV_W - 1):])

    return (out_prompt[0], out_sample[0], out_prompt[1], out_prompt[2], out_prompt[3], out_prompt[4],
            out_prompt[5], out_sample[1], out_sample[2], out_sample[3], out_sample[4], out_sample[5])
```

```python
import functools

import numpy as np
import jax
import jax.numpy as jnp
from jax import lax
from jax.experimental import pallas as pl
from jax.experimental.pallas import tpu as pltpu

F32 = jnp.float32
BF16 = jnp.bfloat16

N_HEADS = 16
HEAD_DIM = 64
N_KV_HEADS = 4
N_BRANCH = 3
ROT_DIM = 16
ROPE_THETA = 500000.0
CMP_LEN = 32
CMP_STRIDE = 16
SLC_LEN = 64
N_SEL = 16
N_LOCAL = 2
WINDOW = 512
CONV_W = 3
NORM_EPS = 1e-6
FORCED_SCORE = 1e9

LANES = 128
SUBLANES = 8
KV_W = 2 * N_KV_HEADS * HEAD_DIM
MASK_BIG = 2.0 ** 100
VMEM_LIMIT = 56 * 1024 * 1024

ROW_TILE = 256
CMP_ROWS = 2048
Q_TILE = 128
K_TILE = 512
SAMPLE_ROWS = 8
PAGES_PER_STEP = 32


def _cparams(sem):
    return pltpu.CompilerParams(dimension_semantics=sem, vmem_limit_bytes=VMEM_LIMIT)


def _const_spec(shape):
    nd = len(shape)
    return pl.BlockSpec(shape, lambda *_: (0,) * nd, pipeline_mode=pl.Buffered(1))


def _dot(a, b):
    return jnp.dot(a, b, preferred_element_type=F32)


def _dot_nt(a, b):
    return lax.dot_general(a, b, (((1,), (1,)), ((), ())), preferred_element_type=F32)


def _rms(x, g):
    y = x * lax.rsqrt(jnp.mean(x * x, axis=-1, keepdims=True) + NORM_EPS)
    return y * g


def _rope_block(x, cos, sa, sb):
    return x * cos + pltpu.roll(x, 8, 1) * sa + pltpu.roll(x, LANES - 8, 1) * sb


def _shifted_conv(buf, p, w_ref, rows):
    buf[SUBLANES:SUBLANES + rows, :] = p
    p1 = buf[SUBLANES - 1:SUBLANES - 1 + rows, :]
    p2 = buf[SUBLANES - 2:SUBLANES - 2 + rows, :]
    return p2 * w_ref[0:1, :] + p1 * w_ref[1:2, :] + p * w_ref[2:3, :]


def _front_kernel(sample, tiles_per_seq, *refs):
    if sample:
        (x_ref, g_ref, cos_ref, sa_ref, sb_ref, wc_ref, wq_ref, wkv_ref, wgl_ref, wgab_ref, cw_ref, wco_ref,
         st_ref, ma_ref, sgb_ref, q_ref, cmp_ref, slc_ref, win_ref, kvb_ref, gl_ref, pst_ref, pbuf) = refs
    else:
        (x_ref, g_ref, cos_ref, sa_ref, sb_ref, wc_ref, wq_ref, wkv_ref, wgl_ref, wgab_ref, cw_ref, wco_ref,
         ma_ref, sgb_ref, q_ref, cmp_ref, slc_ref, win_ref, kvb_ref, gl_ref, pst_ref, pbuf) = refs
    rows, d = x_ref.shape
    u = _rms(x_ref[...], g_ref[...]).astype(BF16)

    zc = _dot(u, wc_ref[...])
    p = zc[:, d:2 * d] * zc[:, 0:d]
    if sample:
        row = lax.broadcasted_iota(jnp.int32, (rows, 1), 0) % SAMPLE_ROWS
        is_state = (row >= SAMPLE_ROWS - 4 - (CONV_W - 1)) & (row < SAMPLE_ROWS - 4)
        p = jnp.where(is_state, st_ref[...], p)
        pbuf[0:SUBLANES, :] = jnp.zeros((SUBLANES, d), F32)
    else:
        @pl.when(pl.program_id(0) % tiles_per_seq == 0)
        def _():
            pbuf[0:SUBLANES, :] = jnp.zeros((SUBLANES, d), F32)
    yc = _shifted_conv(pbuf, p, cw_ref, rows)
    tail = pbuf[rows:rows + SUBLANES, :]
    if sample:
        pst_ref[...] = p
    else:
        pst_ref[0] = tail
        pbuf[0:SUBLANES, :] = tail
    ya = _dot((zc[:, 2 * d:3 * d] * yc).astype(BF16), wco_ref[...])

    zg = _dot(u, wgab_ref[...])
    ma_ref[...] = jax.nn.sigmoid(zg[:, 0:d]) * ya
    sgb_ref[...] = jax.nn.sigmoid(zg[:, d:2 * d])

    cos, sa, sb = cos_ref[...], sa_ref[...], sb_ref[...]
    zq = _dot(u, wq_ref[...])
    scale = HEAD_DIM ** -0.5
    for j in range(zq.shape[1] // LANES):
        blk = _rope_block(zq[:, j * LANES:(j + 1) * LANES], cos, sa, sb)
        q_ref[:, j * LANES:(j + 1) * LANES] = (blk * scale).astype(BF16)

    zkv = _dot(u, wkv_ref[...])
    per_branch = KV_W // LANES
    outs = (cmp_ref, slc_ref, win_ref)
    for j in range(zkv.shape[1] // LANES):
        br, jj = divmod(j, per_branch)
        blk = zkv[:, j * LANES:(j + 1) * LANES]
        if jj < per_branch // 2:
            blk = _rope_block(blk, cos, sa, sb)
        outs[br][:, jj * LANES:(jj + 1) * LANES] = blk
        if br > 0:
            kvb_ref[:, (j - per_branch) * LANES:(j - per_branch + 1) * LANES] = blk.astype(BF16)

    gl_ref[...] = jax.nn.sigmoid(_dot(u, wgl_ref[...]))


def _front(x, g, tabs, w, conv_w, state, seq_rows):
    n, d = x.shape
    sample = state is not None
    tm = n if sample else ROW_TILE
    tiles_per_seq = seq_rows // tm
    nt = n // tm
    n_seq = n // seq_rows
    row = lambda w_: pl.BlockSpec((tm, w_), lambda i: (i, 0))
    tab = pl.BlockSpec((tm, LANES), lambda i: (i % tiles_per_seq, 0))
    args = [x, g, *tabs, *w[:5], conv_w, w[5]]
    in_specs = [row(d), _const_spec((1, d)), tab, tab, tab] + [_const_spec(a.shape) for a in w[:5]] + [
        _const_spec(conv_w.shape), _const_spec(w[5].shape)]
    if sample:
        args.append(state)
        in_specs.append(row(d))
        pst_shape, pst_spec = jax.ShapeDtypeStruct((n, d), F32), row(d)
    else:
        pst_shape = jax.ShapeDtypeStruct((n_seq, SUBLANES, d), F32)
        pst_spec = pl.BlockSpec((1, SUBLANES, d), lambda i: (i // tiles_per_seq, 0, 0))
    out_shape = [jax.ShapeDtypeStruct((n, d), F32), jax.ShapeDtypeStruct((n, d), F32),
                 jax.ShapeDtypeStruct((n, N_HEADS * HEAD_DIM), BF16),
                 jax.ShapeDtypeStruct((n, KV_W), F32), jax.ShapeDtypeStruct((n, KV_W), F32),
                 jax.ShapeDtypeStruct((n, KV_W), F32), jax.ShapeDtypeStruct((n, 2 * KV_W), BF16),
                 jax.ShapeDtypeStruct((n, 2 * LANES), F32), pst_shape]
    out_specs = [row(d), row(d), row(N_HEADS * HEAD_DIM), row(KV_W), row(KV_W), row(KV_W), row(2 * KV_W),
                 row(2 * LANES), pst_spec]
    return pl.pallas_call(
        functools.partial(_front_kernel, sample, tiles_per_seq),
        grid=(nt,), in_specs=in_specs, out_specs=out_specs, out_shape=out_shape,
        scratch_shapes=[pltpu.VMEM((tm + SUBLANES, d), F32)],
        compiler_params=_cparams(("arbitrary",)), name="front_sample" if sample else "front_prompt",
    )(*args)


def _cmp_bias_kernel(pe_ref, w1_ref, b1_ref, o_ref):
    for kv in range(2):
        a = _dot(pe_ref[kv], w1_ref[kv])
        o_ref[kv] = jnp.broadcast_to(b1_ref[kv] + a[0:1, 0:LANES] + a[1:2, LANES:2 * LANES], (SUBLANES, LANES))


def _compress_tile(rows_refs, w1_ref, hb_ref, w2_ref, b2_ref, carry_ref, sh_ref, out_ref):
    nck = rows_refs[0].shape[0] // CMP_STRIDE
    for j, rows_ref in enumerate(rows_refs):
        kv = j // 2
        lhs = jnp.concatenate(
            [rows_ref[pl.ds(s, nck, stride=CMP_STRIDE), :].astype(BF16) for s in range(CMP_STRIDE)],
            axis=1)
        a = _dot(lhs, w1_ref[kv])
        a0 = a[:, 0:LANES]
        sh_ref[SUBLANES:SUBLANES + nck, :] = a0
        sh_ref[0:SUBLANES, :] = carry_ref[j]
        hid = sh_ref[SUBLANES - 1:SUBLANES - 1 + nck, :] + a[:, LANES:2 * LANES] + hb_ref[kv][0:1, :]
        carry_ref[j] = sh_ref[nck:nck + SUBLANES, :]
        o = _dot(jax.nn.silu(hid).astype(BF16), w2_ref[kv]) + b2_ref[kv]
        out_ref[:, j * LANES:(j + 1) * LANES] = o.astype(out_ref.dtype)


def _compress_prompt_kernel(r0, r1, r2, r3, w1_ref, hb_ref, w2_ref, b2_ref, out_ref, carry_ref, sh_ref):
    @pl.when(pl.program_id(1) == 0)
    def _():
        carry_ref[...] = jnp.zeros(carry_ref.shape, F32)
    _compress_tile((r0, r1, r2, r3), w1_ref, hb_ref, w2_ref, b2_ref, carry_ref, sh_ref, out_ref.at[0])


def _page_copies(pt_ref, cache_ref, buf_ref, sem_ref, slot, b, grp, n_pages):
    page_rows = cache_ref.shape[2]
    return [pltpu.make_async_copy(cache_ref.at[pt_ref[b, grp * n_pages + p]],
                                  buf_ref.at[slot, :, pl.ds(p * page_rows, page_rows)], sem_ref.at[slot])
            for p in range(n_pages)]


def _gather_step(pt_ref, cache_ref, buf_ref, sem_ref, n_pages):
    b, g = pl.program_id(0), pl.program_id(1)
    nb, ng = pl.num_programs(0), pl.num_programs(1)
    step = b * ng + g
    slot = step % 2

    @pl.when(step == 0)
    def _():
        for c in _page_copies(pt_ref, cache_ref, buf_ref, sem_ref, 0, 0, 0, n_pages):
            c.start()

    @pl.when(step + 1 < nb * ng)
    def _():
        nxt = step + 1
        for c in _page_copies(pt_ref, cache_ref, buf_ref, sem_ref, 1 - slot, nxt // ng, nxt % ng, n_pages):
            c.start()

    for c in _page_copies(pt_ref, cache_ref, buf_ref, sem_ref, slot, b, g, n_pages):
        c.wait()
    return slot


def _compress_sample_kernel(pt_ref, cache_ref, w1_ref, hb_ref, w2_ref, b2_ref, out_ref,
                            buf_ref, sem_ref, planes_ref, carry_ref, sh_ref):
    n_pages = buf_ref.shape[2] // cache_ref.shape[2]
    slot = _gather_step(pt_ref, cache_ref, buf_ref, sem_ref, n_pages)

    @pl.when(pl.program_id(1) == 0)
    def _():
        carry_ref[...] = jnp.zeros(carry_ref.shape, F32)
    n_planes = planes_ref.shape[0]
    for j in range(n_planes):
        planes_ref[j] = buf_ref[slot, j * LANES:(j + 1) * LANES, :].T
    planes = tuple(planes_ref.at[j] for j in range(n_planes))
    _compress_tile(planes, w1_ref, hb_ref, w2_ref, b2_ref, carry_ref, sh_ref, out_ref.at[0])


def _compress_weights(cmp_pe, cmp_w1, cmp_b1, cmp_w2, cmp_b2):
    r = CMP_LEN // CMP_STRIDE
    eye2 = jnp.eye(2, dtype=F32)
    w1 = cmp_w1.reshape(2, r, CMP_STRIDE, HEAD_DIM, HEAD_DIM)
    w1bd = jnp.einsum("krsde,hg->kshdrge", w1, eye2).reshape(2, CMP_STRIDE * LANES, r * LANES).astype(BF16)
    w2bd = jnp.einsum("kde,hg->khdge", cmp_w2, eye2).reshape(2, LANES, LANES).astype(BF16)
    pe = cmp_pe.reshape(2, r, CMP_STRIDE, 1, HEAD_DIM)
    pe = jnp.broadcast_to(pe, (2, r, CMP_STRIDE, 2, HEAD_DIM)).reshape(2, r, CMP_STRIDE * LANES)
    pe = jnp.pad(pe, ((0, 0), (0, SUBLANES - r), (0, 0))).astype(BF16)
    b1 = jnp.tile(cmp_b1, (1, 2)).reshape(2, 1, LANES)
    b2 = jnp.tile(cmp_b2, (1, 2)).reshape(2, 1, LANES)
    hb = pl.pallas_call(
        _cmp_bias_kernel, out_shape=jax.ShapeDtypeStruct((2, SUBLANES, LANES), F32), name="compress_bias",
    )(pe, w1bd, b1)
    return w1bd, hb, w2bd, b2


def _compress_prompt(cmp_rows, cw, n_seq):
    n = cmp_rows.shape[0]
    s = n // n_seq
    nck = CMP_ROWS // CMP_STRIDE
    tiles = s // CMP_ROWS
    w1bd, hb, w2bd, b2 = cw
    return pl.pallas_call(
        _compress_prompt_kernel,
        grid=(n_seq, tiles),
        in_specs=[pl.BlockSpec((CMP_ROWS, LANES), functools.partial(lambda j, b, t: (b * tiles + t, j), j))
                  for j in range(KV_W // LANES)] + [
                  _const_spec(w1bd.shape), _const_spec(hb.shape), _const_spec(w2bd.shape), _const_spec(b2.shape)],
        out_specs=pl.BlockSpec((1, nck, KV_W), lambda b, t: (b, t, 0)),
        out_shape=jax.ShapeDtypeStruct((n_seq, s // CMP_STRIDE, KV_W), BF16),
        scratch_shapes=[pltpu.VMEM((KV_W // LANES, SUBLANES, LANES), F32),
                        pltpu.VMEM((nck + SUBLANES, LANES), F32)],
        compiler_params=_cparams(("arbitrary", "arbitrary")), name="compress_prompt",
    )(cmp_rows, cmp_rows, cmp_rows, cmp_rows, w1bd, hb, w2bd, b2)


def _feature_major_pages(cache):
    return cache.transpose(0, 2, 3, 4, 1).reshape(cache.shape[0], KV_W, cache.shape[1])


def _compress_sample(page_table, cache_t, cw):
    db, n_pages = page_table.shape
    page_rows = cache_t.shape[2]
    pages_per = CMP_ROWS // page_rows
    groups = n_pages // pages_per
    nck = CMP_ROWS // CMP_STRIDE
    w1bd, hb, w2bd, b2 = cw
    cspec = lambda a: pl.BlockSpec(a.shape, lambda b, g, pt: (0,) * a.ndim, pipeline_mode=pl.Buffered(1))
    grid_spec = pltpu.PrefetchScalarGridSpec(
        num_scalar_prefetch=1, grid=(db, groups),
        in_specs=[pl.BlockSpec(memory_space=pl.ANY), cspec(w1bd), cspec(hb), cspec(w2bd), cspec(b2)],
        out_specs=pl.BlockSpec((1, nck, KV_W), lambda b, g, pt: (b, g, 0)),
        scratch_shapes=[pltpu.VMEM((2, KV_W, CMP_ROWS), F32), pltpu.SemaphoreType.DMA((2,)),
                        pltpu.VMEM((KV_W // LANES, CMP_ROWS, LANES), F32),
                        pltpu.VMEM((KV_W // LANES, SUBLANES, LANES), F32),
                        pltpu.VMEM((nck + SUBLANES, LANES), F32)])
    return pl.pallas_call(
        _compress_sample_kernel, grid_spec=grid_spec,
        out_shape=jax.ShapeDtypeStruct((db, n_pages * page_rows // CMP_STRIDE, KV_W), BF16),
        compiler_params=_cparams(("arbitrary", "arbitrary")), name="compress_sample",
    )(page_table, cache_t, w1bd, hb, w2bd, b2)


def _masked_softmax(s, mask):
    s = jnp.where(mask, s, -jnp.inf)
    m = jnp.max(s, axis=-1, keepdims=True)
    m = jnp.where(m == -jnp.inf, 0.0, m)
    e = jnp.where(mask, jnp.exp(s - m), 0.0)
    return e / jnp.maximum(jnp.sum(e, axis=-1, keepdims=True), 1e-30)


def _topk_not_selected(imp, n_pick):
    rows, width = imp.shape
    lane = lax.broadcasted_iota(jnp.int32, (rows, width), 1).astype(F32)

    def pick(_, carry):
        imp, notsel = carry
        m = jnp.max(imp, axis=-1, keepdims=True)
        first = jnp.min(jnp.where(imp == m, lane, float(width)), axis=-1, keepdims=True)
        hit = lane == first
        return jnp.where(hit, -jnp.inf, imp), jnp.where(hit, 0.0, notsel)

    _, notsel = lax.fori_loop(0, n_pick, pick, (imp, jnp.ones((rows, width), F32)), unroll=True)
    return notsel


def _flash_update(carry, s, v, v_transposed=False):
    m, l, acc = carry
    m_new = jnp.maximum(m, jnp.max(s, axis=-1, keepdims=True))
    alpha = jnp.exp(m - m_new)
    p = jnp.exp(s - m_new)
    l = alpha * l + jnp.sum(p, axis=-1, keepdims=True)
    pv = _dot_nt(p.astype(v.dtype), v) if v_transposed else _dot(p.astype(v.dtype), v)
    return m_new, l, alpha * acc + pv


def _half_mask(shape):
    return lax.broadcasted_iota(jnp.int32, shape, len(shape) - 1) < HEAD_DIM


def _stack_heads(q_blk, nr):
    first = _half_mask((q_blk.shape[0], LANES))
    zero = jnp.zeros((), q_blk.dtype)
    parts = []
    for gl in range(2):
        for r in range(nr):
            blk = q_blk[:, r * LANES:(r + 1) * LANES]
            parts.append(jnp.where(first if gl == 0 else ~first, blk, zero))
    return jnp.concatenate(parts, axis=0)


def _merge_pair(o, nr, t):
    first = _half_mask((t, LANES))
    return [jnp.where(first, o[r * t:(r + 1) * t], o[(nr + r) * t:(nr + r + 1) * t]) for r in range(nr)]


def _gate_vec(g_ref, br, r, t):
    c = br * 2 * (N_HEADS // N_KV_HEADS) + r * 2
    return jnp.where(_half_mask((t, LANES)), g_ref[:, c:c + 1], g_ref[:, c + 1:c + 2])


def _forced_importance(imp, blk, t_pos):
    cur = t_pos // SLC_LEN
    valid = blk <= cur
    forced = (blk == 0) | (valid & (blk > cur - N_LOCAL))
    return jnp.where(forced, FORCED_SCORE, jnp.where(valid, imp, -FORCED_SCORE))


def _attend_prompt_kernel(q_ref, g_ref, kc_ref, vc_ref, ks_ref, vs_ref, kw_ref, vw_ref, e_ref, ovl_ref, o_ref):
    i = pl.program_id(2)
    nr = N_HEADS // N_KV_HEADS
    t = Q_TILE
    rows = 2 * nr * t
    t0 = i * t
    tq = t0 + lax.broadcasted_iota(jnp.int32, (t, 1), 0)
    tq_rows = jnp.concatenate([tq] * (2 * nr), axis=0)
    qs = _stack_heads(q_ref[...], nr)

    ncmp = kc_ref.shape[1]
    cidx = lax.broadcasted_iota(jnp.int32, (1, ncmp), 1)
    c_mask = (cidx >= 1) & (cidx * CMP_STRIDE + (CMP_LEN - CMP_STRIDE - 1) <= tq_rows)
    p_c = _masked_softmax(_dot_nt(qs, kc_ref[0]), c_mask)
    o_c = _dot(p_c.astype(BF16), vc_ref[0])

    nslc = ovl_ref.shape[1]
    blk = lax.broadcasted_iota(jnp.int32, (1, nslc), 1)
    psums = []
    for gl in range(2):
        psum = p_c[gl * nr * t:gl * nr * t + t]
        for r in range(1, nr):
            psum = psum + p_c[(gl * nr + r) * t:(gl * nr + r + 1) * t]
        psums.append(psum)
    imp = jnp.dot(jnp.concatenate(psums, axis=0), ovl_ref[...], preferred_element_type=F32,
                  precision=lax.Precision.HIGHEST)
    imp = _forced_importance(imp, blk, jnp.concatenate([tq, tq], axis=0))
    ns = _topk_not_selected(imp, min(N_SEL, nslc)).astype(BF16)
    notsel = [ns[0:t]] * nr + [ns[t:2 * t]] * nr
    qa = jnp.concatenate([qs, jnp.concatenate(notsel, axis=0)], axis=1)

    def scores(j):
        off = pl.multiple_of(j * K_TILE, K_TILE)
        ka = jnp.concatenate([ks_ref[pl.ds(off, K_TILE), :], e_ref[pl.ds(off, K_TILE), :]], axis=1)
        return _dot_nt(qa, ka), vs_ref[pl.ds(off, K_TILE), :]

    def body(j, carry):
        s, v = scores(j)
        return _flash_update(carry, s, v)

    last = (t0 + t - 1) // K_TILE
    init = (jnp.full((rows, 1), -jnp.inf, F32), jnp.zeros((rows, 1), F32), jnp.zeros((rows, LANES), F32))
    carry = lax.fori_loop(0, last, body, init)
    s, v = scores(last)
    kpos = last * K_TILE + lax.broadcasted_iota(jnp.int32, (1, K_TILE), 1)
    _, l, acc = _flash_update(carry, jnp.where(kpos <= tq_rows, s, -MASK_BIG), v)
    o_s = acc / l

    wlen = WINDOW + t
    ws = pl.multiple_of(jnp.maximum(t0 - WINDOW, 0), t)
    wpos = ws + lax.broadcasted_iota(jnp.int32, (1, wlen), 1)
    w_mask = (wpos <= tq_rows) & (wpos > tq_rows - WINDOW)
    p_w = _masked_softmax(_dot_nt(qs, kw_ref[pl.ds(ws, wlen), :]), w_mask)
    o_w = _dot(p_w.astype(BF16), vw_ref[pl.ds(ws, wlen), :])

    oc, os_, ow = _merge_pair(o_c, nr, t), _merge_pair(o_s, nr, t), _merge_pair(o_w, nr, t)
    for r in range(nr):
        o = _gate_vec(g_ref, 0, r, t) * oc[r] + _gate_vec(g_ref, 1, r, t) * os_[r] + _gate_vec(g_ref, 2, r, t) * ow[r]
        o_ref[:, r * LANES:(r + 1) * LANES] = o.astype(o_ref.dtype)


def _attend_prompt(q, gates, kcvc, kvb, econst, ovl, n_seq):
    n = q.shape[0]
    s = n // n_seq
    nq = s // Q_TILE
    nr = N_HEADS // N_KV_HEADS
    ncmp = kcvc.shape[1]
    qw = nr * LANES
    seq_blk = lambda c0: pl.BlockSpec((s, LANES), lambda b, k, i: (b, c0 + k))
    return pl.pallas_call(
        _attend_prompt_kernel,
        grid=(n_seq, 2, nq),
        in_specs=[pl.BlockSpec((Q_TILE, qw), lambda b, k, i: (b * nq + i, k)),
                  pl.BlockSpec((Q_TILE, LANES), lambda b, k, i: (b * nq + i, k)),
                  pl.BlockSpec((1, ncmp, LANES), lambda b, k, i: (b, 0, k)),
                  pl.BlockSpec((1, ncmp, LANES), lambda b, k, i: (b, 0, 2 + k)),
                  seq_blk(0), seq_blk(2), seq_blk(4), seq_blk(6),
                  pl.BlockSpec(econst.shape, lambda b, k, i: (0, 0), pipeline_mode=pl.Buffered(1)),
                  pl.BlockSpec(ovl.shape, lambda b, k, i: (0, 0), pipeline_mode=pl.Buffered(1))],
        out_specs=pl.BlockSpec((Q_TILE, qw), lambda b, k, i: (b * nq + i, k)),
        out_shape=jax.ShapeDtypeStruct((n, N_HEADS * HEAD_DIM), BF16),
        compiler_params=_cparams(("arbitrary", "arbitrary", "arbitrary")), name="attend_prompt",
    )(q, gates, kcvc, kcvc, kvb, kvb, kvb, kvb, econst, ovl)


def _attend_sample_kernel(past_len, pt_ref, q_ref, g_ref, kcvc_ref, cache_ref, slc_ref, wcache_ref, win_ref,
                          e_ref, ovl_ref, o_ref, buf_ref, sem_ref, ns_ref, oc_ref, m_ref, l_ref, acc_ref):
    g = pl.program_id(1)
    ng = pl.num_programs(1)
    nr = N_HEADS // N_KV_HEADS
    t = SAMPLE_ROWS
    rows = 2 * nr * t
    n_pages = buf_ref.shape[2] // cache_ref.shape[2]
    keys = buf_ref.shape[2]
    slot = _gather_step(pt_ref, cache_ref, buf_ref, sem_ref, n_pages)
    tq = past_len - (t - 4) + lax.broadcasted_iota(jnp.int32, (t, 1), 0)
    tq_rows = jnp.concatenate([tq] * (2 * nr), axis=0)
    qs = [_stack_heads(q_ref[:, k * nr * LANES:(k + 1) * nr * LANES], nr) for k in range(2)]

    @pl.when(g == 0)
    def _():
        ncmp = kcvc_ref.shape[1]
        cidx = lax.broadcasted_iota(jnp.int32, (1, ncmp), 1)
        c_mask = (cidx >= 1) & (cidx * CMP_STRIDE + (CMP_LEN - CMP_STRIDE - 1) <= tq_rows)
        width = ovl_ref.shape[1]
        lane = lax.broadcasted_iota(jnp.int32, (1, width), 1)
        per = keys // SLC_LEN
        blk = (lane // LANES) * per + lane % LANES
        real = (lane % LANES < per) & (blk <= past_len // SLC_LEN)
        psums = []
        for k in range(2):
            p_c = _masked_softmax(_dot_nt(qs[k], kcvc_ref[0, :, k * LANES:(k + 1) * LANES]), c_mask)
            oc_ref[k] = _dot(p_c.astype(BF16), kcvc_ref[0, :, (2 + k) * LANES:(3 + k) * LANES])
            for gl in range(2):
                psum = p_c[gl * nr * t:gl * nr * t + t]
                for r in range(1, nr):
                    psum = psum + p_c[(gl * nr + r) * t:(gl * nr + r + 1) * t]
                psums.append(psum)
        imp = jnp.dot(jnp.concatenate(psums, axis=0), ovl_ref[...], preferred_element_type=F32,
                      precision=lax.Precision.HIGHEST)
        imp = jnp.where(real, _forced_importance(imp, blk, jnp.concatenate([tq] * 4, axis=0)), -jnp.inf)
        ns = _topk_not_selected(imp, N_SEL)
        for k in range(2):
            for gl in range(2):
                for r in range(nr):
                    ns_ref[k, (gl * nr + r) * t:(gl * nr + r + 1) * t, :] = ns[(2 * k + gl) * t:(2 * k + gl + 1) * t]
        m_ref[...] = jnp.full(m_ref.shape, -jnp.inf, F32)
        l_ref[...] = jnp.zeros(l_ref.shape, F32)
        acc_ref[...] = jnp.zeros(acc_ref.shape, F32)

    page = buf_ref.at[slot]
    for k in range(2):
        ns = ns_ref[k, :, pl.ds(pl.multiple_of(g * LANES, LANES), LANES)]
        qa = jnp.concatenate([qs[k], ns.astype(BF16)], axis=1)
        ka = jnp.concatenate([page[k * LANES:(k + 1) * LANES, :].astype(BF16), e_ref[...]], axis=0)
        v = page[(2 + k) * LANES:(3 + k) * LANES, :].astype(BF16)
        m_ref[k], l_ref[k], acc_ref[k] = _flash_update((m_ref[k], l_ref[k], acc_ref[k]), _dot(qa, ka), v,
                                                       v_transposed=True)

    @pl.when(g == ng - 1)
    def _():
        kidx = lax.broadcasted_iota(jnp.int32, (1, t), 1)
        kpos = past_len - (t - 4) + kidx
        new_ok = (kidx >= t - 4) & (kpos <= tq_rows)
        wb = wcache_ref.shape[1]
        wpos = past_len - wb + lax.broadcasted_iota(jnp.int32, (1, wb), 1)
        w_old = (wpos <= tq_rows) & (wpos > tq_rows - WINDOW) & (wpos >= 0)
        w_new = new_ok & (kpos > tq_rows - WINDOW)
        for k in range(2):
            kcol, vcol = slice(k * LANES, (k + 1) * LANES), slice((2 + k) * LANES, (3 + k) * LANES)
            qf = qs[k].astype(F32)
            s_new = jnp.where(new_ok, _dot_nt(qf, slc_ref[:, kcol]), -MASK_BIG)
            _, l, acc = _flash_update((m_ref[k], l_ref[k], acc_ref[k]), s_new, slc_ref[:, vcol])
            o_s = acc / l
            so = jnp.where(w_old, _dot_nt(qs[k], wcache_ref[0, :, kcol].astype(BF16)), -jnp.inf)
            sn = jnp.where(w_new, _dot_nt(qf, win_ref[:, kcol]), -jnp.inf)
            m = jnp.maximum(jnp.max(so, axis=-1, keepdims=True), jnp.max(sn, axis=-1, keepdims=True))
            m = jnp.where(m == -jnp.inf, 0.0, m)
            eo = jnp.where(w_old, jnp.exp(so - m), 0.0)
            en = jnp.where(w_new, jnp.exp(sn - m), 0.0)
            den = jnp.maximum(jnp.sum(eo, axis=-1, keepdims=True) + jnp.sum(en, axis=-1, keepdims=True), 1e-30)
            o_w = (_dot(eo.astype(BF16), wcache_ref[0, :, vcol].astype(BF16)) + _dot(en, win_ref[:, vcol])) / den
            oc, os_, ow = _merge_pair(oc_ref[k], nr, t), _merge_pair(o_s, nr, t), _merge_pair(o_w, nr, t)
            gk = g_ref.at[:, k * LANES:(k + 1) * LANES]
            for r in range(nr):
                o = _gate_vec(gk, 0, r, t) * oc[r] + _gate_vec(gk, 1, r, t) * os_[r] + _gate_vec(gk, 2, r, t) * ow[r]
                o_ref[:, (k * nr + r) * LANES:(k * nr + r + 1) * LANES] = o.astype(o_ref.dtype)


def _attend_sample(page_table, q, gates, kcvc, cache_t, slc_new, wcache, win_new, econst_t, ovl, past_len):
    db, n_pages = page_table.shape
    page_rows = cache_t.shape[2]
    econst = econst_t
    groups = n_pages // PAGES_PER_STEP
    keys = PAGES_PER_STEP * page_rows
    nr = N_HEADS // N_KV_HEADS
    t = SAMPLE_ROWS
    rows = 2 * nr * t
    qw = N_HEADS * HEAD_DIM
    cmap = lambda nd: (lambda b, g, pt: (0,) * nd)
    grid_spec = pltpu.PrefetchScalarGridSpec(
        num_scalar_prefetch=1, grid=(db, groups),
        in_specs=[pl.BlockSpec((t, qw), lambda b, g, pt: (b, 0)),
                  pl.BlockSpec((t, 2 * LANES), lambda b, g, pt: (b, 0)),
                  pl.BlockSpec((1,) + kcvc.shape[1:], lambda b, g, pt: (b, 0, 0)),
                  pl.BlockSpec(memory_space=pl.ANY),
                  pl.BlockSpec((t, KV_W), lambda b, g, pt: (b, 0)),
                  pl.BlockSpec((1,) + wcache.shape[1:], lambda b, g, pt: (b, 0, 0)),
                  pl.BlockSpec((t, KV_W), lambda b, g, pt: (b, 0)),
                  pl.BlockSpec(econst.shape, cmap(2), pipeline_mode=pl.Buffered(1)),
                  pl.BlockSpec(ovl.shape, cmap(2), pipeline_mode=pl.Buffered(1))],
        out_specs=pl.BlockSpec((t, qw), lambda b, g, pt: (b, 0)),
        scratch_shapes=[pltpu.VMEM((2, KV_W, keys), F32), pltpu.SemaphoreType.DMA((2,)),
                        pltpu.VMEM((2, rows, ovl.shape[1]), F32), pltpu.VMEM((2, rows, LANES), F32),
                        pltpu.VMEM((2, rows, 1), F32), pltpu.VMEM((2, rows, 1), F32),
                        pltpu.VMEM((2, rows, LANES), F32)])
    return pl.pallas_call(
        functools.partial(_attend_sample_kernel, past_len), grid_spec=grid_spec,
        out_shape=jax.ShapeDtypeStruct((db * t, qw), BF16),
        compiler_params=_cparams(("arbitrary", "arbitrary")), name="attend_sample",
    )(page_table, q, gates, kcvc, cache_t, slc_new, wcache, win_new, econst, ovl)


def _post_kernel(sample, tiles_per_seq, *refs):
    if sample:
        (x_ref, o_ref, ma_ref, sgb_ref, wao_ref, wo_ref, gf_ref, wfi_ref, fcw_ref, wfd_ref, gfin_ref, st_ref,
         y_ref, ast_ref, abuf) = refs
    else:
        (x_ref, o_ref, ma_ref, sgb_ref, wao_ref, wo_ref, gf_ref, wfi_ref, fcw_ref, wfd_ref, gfin_ref,
         y_ref, ast_ref, abuf) = refs
    rows = x_ref.shape[0]
    dff = fcw_ref.shape[1]
    m = ma_ref[...] + sgb_ref[...] * _dot(o_ref[...], wao_ref[...])
    h = x_ref[...] + _dot(m.astype(BF16), wo_ref[...])
    ag = _dot(_rms(h, gf_ref[...]).astype(BF16), wfi_ref[...])
    a = ag[:, 0:dff]
    if sample:
        row = lax.broadcasted_iota(jnp.int32, (rows, 1), 0) % SAMPLE_ROWS
        is_state = (row >= SAMPLE_ROWS - 4 - (CONV_W - 1)) & (row < SAMPLE_ROWS - 4)
        a = jnp.where(is_state, st_ref[...], a)
        abuf[0:SUBLANES, :] = jnp.zeros((SUBLANES, dff), F32)
    else:
        @pl.when(pl.program_id(0) % tiles_per_seq == 0)
        def _():
            abuf[0:SUBLANES, :] = jnp.zeros((SUBLANES, dff), F32)
    ac = _shifted_conv(abuf, a, fcw_ref, rows)
    tail = abuf[rows:rows + SUBLANES, :]
    if sample:
        ast_ref[...] = a
    else:
        ast_ref[0] = tail
        abuf[0:SUBLANES, :] = tail
    hh = h + _dot((jax.nn.silu(ac) * ag[:, dff:2 * dff]).astype(BF16), wfd_ref[...])
    y_ref[...] = _rms(hh, gfin_ref[...])


def _post(x, o, ma, sgb, w, state, seq_rows):
    n, d = x.shape
    sample = state is not None
    tm = n if sample else ROW_TILE
    tiles_per_seq = seq_rows // tm
    n_seq = n // seq_rows
    wao, wo, gf, wfi, fcw, wfd, gfin = w
    dff = fcw.shape[1]
    row = lambda w_: pl.BlockSpec((tm, w_), lambda i: (i, 0))
    args = [x, o, ma, sgb, *w]
    in_specs = [row(d), row(o.shape[1]), row(d), row(d)] + [_const_spec(a.shape) for a in w]
    if sample:
        args.append(state)
        in_specs.append(row(dff))
        ast_shape, ast_spec = jax.ShapeDtypeStruct((n, dff), F32), row(dff)
    else:
        ast_shape = jax.ShapeDtypeStruct((n_seq, SUBLANES, dff), F32)
        ast_spec = pl.BlockSpec((1, SUBLANES, dff), lambda i: (i // tiles_per_seq, 0, 0))
    return pl.pallas_call(
        functools.partial(_post_kernel, sample, tiles_per_seq),
        grid=(n // tm,), in_specs=in_specs, out_specs=[row(d), ast_spec],
        out_shape=[jax.ShapeDtypeStruct((n, d), F32), ast_shape],
        scratch_shapes=[pltpu.VMEM((tm + SUBLANES, dff), F32)],
        compiler_params=_cparams(("arbitrary",)), name="post_sample" if sample else "post_prompt",
    )(*args)


def _pair_head_order():
    nr = N_HEADS // N_KV_HEADS
    return [2 * nr * k + nr * half + r for k in range(2) for r in range(nr) for half in range(2)]


def _rope_tables(pos):
    half = ROT_DIM // 2
    inv_freq = jnp.power(ROPE_THETA, -jnp.arange(half, dtype=F32) * (2.0 / ROT_DIM))
    ang = pos.astype(F32)[:, None] * inv_freq[None, :]
    cos, sin = jnp.cos(ang), jnp.sin(ang)
    n = pos.shape[0]
    one, zero = jnp.ones((n, HEAD_DIM - ROT_DIM), F32), jnp.zeros((n, HEAD_DIM - ROT_DIM), F32)
    zh = jnp.zeros((n, half), F32)
    tabs = (jnp.concatenate([cos, cos, one], 1), jnp.concatenate([zh, sin, zero], 1),
            jnp.concatenate([-sin, zh, zero], 1))
    return tuple(jnp.tile(a, (1, LANES // HEAD_DIM)) for a in tabs)


def _overlap_matrix(n_rows, n_cols, col_block):
    c = np.arange(n_rows)[:, None] - 1
    blk = col_block[None, :]
    cs, ss = c * CMP_STRIDE, blk * SLC_LEN
    return ((c >= 0) & (blk >= 0) & (cs < ss + SLC_LEN) & (cs + CMP_LEN > ss)).astype(np.float32)


def _block_onehot(n_keys, transposed=False):
    e = (np.arange(n_keys)[:, None] // SLC_LEN == np.arange(LANES)[None, :]).astype(np.float32)
    return jnp.asarray(-MASK_BIG * (e.T if transposed else e), dtype=BF16)


def _split_w_in(w_in, d):
    conv_dim = d
    q_dim = N_HEADS * HEAD_DIM
    kv_dim = N_BRANCH * KV_W
    o0 = 3 * conv_dim
    wc = w_in[:, 0:o0]
    wq = w_in[:, o0:o0 + q_dim].reshape(d, N_HEADS, HEAD_DIM)[:, np.array(_pair_head_order())].reshape(d, q_dim)
    o1 = o0 + q_dim
    wkv = w_in[:, o1:o1 + kv_dim]
    o2 = o1 + kv_dim
    wl = w_in[:, o2:o2 + N_BRANCH * N_HEADS].reshape(d, N_HEADS, N_BRANCH)
    wl = wl[:, np.array(_pair_head_order())].reshape(d, 2, N_HEADS // 2, N_BRANCH).transpose(0, 1, 3, 2)
    wl = jnp.pad(wl.reshape(d, 2, N_BRANCH * N_HEADS // 2), ((0, 0), (0, 0), (0, LANES - N_BRANCH * N_HEADS // 2)))
    wl = wl.reshape(d, 2 * LANES)
    o3 = o2 + N_BRANCH * N_HEADS
    wgab = w_in[:, o3:o3 + 2 * d]
    return [a.astype(BF16) for a in (wc, wq, wkv, wl, wgab)]


def kernel(x_prompt, x_sample, cache_cmp_kv, cache_slc_kv, cache_win_kv, state_conv_mix, state_conv_ffn,
           page_table, norm_mix_g, w_in, conv_mix_w, w_conv_out, cmp_pe, cmp_w1, cmp_b1, cmp_w2, cmp_b2,
           w_attn_out, w_out, norm_ffn_g, w_ff_in, ff_conv_w, w_ff_down, norm_final_g):
    depth = w_in.shape[0]
    assert depth == 1, "single-layer step"
    b, s, d = x_prompt.shape
    db, t, _ = x_sample.shape
    page_rows = cache_cmp_kv.shape[2]
    past_len = page_table.shape[1] * page_rows
    assert t == 4 and s % K_TILE == 0 and s % CMP_ROWS == 0 and s >= WINDOW + Q_TILE
    assert past_len % (PAGES_PER_STEP * page_rows) == 0 and cache_win_kv.shape[2] == WINDOW
    assert N_SEL <= s // SLC_LEN <= LANES, "prompt selection blocks fit one lane block"
    dff = ff_conv_w.shape[2]
    l = 0

    order = np.array(_pair_head_order())
    front_w = _split_w_in(w_in[l], d) + [w_conv_out[l].astype(BF16)]
    post_w = [w_attn_out[l].reshape(N_HEADS, HEAD_DIM, d)[order].reshape(N_HEADS * HEAD_DIM, d).astype(BF16),
              w_out[l].astype(BF16), norm_ffn_g[l].reshape(1, d), w_ff_in[l].astype(BF16), ff_conv_w[l],
              w_ff_down[l].astype(BF16), norm_final_g.reshape(1, d)]
    g_mix = norm_mix_g[l].reshape(1, d)
    cw = _compress_weights(cmp_pe[l], cmp_w1[l], cmp_b1[l], cmp_w2[l], cmp_b2[l])
    econst = _block_onehot(s)
    econst_t = _block_onehot(PAGES_PER_STEP * page_rows, transposed=True)

    xp = x_prompt.reshape(b * s, d)
    ma, sgb, q, cmp_p, slc_p, win_p, kvb, gates, pst = _front(
        xp, g_mix, _rope_tables(jnp.arange(s, dtype=jnp.int32)), front_w, conv_mix_w[l], None, s)
    kcvc = _compress_prompt(cmp_p, cw, b)
    ovl_p = jnp.asarray(_overlap_matrix(s // CMP_STRIDE, LANES, np.arange(LANES)))
    o = _attend_prompt(q, gates, kcvc, kvb, econst, ovl_p, b)
    y_p, ast = _post(xp, o, ma, sgb, post_w, None, s)
    kv_shape = (2, N_KV_HEADS, HEAD_DIM)
    wb_p = min(WINDOW, s)
    out_prompt = (
        y_p.reshape(b, s, d),
        cmp_p.reshape((1, b, s) + kv_shape), slc_p.reshape((1, b, s) + kv_shape),
        win_p.reshape((b, s) + kv_shape)[None, :, s - wb_p:],
        pst[None, :, SUBLANES - (CONV_W - 1):], ast[None, :, SUBLANES - (CONV_W - 1):])

    r8 = SAMPLE_ROWS
    pad_rows = lambda a, lo: jnp.pad(a, ((0, 0), (lo, r8 - lo - a.shape[1]), (0, 0))).reshape(db * r8, a.shape[2])
    xs = pad_rows(x_sample, r8 - t)
    st_mix = pad_rows(state_conv_mix[l], r8 - t - (CONV_W - 1))
    st_ffn = pad_rows(state_conv_ffn[l], r8 - t - (CONV_W - 1))
    pos_s = past_len - (r8 - t) + jnp.arange(r8, dtype=jnp.int32)
    tabs_s = tuple(jnp.tile(a, (db, 1)) for a in _rope_tables(pos_s))
    ma_s, sgb_s, q_s, cmp_s, slc_s, win_s, _, gates_s, p_s = _front(
        xs, g_mix, tabs_s, front_w, conv_mix_w[l], st_mix, db * r8)
    kcvc_s = _compress_sample(page_table, _feature_major_pages(cache_cmp_kv[l]), cw)
    keys_per_step = PAGES_PER_STEP * page_rows
    per = keys_per_step // SLC_LEN
    n_slots = past_len // keys_per_step + 1
    lane = np.arange(n_slots * LANES)
    col_block = np.where(lane % LANES < per, (lane // LANES) * per + lane % LANES, -1)
    col_block = np.where(col_block <= past_len // SLC_LEN, col_block, -1)
    ovl_s = jnp.asarray(_overlap_matrix(past_len // CMP_STRIDE, n_slots * LANES, col_block))
    wcache = cache_win_kv[l].reshape(db, WINDOW, KV_W)
    o_s = _attend_sample(page_table, q_s, gates_s, kcvc_s, _feature_major_pages(cache_slc_kv[l]), slc_s, wcache,
                         win_s, econst_t, ovl_s, past_len)
    y_s, a_s = _post(xs, o_s, ma_s, sgb_s, post_w, st_ffn, db * r8)
    tok = lambda a: a.reshape(db, r8, -1)[:, r8 - t:]
    win_new = jnp.concatenate([wcache, tok(win_s)], axis=1)[:, -WINDOW:]
    out_sample = (
        tok(y_s),
        tok(cmp_s).reshape((1, db, t) + kv_shape), tok(slc_s).reshape((1, db, t) + kv_shape),
        win_new.reshape((1, db, WINDOW) + kv_shape),
        p_s.reshape(db, r8, d)[None, :, r8 - (CONV_W - 1):], a_s.reshape(db, r8, dff)[None, :, r8 - (CONV_W - 1):])

    return (out_prompt[0], out_sample[0], out_prompt[1], out_prompt[2], out_prompt[3], out_prompt[4],
            out_prompt[5], out_sample[1], out_sample[2], out_sample[3], out_sample[4], out_sample[5])
```

```python
import functools

import numpy as np
import jax
import jax.numpy as jnp
from jax import lax
from jax.experimental import pallas as pl
from jax.experimental.pallas import tpu as pltpu

F32 = jnp.float32
BF16 = jnp.bfloat16

N_HEADS = 16
HEAD_DIM = 64
N_KV_HEADS = 4
N_BRANCH = 3
ROT_DIM = 16
ROPE_THETA = 500000.0
CMP_LEN = 32
CMP_STRIDE = 16
SLC_LEN = 64
N_SEL = 16
N_LOCAL = 2
WINDOW = 512
CONV_W = 3
NORM_EPS = 1e-6
FORCED_SCORE = 1e9

LANES = 128
SUBLANES = 8
KV_W = 2 * N_KV_HEADS * HEAD_DIM
MASK_BIG = 2.0 ** 100
VMEM_LIMIT = 56 * 1024 * 1024

ROW_TILE = 256
CMP_ROWS = 2048
Q_TILE = 128
K_TILE = 512
SAMPLE_ROWS = 8
PAGES_PER_STEP = 32


def _cparams(sem):
    return pltpu.CompilerParams(dimension_semantics=sem, vmem_limit_bytes=VMEM_LIMIT)


def _const_spec(shape):
    nd = len(shape)
    return pl.BlockSpec(shape, lambda *_: (0,) * nd, pipeline_mode=pl.Buffered(1))


def _dot(a, b):
    return jnp.dot(a, b, preferred_element_type=F32)


def _dot_nt(a, b):
    return lax.dot_general(a, b, (((1,), (1,)), ((), ())), preferred_element_type=F32)


def _rms(x, g):
    y = x * lax.rsqrt(jnp.mean(x * x, axis=-1, keepdims=True) + NORM_EPS)
    return y * g


def _rope_block(x, cos, sa, sb):
    return x * cos + pltpu.roll(x, 8, 1) * sa + pltpu.roll(x, LANES - 8, 1) * sb


def _shifted_conv(buf, p, w_ref, rows):
    buf[SUBLANES:SUBLANES + rows, :] = p
    p1 = buf[SUBLANES - 1:SUBLANES - 1 + rows, :]
    p2 = buf[SUBLANES - 2:SUBLANES - 2 + rows, :]
    return p2 * w_ref[0:1, :] + p1 * w_ref[1:2, :] + p * w_ref[2:3, :]


def _front_kernel(sample, tiles_per_seq, *refs):
    (x_ref, g_ref, cos_ref, sa_ref, sb_ref, wc_ref, wq_ref, wkv_ref, wgl_ref, wgab_ref, cw_ref, wco_ref) = refs[:12]
    if sample:
        st_ref, ma_ref, sgb_ref, q_ref, cmp_ref, slc_ref, win_ref, gl_ref, pst_ref, pbuf = refs[12:]
    else:
        ma_ref, sgb_ref, qt_ref, cmp_ref, slc_ref, win_ref, kb_ref, vt_ref, glt_ref, pst_ref, pbuf = refs[12:]
    rows, d = x_ref.shape
    u = _rms(x_ref[...], g_ref[...]).astype(BF16)

    zc = _dot(u, wc_ref[...])
    p = zc[:, d:2 * d] * zc[:, 0:d]
    if sample:
        row = lax.broadcasted_iota(jnp.int32, (rows, 1), 0) % SAMPLE_ROWS
        is_state = (row >= SAMPLE_ROWS - 4 - (CONV_W - 1)) & (row < SAMPLE_ROWS - 4)
        p = jnp.where(is_state, st_ref[...], p)
        pbuf[0:SUBLANES, :] = jnp.zeros((SUBLANES, d), F32)
    else:
        @pl.when(pl.program_id(0) % tiles_per_seq == 0)
        def _():
            pbuf[0:SUBLANES, :] = jnp.zeros((SUBLANES, d), F32)
    yc = _shifted_conv(pbuf, p, cw_ref, rows)
    tail = pbuf[rows:rows + SUBLANES, :]
    if sample:
        pst_ref[...] = p
    else:
        pst_ref[0] = tail
        pbuf[0:SUBLANES, :] = tail
    ya = _dot((zc[:, 2 * d:3 * d] * yc).astype(BF16), wco_ref[...])

    zg = _dot(u, wgab_ref[...])
    ma_ref[...] = jax.nn.sigmoid(zg[:, 0:d]) * ya
    sgb_ref[...] = jax.nn.sigmoid(zg[:, d:2 * d])

    cos, sa, sb = cos_ref[...], sa_ref[...], sb_ref[...]
    zq = _dot(u, wq_ref[...])
    scale = HEAD_DIM ** -0.5
    for j in range(zq.shape[1] // LANES):
        blk = _rope_block(zq[:, j * LANES:(j + 1) * LANES], cos, sa, sb) * scale
        if sample:
            q_ref[:, j * LANES:(j + 1) * LANES] = blk.astype(BF16)
        else:
            qt_ref[j * LANES:(j + 1) * LANES, :] = blk.T.astype(BF16)

    zkv = _dot(u, wkv_ref[...])
    per_branch = KV_W // LANES
    half = per_branch // 2
    outs = (cmp_ref, slc_ref, win_ref)
    for j in range(zkv.shape[1] // LANES):
        br, jj = divmod(j, per_branch)
        blk = zkv[:, j * LANES:(j + 1) * LANES]
        if jj < half:
            blk = _rope_block(blk, cos, sa, sb)
        outs[br][:, jj * LANES:(jj + 1) * LANES] = blk
        if br > 0 and not sample:
            c = (br - 1) * half + jj % half
            if jj < half:
                kb_ref[:, c * LANES:(c + 1) * LANES] = blk.astype(BF16)
            else:
                vt_ref[c * LANES:(c + 1) * LANES, :] = blk.T.astype(BF16)

    gl = jax.nn.sigmoid(_dot(u, wgl_ref[...]))
    if sample:
        gl_ref[...] = gl
    else:
        for j in range(gl.shape[1] // LANES):
            glt_ref[j * LANES:(j + 1) * LANES, :] = gl[:, j * LANES:(j + 1) * LANES].T


def _front(x, g, tabs, w, conv_w, state, seq_rows):
    n, d = x.shape
    sample = state is not None
    tm = n if sample else ROW_TILE
    tiles_per_seq = seq_rows // tm
    nt = n // tm
    n_seq = n // seq_rows
    qw = N_HEADS * HEAD_DIM
    row = lambda w_: pl.BlockSpec((tm, w_), lambda i: (i, 0))
    col = lambda h_: pl.BlockSpec((h_, tm), lambda i: (0, i))
    tab = pl.BlockSpec((tm, LANES), lambda i: (i % tiles_per_seq, 0))
    args = [x, g, *tabs, *w[:5], conv_w, w[5]]
    in_specs = [row(d), _const_spec((1, d)), tab, tab, tab] + [_const_spec(a.shape) for a in w[:5]] + [
        _const_spec(conv_w.shape), _const_spec(w[5].shape)]
    f32_rows = lambda w_: jax.ShapeDtypeStruct((n, w_), F32)
    if sample:
        args.append(state)
        in_specs.append(row(d))
        out_shape = [f32_rows(d), f32_rows(d), jax.ShapeDtypeStruct((n, qw), BF16), f32_rows(KV_W), f32_rows(KV_W),
                     f32_rows(KV_W), f32_rows(2 * LANES), f32_rows(d)]
        out_specs = [row(d), row(d), row(qw), row(KV_W), row(KV_W), row(KV_W), row(2 * LANES), row(d)]
    else:
        out_shape = [f32_rows(d), f32_rows(d), jax.ShapeDtypeStruct((qw, n), BF16), f32_rows(KV_W), f32_rows(KV_W),
                     f32_rows(KV_W), jax.ShapeDtypeStruct((n, KV_W), BF16), jax.ShapeDtypeStruct((KV_W, n), BF16),
                     jax.ShapeDtypeStruct((2 * LANES, n), F32), jax.ShapeDtypeStruct((n_seq, SUBLANES, d), F32)]
        out_specs = [row(d), row(d), col(qw), row(KV_W), row(KV_W), row(KV_W), row(KV_W), col(KV_W),
                     col(2 * LANES), pl.BlockSpec((1, SUBLANES, d), lambda i: (i // tiles_per_seq, 0, 0))]
    return pl.pallas_call(
        functools.partial(_front_kernel, sample, tiles_per_seq),
        grid=(nt,), in_specs=in_specs, out_specs=out_specs, out_shape=out_shape,
        scratch_shapes=[pltpu.VMEM((tm + SUBLANES, d), F32)],
        compiler_params=_cparams(("arbitrary",)), name="front_sample" if sample else "front_prompt",
    )(*args)


def _cmp_bias_kernel(pe_ref, w1_ref, b1_ref, o_ref):
    for kv in range(2):
        a = _dot(pe_ref[kv], w1_ref[kv])
        o_ref[kv] = jnp.broadcast_to(b1_ref[kv] + a[0:1, 0:LANES] + a[1:2, LANES:2 * LANES], (SUBLANES, LANES))


def _compress_tile(rows_refs, w1_ref, hb_ref, w2_ref, b2_ref, carry_ref, sh_ref, out_ref, vt_ref=None):
    nck = rows_refs[0].shape[0] // CMP_STRIDE
    for j, rows_ref in enumerate(rows_refs):
        kv = j // 2
        lhs = jnp.concatenate(
            [rows_ref[pl.ds(s, nck, stride=CMP_STRIDE), :].astype(BF16) for s in range(CMP_STRIDE)],
            axis=1)
        a = _dot(lhs, w1_ref[kv])
        a0 = a[:, 0:LANES]
        sh_ref[SUBLANES:SUBLANES + nck, :] = a0
        sh_ref[0:SUBLANES, :] = carry_ref[j]
        hid = sh_ref[SUBLANES - 1:SUBLANES - 1 + nck, :] + a[:, LANES:2 * LANES] + hb_ref[kv][0:1, :]
        carry_ref[j] = sh_ref[nck:nck + SUBLANES, :]
        o = _dot(jax.nn.silu(hid).astype(BF16), w2_ref[kv]) + b2_ref[kv]
        out_ref[:, j * LANES:(j + 1) * LANES] = o.astype(out_ref.dtype)
        if vt_ref is not None and kv == 1:
            vt_ref[(j - 2) * LANES:(j - 1) * LANES, :] = o.T.astype(vt_ref.dtype)


def _compress_prompt_kernel(r0, r1, r2, r3, w1_ref, hb_ref, w2_ref, b2_ref, out_ref, vt_ref, carry_ref, sh_ref):
    @pl.when(pl.program_id(1) == 0)
    def _():
        carry_ref[...] = jnp.zeros(carry_ref.shape, F32)
    _compress_tile((r0, r1, r2, r3), w1_ref, hb_ref, w2_ref, b2_ref, carry_ref, sh_ref, out_ref.at[0], vt_ref.at[0])


def _page_copies(pt_ref, cache_ref, buf_ref, sem_ref, slot, b, grp, n_pages):
    page_rows = cache_ref.shape[2]
    return [pltpu.make_async_copy(cache_ref.at[pt_ref[b, grp * n_pages + p]],
                                  buf_ref.at[slot, :, pl.ds(p * page_rows, page_rows)], sem_ref.at[slot])
            for p in range(n_pages)]


def _gather_step(pt_ref, cache_ref, buf_ref, sem_ref, n_pages):
    b, g = pl.program_id(0), pl.program_id(1)
    nb, ng = pl.num_programs(0), pl.num_programs(1)
    step = b * ng + g
    slot = step % 2

    @pl.when(step == 0)
    def _():
        for c in _page_copies(pt_ref, cache_ref, buf_ref, sem_ref, 0, 0, 0, n_pages):
            c.start()

    @pl.when(step + 1 < nb * ng)
    def _():
        nxt = step + 1
        for c in _page_copies(pt_ref, cache_ref, buf_ref, sem_ref, 1 - slot, nxt // ng, nxt % ng, n_pages):
            c.start()

    for c in _page_copies(pt_ref, cache_ref, buf_ref, sem_ref, slot, b, g, n_pages):
        c.wait()
    return slot


def _compress_sample_kernel(pt_ref, cache_ref, w1_ref, hb_ref, w2_ref, b2_ref, out_ref,
                            buf_ref, sem_ref, planes_ref, carry_ref, sh_ref):
    n_pages = buf_ref.shape[2] // cache_ref.shape[2]
    slot = _gather_step(pt_ref, cache_ref, buf_ref, sem_ref, n_pages)

    @pl.when(pl.program_id(1) == 0)
    def _():
        carry_ref[...] = jnp.zeros(carry_ref.shape, F32)
    n_planes = planes_ref.shape[0]
    for j in range(n_planes):
        planes_ref[j] = buf_ref[slot, j * LANES:(j + 1) * LANES, :].T
    planes = tuple(planes_ref.at[j] for j in range(n_planes))
    _compress_tile(planes, w1_ref, hb_ref, w2_ref, b2_ref, carry_ref, sh_ref, out_ref.at[0])


def _compress_weights(cmp_pe, cmp_w1, cmp_b1, cmp_w2, cmp_b2):
    r = CMP_LEN // CMP_STRIDE
    eye2 = jnp.eye(2, dtype=F32)
    w1 = cmp_w1.reshape(2, r, CMP_STRIDE, HEAD_DIM, HEAD_DIM)
    w1bd = jnp.einsum("krsde,hg->kshdrge", w1, eye2).reshape(2, CMP_STRIDE * LANES, r * LANES).astype(BF16)
    w2bd = jnp.einsum("kde,hg->khdge", cmp_w2, eye2).reshape(2, LANES, LANES).astype(BF16)
    pe = cmp_pe.reshape(2, r, CMP_STRIDE, 1, HEAD_DIM)
    pe = jnp.broadcast_to(pe, (2, r, CMP_STRIDE, 2, HEAD_DIM)).reshape(2, r, CMP_STRIDE * LANES)
    pe = jnp.pad(pe, ((0, 0), (0, SUBLANES - r), (0, 0))).astype(BF16)
    b1 = jnp.tile(cmp_b1, (1, 2)).reshape(2, 1, LANES)
    b2 = jnp.tile(cmp_b2, (1, 2)).reshape(2, 1, LANES)
    hb = pl.pallas_call(
        _cmp_bias_kernel, out_shape=jax.ShapeDtypeStruct((2, SUBLANES, LANES), F32), name="compress_bias",
    )(pe, w1bd, b1)
    return w1bd, hb, w2bd, b2


def _compress_prompt(cmp_rows, cw, n_seq):
    n = cmp_rows.shape[0]
    s = n // n_seq
    nck = CMP_ROWS // CMP_STRIDE
    tiles = s // CMP_ROWS
    w1bd, hb, w2bd, b2 = cw
    return pl.pallas_call(
        _compress_prompt_kernel,
        grid=(n_seq, tiles),
        in_specs=[pl.BlockSpec((CMP_ROWS, LANES), functools.partial(lambda j, b, t: (b * tiles + t, j), j))
                  for j in range(KV_W // LANES)] + [
                  _const_spec(w1bd.shape), _const_spec(hb.shape), _const_spec(w2bd.shape), _const_spec(b2.shape)],
        out_specs=[pl.BlockSpec((1, nck, KV_W), lambda b, t: (b, t, 0)),
                   pl.BlockSpec((1, KV_W // 2, nck), lambda b, t: (b, 0, t))],
        out_shape=[jax.ShapeDtypeStruct((n_seq, s // CMP_STRIDE, KV_W), BF16),
                   jax.ShapeDtypeStruct((n_seq, KV_W // 2, s // CMP_STRIDE), BF16)],
        scratch_shapes=[pltpu.VMEM((KV_W // LANES, SUBLANES, LANES), F32),
                        pltpu.VMEM((nck + SUBLANES, LANES), F32)],
        compiler_params=_cparams(("arbitrary", "arbitrary")), name="compress_prompt",
    )(cmp_rows, cmp_rows, cmp_rows, cmp_rows, w1bd, hb, w2bd, b2)


def _feature_major_pages(cache):
    return cache.transpose(0, 2, 3, 4, 1).reshape(cache.shape[0], KV_W, cache.shape[1])


def _compress_sample(page_table, cache_t, cw):
    db, n_pages = page_table.shape
    page_rows = cache_t.shape[2]
    pages_per = CMP_ROWS // page_rows
    groups = n_pages // pages_per
    nck = CMP_ROWS // CMP_STRIDE
    w1bd, hb, w2bd, b2 = cw
    cspec = lambda a: pl.BlockSpec(a.shape, lambda b, g, pt: (0,) * a.ndim, pipeline_mode=pl.Buffered(1))
    grid_spec = pltpu.PrefetchScalarGridSpec(
        num_scalar_prefetch=1, grid=(db, groups),
        in_specs=[pl.BlockSpec(memory_space=pl.ANY), cspec(w1bd), cspec(hb), cspec(w2bd), cspec(b2)],
        out_specs=pl.BlockSpec((1, nck, KV_W), lambda b, g, pt: (b, g, 0)),
        scratch_shapes=[pltpu.VMEM((2, KV_W, CMP_ROWS), F32), pltpu.SemaphoreType.DMA((2,)),
                        pltpu.VMEM((KV_W // LANES, CMP_ROWS, LANES), F32),
                        pltpu.VMEM((KV_W // LANES, SUBLANES, LANES), F32),
                        pltpu.VMEM((nck + SUBLANES, LANES), F32)])
    return pl.pallas_call(
        _compress_sample_kernel, grid_spec=grid_spec,
        out_shape=jax.ShapeDtypeStruct((db, n_pages * page_rows // CMP_STRIDE, KV_W), BF16),
        compiler_params=_cparams(("arbitrary", "arbitrary")), name="compress_sample",
    )(page_table, cache_t, w1bd, hb, w2bd, b2)


def _masked_softmax(s, mask):
    s = jnp.where(mask, s, -jnp.inf)
    m = jnp.max(s, axis=-1, keepdims=True)
    m = jnp.where(m == -jnp.inf, 0.0, m)
    e = jnp.where(mask, jnp.exp(s - m), 0.0)
    return e / jnp.maximum(jnp.sum(e, axis=-1, keepdims=True), 1e-30)


def _topk_not_selected(imp, n_pick):
    rows, width = imp.shape
    lane = lax.broadcasted_iota(jnp.int32, (rows, width), 1).astype(F32)

    def pick(_, carry):
        imp, notsel = carry
        m = jnp.max(imp, axis=-1, keepdims=True)
        first = jnp.min(jnp.where(imp == m, lane, float(width)), axis=-1, keepdims=True)
        hit = lane == first
        return jnp.where(hit, -jnp.inf, imp), jnp.where(hit, 0.0, notsel)

    _, notsel = lax.fori_loop(0, n_pick, pick, (imp, jnp.ones((rows, width), F32)), unroll=True)
    return notsel


def _flash_update(carry, s, v, v_transposed=False):
    m, l, acc = carry
    m_new = jnp.maximum(m, jnp.max(s, axis=-1, keepdims=True))
    alpha = jnp.exp(m - m_new)
    p = jnp.exp(s - m_new)
    l = alpha * l + jnp.sum(p, axis=-1, keepdims=True)
    pv = _dot_nt(p.astype(v.dtype), v) if v_transposed else _dot(p.astype(v.dtype), v)
    return m_new, l, alpha * acc + pv


def _half_mask(shape):
    return lax.broadcasted_iota(jnp.int32, shape, len(shape) - 1) < HEAD_DIM


def _stack_heads(q_blk, nr):
    first = _half_mask((q_blk.shape[0], LANES))
    zero = jnp.zeros((), q_blk.dtype)
    parts = []
    for gl in range(2):
        for r in range(nr):
            blk = q_blk[:, r * LANES:(r + 1) * LANES]
            parts.append(jnp.where(first if gl == 0 else ~first, blk, zero))
    return jnp.concatenate(parts, axis=0)


def _merge_pair(o, nr, t):
    first = _half_mask((t, LANES))
    return [jnp.where(first, o[r * t:(r + 1) * t], o[(nr + r) * t:(nr + r + 1) * t]) for r in range(nr)]


def _gate_vec(g_ref, br, r, t):
    c = br * 2 * (N_HEADS // N_KV_HEADS) + r * 2
    return jnp.where(_half_mask((t, LANES)), g_ref[:, c:c + 1], g_ref[:, c + 1:c + 2])


def _forced_importance(imp, blk, t_pos):
    cur = t_pos // SLC_LEN
    valid = blk <= cur
    forced = (blk == 0) | (valid & (blk > cur - N_LOCAL))
    return jnp.where(forced, FORCED_SCORE, jnp.where(valid, imp, -FORCED_SCORE))


def _masked_exp0(s, mask):
    s = jnp.where(mask, s, -jnp.inf)
    m = jnp.max(s, axis=0, keepdims=True)
    m = jnp.where(m == -jnp.inf, 0.0, m)
    e = jnp.where(mask, jnp.exp(s - m), 0.0)
    return e, 1.0 / jnp.maximum(jnp.sum(e, axis=0, keepdims=True), 1e-30)


def _topk_not_selected0(imp, n_pick):
    length, cols = imp.shape
    idx = lax.broadcasted_iota(jnp.int32, (length, cols), 0).astype(F32)

    def pick(_, carry):
        imp, notsel = carry
        m = jnp.max(imp, axis=0, keepdims=True)
        first = jnp.min(jnp.where(imp == m, idx, float(length)), axis=0, keepdims=True)
        hit = idx == first
        return jnp.where(hit, -jnp.inf, imp), jnp.where(hit, 0.0, notsel)

    _, notsel = lax.fori_loop(0, n_pick, pick, (imp, jnp.ones((length, cols), F32)), unroll=True)
    return notsel


def _flash_update0(carry, s, vt):
    m, l, acc = carry
    m_new = jnp.maximum(m, jnp.max(s, axis=0, keepdims=True))
    alpha = jnp.exp(m - m_new)
    p = jnp.exp(s - m_new)
    l = alpha * l + jnp.sum(p, axis=0, keepdims=True)
    return m_new, l, alpha * acc + _dot(vt, p.astype(vt.dtype))


def _attend_prompt_kernel(qt_ref, gt_ref, kc_ref, vct_ref, ks_ref, vst_ref, kw_ref, vwt_ref, e_ref, ovlt_ref,
                          o_ref, qa_ref, sa_ref, sb_ref, m_ref, l_ref, acc_ref):
    i = pl.program_id(2)
    nr = N_HEADS // N_KV_HEADS
    t = Q_TILE
    rows = 2 * nr * t
    t0 = i * t
    tq = t0 + lax.broadcasted_iota(jnp.int32, (1, t), 1)
    tq_rows = jnp.concatenate([tq] * (2 * nr), axis=1)
    first = lax.broadcasted_iota(jnp.int32, (LANES, t), 0) < HEAD_DIM
    zero = jnp.zeros((), BF16)
    qt = jnp.concatenate([jnp.where(first if gl == 0 else ~first, qt_ref[r * LANES:(r + 1) * LANES, :], zero)
                          for gl in range(2) for r in range(nr)], axis=1)

    ncmp = kc_ref.shape[1]
    cidx = lax.broadcasted_iota(jnp.int32, (ncmp, 1), 0)
    c_mask = (cidx >= 1) & (cidx * CMP_STRIDE + (CMP_LEN - CMP_STRIDE - 1) <= tq_rows)
    e_c, inv_c = _masked_exp0(_dot(kc_ref[0], qt), c_mask)
    p_c = e_c * inv_c
    o_c = _dot(vct_ref[0], p_c.astype(BF16))

    psum = []
    for gl in range(2):
        acc = p_c[:, gl * nr * t:gl * nr * t + t]
        for r in range(1, nr):
            acc = acc + p_c[:, (gl * nr + r) * t:(gl * nr + r + 1) * t]
        psum.append(acc)
    imp = jnp.dot(ovlt_ref[...], jnp.concatenate(psum, axis=1), preferred_element_type=F32,
                  precision=lax.Precision.HIGHEST)
    nslc = ovlt_ref.shape[0]
    blk = lax.broadcasted_iota(jnp.int32, (nslc, 1), 0)
    imp = _forced_importance(imp, blk, jnp.concatenate([tq, tq], axis=1))
    ns = _topk_not_selected0(imp, min(N_SEL, nslc)).astype(BF16)
    qa = jnp.concatenate([qt, jnp.concatenate([ns[:, 0:t]] * nr + [ns[:, t:2 * t]] * nr, axis=1)], axis=0)

    qa_ref[...] = qa

    def produce(s_ref, j):
        off = pl.multiple_of(j * K_TILE, K_TILE)
        ka = jnp.concatenate([ks_ref[pl.ds(off, K_TILE), :], e_ref[pl.ds(off, K_TILE), :]], axis=1)
        s_ref[...] = _dot(ka, qa_ref[...])

    def consume(s_ref, j, causal):
        off = pl.multiple_of(j * K_TILE, K_TILE)
        s = s_ref[...]
        if causal:
            kpos = off + lax.broadcasted_iota(jnp.int32, (K_TILE, 1), 0)
            s = jnp.where(kpos <= tq_rows, s, -MASK_BIG)
        m_ref[...], l_ref[...], acc_ref[...] = _flash_update0(
            (m_ref[...], l_ref[...], acc_ref[...]), s, vst_ref[:, pl.ds(off, K_TILE)])

    m_ref[...] = jnp.full(m_ref.shape, -jnp.inf, F32)
    l_ref[...] = jnp.zeros(l_ref.shape, F32)
    acc_ref[...] = jnp.zeros(acc_ref.shape, F32)
    last = (t0 + t - 1) // K_TILE
    produce(sa_ref, 0)

    def pair(jj, _):
        j = 2 * jj
        produce(sb_ref, j + 1)
        consume(sa_ref, j, False)
        produce(sa_ref, j + 2)
        consume(sb_ref, j + 1, False)
        return 0

    lax.fori_loop(0, last // 2, pair, 0)

    @pl.when(last % 2 == 1)
    def _():
        produce(sb_ref, last)
        consume(sa_ref, last - 1, False)
        consume(sb_ref, last, True)

    @pl.when(last % 2 == 0)
    def _():
        consume(sa_ref, last, True)

    o_s = acc_ref[...] * (1.0 / l_ref[...])

    wlen = WINDOW + t
    ws = pl.multiple_of(jnp.maximum(t0 - WINDOW, 0), t)
    wpos = ws + lax.broadcasted_iota(jnp.int32, (wlen, 1), 0)
    w_mask = (wpos <= tq_rows) & (wpos > tq_rows - WINDOW)
    e_w, inv_w = _masked_exp0(_dot(kw_ref[pl.ds(ws, wlen), :], qt), w_mask)
    o_w = _dot(vwt_ref[:, pl.ds(ws, wlen)], e_w.astype(BF16)) * inv_w

    for r in range(nr):
        c0, c1 = r * t, (nr + r) * t
        o = None
        for br, ob in enumerate((o_c, o_s, o_w)):
            c = br * 2 * nr + r * 2
            gate = jnp.where(first, gt_ref[c:c + 1, :], gt_ref[c + 1:c + 2, :])
            term = gate * jnp.where(first, ob[:, c0:c0 + t], ob[:, c1:c1 + t])
            o = term if o is None else o + term
        o_ref[:, r * LANES:(r + 1) * LANES] = o.T.astype(o_ref.dtype)


def _attend_prompt(qt, gt, kcvc, vct, kb, vt, econst, ovlt, n_seq):
    n = qt.shape[1]
    s = n // n_seq
    nq = s // Q_TILE
    nr = N_HEADS // N_KV_HEADS
    ncmp = kcvc.shape[1]
    qw = nr * LANES
    rows = 2 * nr * Q_TILE
    rows_blk = lambda c0: pl.BlockSpec((s, LANES), lambda b, k, i: (b, c0 + k))
    cols_blk = lambda c0: pl.BlockSpec((LANES, s), lambda b, k, i: (c0 + k, b))
    return pl.pallas_call(
        _attend_prompt_kernel,
        grid=(n_seq, 2, nq),
        in_specs=[pl.BlockSpec((qw, Q_TILE), lambda b, k, i: (k, b * nq + i)),
                  pl.BlockSpec((LANES, Q_TILE), lambda b, k, i: (k, b * nq + i)),
                  pl.BlockSpec((1, ncmp, LANES), lambda b, k, i: (b, 0, k)),
                  pl.BlockSpec((1, LANES, ncmp), lambda b, k, i: (b, k, 0)),
                  rows_blk(0), cols_blk(0), rows_blk(2), cols_blk(2),
                  pl.BlockSpec(econst.shape, lambda b, k, i: (0, 0), pipeline_mode=pl.Buffered(1)),
                  pl.BlockSpec(ovlt.shape, lambda b, k, i: (0, 0), pipeline_mode=pl.Buffered(1))],
        out_specs=pl.BlockSpec((Q_TILE, qw), lambda b, k, i: (b * nq + i, k)),
        out_shape=jax.ShapeDtypeStruct((n, N_HEADS * HEAD_DIM), BF16),
        scratch_shapes=[pltpu.VMEM((2 * LANES, rows), BF16), pltpu.VMEM((K_TILE, rows), F32),
                        pltpu.VMEM((K_TILE, rows), F32), pltpu.VMEM((1, rows), F32), pltpu.VMEM((1, rows), F32),
                        pltpu.VMEM((LANES, rows), F32)],
        compiler_params=_cparams(("arbitrary", "arbitrary", "arbitrary")), name="attend_prompt",
    )(qt, gt, kcvc, vct, kb, vt, kb, vt, econst, ovlt)


def _attend_sample_kernel(past_len, pt_ref, q_ref, g_ref, kcvc_ref, cache_ref, slc_ref, wcache_ref, win_ref,
                          e_ref, ovl_ref, o_ref, buf_ref, sem_ref, ns_ref, oc_ref, m_ref, l_ref, acc_ref):
    g = pl.program_id(1)
    ng = pl.num_programs(1)
    nr = N_HEADS // N_KV_HEADS
    t = SAMPLE_ROWS
    rows = 2 * nr * t
    n_pages = buf_ref.shape[2] // cache_ref.shape[2]
    keys = buf_ref.shape[2]
    slot = _gather_step(pt_ref, cache_ref, buf_ref, sem_ref, n_pages)
    tq = past_len - (t - 4) + lax.broadcasted_iota(jnp.int32, (t, 1), 0)
    tq_rows = jnp.concatenate([tq] * (2 * nr), axis=0)
    qs = [_stack_heads(q_ref[:, k * nr * LANES:(k + 1) * nr * LANES], nr) for k in range(2)]

    @pl.when(g == 0)
    def _():
        ncmp = kcvc_ref.shape[1]
        cidx = lax.broadcasted_iota(jnp.int32, (1, ncmp), 1)
        c_mask = (cidx >= 1) & (cidx * CMP_STRIDE + (CMP_LEN - CMP_STRIDE - 1) <= tq_rows)
        width = ovl_ref.shape[1]
        lane = lax.broadcasted_iota(jnp.int32, (1, width), 1)
        per = keys // SLC_LEN
        blk = (lane // LANES) * per + lane % LANES
        real = (lane % LANES < per) & (blk <= past_len // SLC_LEN)
        psums = []
        for k in range(2):
            p_c = _masked_softmax(_dot_nt(qs[k], kcvc_ref[0, :, k * LANES:(k + 1) * LANES]), c_mask)
            oc_ref[k] = _dot(p_c.astype(BF16), kcvc_ref[0, :, (2 + k) * LANES:(3 + k) * LANES])
            for gl in range(2):
                psum = p_c[gl * nr * t:gl * nr * t + t]
                for r in range(1, nr):
                    psum = psum + p_c[(gl * nr + r) * t:(gl * nr + r + 1) * t]
                psums.append(psum)
        imp = jnp.dot(jnp.concatenate(psums, axis=0), ovl_ref[...], preferred_element_type=F32,
                      precision=lax.Precision.HIGHEST)
        imp = jnp.where(real, _forced_importance(imp, blk, jnp.concatenate([tq] * 4, axis=0)), -jnp.inf)
        ns = _topk_not_selected(imp, N_SEL)
        for k in range(2):
            for gl in range(2):
                for r in range(nr):
                    ns_ref[k, (gl * nr + r) * t:(gl * nr + r + 1) * t, :] = ns[(2 * k + gl) * t:(2 * k + gl + 1) * t]
        m_ref[...] = jnp.full(m_ref.shape, -jnp.inf, F32)
        l_ref[...] = jnp.zeros(l_ref.shape, F32)
        acc_ref[...] = jnp.zeros(acc_ref.shape, F32)

    page = buf_ref.at[slot]
    for k in range(2):
        ns = ns_ref[k, :, pl.ds(pl.multiple_of(g * LANES, LANES), LANES)]
        qa = jnp.concatenate([qs[k], ns.astype(BF16)], axis=1)
        ka = jnp.concatenate([page[k * LANES:(k + 1) * LANES, :].astype(BF16), e_ref[...]], axis=0)
        v = page[(2 + k) * LANES:(3 + k) * LANES, :].astype(BF16)
        m_ref[k], l_ref[k], acc_ref[k] = _flash_update((m_ref[k], l_ref[k], acc_ref[k]), _dot(qa, ka), v,
                                                       v_transposed=True)

    @pl.when(g == ng - 1)
    def _():
        kidx = lax.broadcasted_iota(jnp.int32, (1, t), 1)
        kpos = past_len - (t - 4) + kidx
        new_ok = (kidx >= t - 4) & (kpos <= tq_rows)
        wb = wcache_ref.shape[1]
        wpos = past_len - wb + lax.broadcasted_iota(jnp.int32, (1, wb), 1)
        w_old = (wpos <= tq_rows) & (wpos > tq_rows - WINDOW) & (wpos >= 0)
        w_new = new_ok & (kpos > tq_rows - WINDOW)
        for k in range(2):
            kcol, vcol = slice(k * LANES, (k + 1) * LANES), slice((2 + k) * LANES, (3 + k) * LANES)
            qf = qs[k].astype(F32)
            s_new = jnp.where(new_ok, _dot_nt(qf, slc_ref[:, kcol]), -MASK_BIG)
            _, l, acc = _flash_update((m_ref[k], l_ref[k], acc_ref[k]), s_new, slc_ref[:, vcol])
            o_s = acc / l
            so = jnp.where(w_old, _dot_nt(qs[k], wcache_ref[0, :, kcol].astype(BF16)), -jnp.inf)
            sn = jnp.where(w_new, _dot_nt(qf, win_ref[:, kcol]), -jnp.inf)
            m = jnp.maximum(jnp.max(so, axis=-1, keepdims=True), jnp.max(sn, axis=-1, keepdims=True))
            m = jnp.where(m == -jnp.inf, 0.0, m)
            eo = jnp.where(w_old, jnp.exp(so - m), 0.0)
            en = jnp.where(w_new, jnp.exp(sn - m), 0.0)
            den = jnp.maximum(jnp.sum(eo, axis=-1, keepdims=True) + jnp.sum(en, axis=-1, keepdims=True), 1e-30)
            o_w = (_dot(eo.astype(BF16), wcache_ref[0, :, vcol].astype(BF16)) + _dot(en, win_ref[:, vcol])) / den
            oc, os_, ow = _merge_pair(oc_ref[k], nr, t), _merge_pair(o_s, nr, t), _merge_pair(o_w, nr, t)
            gk = g_ref.at[:, k * LANES:(k + 1) * LANES]
            for r in range(nr):
                o = _gate_vec(gk, 0, r, t) * oc[r] + _gate_vec(gk, 1, r, t) * os_[r] + _gate_vec(gk, 2, r, t) * ow[r]
                o_ref[:, (k * nr + r) * LANES:(k * nr + r + 1) * LANES] = o.astype(o_ref.dtype)


def _attend_sample(page_table, q, gates, kcvc, cache_t, slc_new, wcache, win_new, econst_t, ovl, past_len):
    db, n_pages = page_table.shape
    page_rows = cache_t.shape[2]
    econst = econst_t
    groups = n_pages // PAGES_PER_STEP
    keys = PAGES_PER_STEP * page_rows
    nr = N_HEADS // N_KV_HEADS
    t = SAMPLE_ROWS
    rows = 2 * nr * t
    qw = N_HEADS * HEAD_DIM
    cmap = lambda nd: (lambda b, g, pt: (0,) * nd)
    grid_spec = pltpu.PrefetchScalarGridSpec(
        num_scalar_prefetch=1, grid=(db, groups),
        in_specs=[pl.BlockSpec((t, qw), lambda b, g, pt: (b, 0)),
                  pl.BlockSpec((t, 2 * LANES), lambda b, g, pt: (b, 0)),
                  pl.BlockSpec((1,) + kcvc.shape[1:], lambda b, g, pt: (b, 0, 0)),
                  pl.BlockSpec(memory_space=pl.ANY),
                  pl.BlockSpec((t, KV_W), lambda b, g, pt: (b, 0)),
                  pl.BlockSpec((1,) + wcache.shape[1:], lambda b, g, pt: (b, 0, 0)),
                  pl.BlockSpec((t, KV_W), lambda b, g, pt: (b, 0)),
                  pl.BlockSpec(econst.shape, cmap(2), pipeline_mode=pl.Buffered(1)),
                  pl.BlockSpec(ovl.shape, cmap(2), pipeline_mode=pl.Buffered(1))],
        out_specs=pl.BlockSpec((t, qw), lambda b, g, pt: (b, 0)),
        scratch_shapes=[pltpu.VMEM((2, KV_W, keys), F32), pltpu.SemaphoreType.DMA((2,)),
                        pltpu.VMEM((2, rows, ovl.shape[1]), F32), pltpu.VMEM((2, rows, LANES), F32),
                        pltpu.VMEM((2, rows, 1), F32), pltpu.VMEM((2, rows, 1), F32),
                        pltpu.VMEM((2, rows, LANES), F32)])
    return pl.pallas_call(
        functools.partial(_attend_sample_kernel, past_len), grid_spec=grid_spec,
        out_shape=jax.ShapeDtypeStruct((db * t, qw), BF16),
        compiler_params=_cparams(("arbitrary", "arbitrary")), name="attend_sample",
    )(page_table, q, gates, kcvc, cache_t, slc_new, wcache, win_new, econst, ovl)


def _post_kernel(sample, tiles_per_seq, *refs):
    if sample:
        (x_ref, o_ref, ma_ref, sgb_ref, wao_ref, wo_ref, gf_ref, wfi_ref, fcw_ref, wfd_ref, gfin_ref, st_ref,
         y_ref, ast_ref, abuf) = refs
    else:
        (x_ref, o_ref, ma_ref, sgb_ref, wao_ref, wo_ref, gf_ref, wfi_ref, fcw_ref, wfd_ref, gfin_ref,
         y_ref, ast_ref, abuf) = refs
    rows = x_ref.shape[0]
    dff = fcw_ref.shape[1]
    m = ma_ref[...] + sgb_ref[...] * _dot(o_ref[...], wao_ref[...])
    h = x_ref[...] + _dot(m.astype(BF16), wo_ref[...])
    ag = _dot(_rms(h, gf_ref[...]).astype(BF16), wfi_ref[...])
    a = ag[:, 0:dff]
    if sample:
        row = lax.broadcasted_iota(jnp.int32, (rows, 1), 0) % SAMPLE_ROWS
        is_state = (row >= SAMPLE_ROWS - 4 - (CONV_W - 1)) & (row < SAMPLE_ROWS - 4)
        a = jnp.where(is_state, st_ref[...], a)
        abuf[0:SUBLANES, :] = jnp.zeros((SUBLANES, dff), F32)
    else:
        @pl.when(pl.program_id(0) % tiles_per_seq == 0)
        def _():
            abuf[0:SUBLANES, :] = jnp.zeros((SUBLANES, dff), F32)
    ac = _shifted_conv(abuf, a, fcw_ref, rows)
    tail = abuf[rows:rows + SUBLANES, :]
    if sample:
        ast_ref[...] = a
    else:
        ast_ref[0] = tail
        abuf[0:SUBLANES, :] = tail
    hh = h + _dot((jax.nn.silu(ac) * ag[:, dff:2 * dff]).astype(BF16), wfd_ref[...])
    y_ref[...] = _rms(hh, gfin_ref[...])


def _post(x, o, ma, sgb, w, state, seq_rows):
    n, d = x.shape
    sample = state is not None
    tm = n if sample else ROW_TILE
    tiles_per_seq = seq_rows // tm
    n_seq = n // seq_rows
    wao, wo, gf, wfi, fcw, wfd, gfin = w
    dff = fcw.shape[1]
    row = lambda w_: pl.BlockSpec((tm, w_), lambda i: (i, 0))
    args = [x, o, ma, sgb, *w]
    in_specs = [row(d), row(o.shape[1]), row(d), row(d)] + [_const_spec(a.shape) for a in w]
    if sample:
        args.append(state)
        in_specs.append(row(dff))
        ast_shape, ast_spec = jax.ShapeDtypeStruct((n, dff), F32), row(dff)
    else:
        ast_shape = jax.ShapeDtypeStruct((n_seq, SUBLANES, dff), F32)
        ast_spec = pl.BlockSpec((1, SUBLANES, dff), lambda i: (i // tiles_per_seq, 0, 0))
    return pl.pallas_call(
        functools.partial(_post_kernel, sample, tiles_per_seq),
        grid=(n // tm,), in_specs=in_specs, out_specs=[row(d), ast_spec],
        out_shape=[jax.ShapeDtypeStruct((n, d), F32), ast_shape],
        scratch_shapes=[pltpu.VMEM((tm + SUBLANES, dff), F32)],
        compiler_params=_cparams(("arbitrary",)), name="post_sample" if sample else "post_prompt",
    )(*args)


def _pair_head_order():
    nr = N_HEADS // N_KV_HEADS
    return [2 * nr * k + nr * half + r for k in range(2) for r in range(nr) for half in range(2)]


def _rope_tables(pos):
    half = ROT_DIM // 2
    inv_freq = jnp.power(ROPE_THETA, -jnp.arange(half, dtype=F32) * (2.0 / ROT_DIM))
    ang = pos.astype(F32)[:, None] * inv_freq[None, :]
    cos, sin = jnp.cos(ang), jnp.sin(ang)
    n = pos.shape[0]
    one, zero = jnp.ones((n, HEAD_DIM - ROT_DIM), F32), jnp.zeros((n, HEAD_DIM - ROT_DIM), F32)
    zh = jnp.zeros((n, half), F32)
    tabs = (jnp.concatenate([cos, cos, one], 1), jnp.concatenate([zh, sin, zero], 1),
            jnp.concatenate([-sin, zh, zero], 1))
    return tuple(jnp.tile(a, (1, LANES // HEAD_DIM)) for a in tabs)


def _overlap_matrix(n_rows, n_cols, col_block):
    c = np.arange(n_rows)[:, None] - 1
    blk = col_block[None, :]
    cs, ss = c * CMP_STRIDE, blk * SLC_LEN
    return ((c >= 0) & (blk >= 0) & (cs < ss + SLC_LEN) & (cs + CMP_LEN > ss)).astype(np.float32)


def _block_onehot(n_keys, transposed=False):
    e = (np.arange(n_keys)[:, None] // SLC_LEN == np.arange(LANES)[None, :]).astype(np.float32)
    return jnp.asarray(-MASK_BIG * (e.T if transposed else e), dtype=BF16)


def _split_w_in(w_in, d):
    conv_dim = d
    q_dim = N_HEADS * HEAD_DIM
    kv_dim = N_BRANCH * KV_W
    o0 = 3 * conv_dim
    wc = w_in[:, 0:o0]
    wq = w_in[:, o0:o0 + q_dim].reshape(d, N_HEADS, HEAD_DIM)[:, np.array(_pair_head_order())].reshape(d, q_dim)
    o1 = o0 + q_dim
    wkv = w_in[:, o1:o1 + kv_dim]
    o2 = o1 + kv_dim
    wl = w_in[:, o2:o2 + N_BRANCH * N_HEADS].reshape(d, N_HEADS, N_BRANCH)
    wl = wl[:, np.array(_pair_head_order())].reshape(d, 2, N_HEADS // 2, N_BRANCH).transpose(0, 1, 3, 2)
    wl = jnp.pad(wl.reshape(d, 2, N_BRANCH * N_HEADS // 2), ((0, 0), (0, 0), (0, LANES - N_BRANCH * N_HEADS // 2)))
    wl = wl.reshape(d, 2 * LANES)
    o3 = o2 + N_BRANCH * N_HEADS
    wgab = w_in[:, o3:o3 + 2 * d]
    return [a.astype(BF16) for a in (wc, wq, wkv, wl, wgab)]


def kernel(x_prompt, x_sample, cache_cmp_kv, cache_slc_kv, cache_win_kv, state_conv_mix, state_conv_ffn,
           page_table, norm_mix_g, w_in, conv_mix_w, w_conv_out, cmp_pe, cmp_w1, cmp_b1, cmp_w2, cmp_b2,
           w_attn_out, w_out, norm_ffn_g, w_ff_in, ff_conv_w, w_ff_down, norm_final_g):
    depth = w_in.shape[0]
    assert depth == 1, "single-layer step"
    b, s, d = x_prompt.shape
    db, t, _ = x_sample.shape
    page_rows = cache_cmp_kv.shape[2]
    past_len = page_table.shape[1] * page_rows
    assert t == 4 and s % K_TILE == 0 and s % CMP_ROWS == 0 and s >= WINDOW + Q_TILE
    assert past_len % (PAGES_PER_STEP * page_rows) == 0 and cache_win_kv.shape[2] == WINDOW
    assert N_SEL <= s // SLC_LEN <= LANES, "prompt selection blocks fit one lane block"
    dff = ff_conv_w.shape[2]
    l = 0

    order = np.array(_pair_head_order())
    front_w = _split_w_in(w_in[l], d) + [w_conv_out[l].astype(BF16)]
    post_w = [w_attn_out[l].reshape(N_HEADS, HEAD_DIM, d)[order].reshape(N_HEADS * HEAD_DIM, d).astype(BF16),
              w_out[l].astype(BF16), norm_ffn_g[l].reshape(1, d), w_ff_in[l].astype(BF16), ff_conv_w[l],
              w_ff_down[l].astype(BF16), norm_final_g.reshape(1, d)]
    g_mix = norm_mix_g[l].reshape(1, d)
    cw = _compress_weights(cmp_pe[l], cmp_w1[l], cmp_b1[l], cmp_w2[l], cmp_b2[l])
    econst = _block_onehot(s)
    econst_t = _block_onehot(PAGES_PER_STEP * page_rows, transposed=True)

    xp = x_prompt.reshape(b * s, d)
    ma, sgb, qt, cmp_p, slc_p, win_p, kb, vt, gt, pst = _front(
        xp, g_mix, _rope_tables(jnp.arange(s, dtype=jnp.int32)), front_w, conv_mix_w[l], None, s)
    kcvc, vct = _compress_prompt(cmp_p, cw, b)
    ovlt_p = jnp.asarray(_overlap_matrix(s // CMP_STRIDE, LANES, np.arange(LANES)).T)
    o = _attend_prompt(qt, gt, kcvc, vct, kb, vt, econst, ovlt_p, b)
    y_p, ast = _post(xp, o, ma, sgb, post_w, None, s)
    kv_shape = (2, N_KV_HEADS, HEAD_DIM)
    wb_p = min(WINDOW, s)
    out_prompt = (
        y_p.reshape(b, s, d),
        cmp_p.reshape((1, b, s) + kv_shape), slc_p.reshape((1, b, s) + kv_shape),
        win_p.reshape((b, s) + kv_shape)[None, :, s - wb_p:],
        pst[None, :, SUBLANES - (CONV_W - 1):], ast[None, :, SUBLANES - (CONV_W - 1):])

    r8 = SAMPLE_ROWS
    pad_rows = lambda a, lo: jnp.pad(a, ((0, 0), (lo, r8 - lo - a.shape[1]), (0, 0))).reshape(db * r8, a.shape[2])
    xs = pad_rows(x_sample, r8 - t)
    st_mix = pad_rows(state_conv_mix[l], r8 - t - (CONV_W - 1))
    st_ffn = pad_rows(state_conv_ffn[l], r8 - t - (CONV_W - 1))
    pos_s = past_len - (r8 - t) + jnp.arange(r8, dtype=jnp.int32)
    tabs_s = tuple(jnp.tile(a, (db, 1)) for a in _rope_tables(pos_s))
    ma_s, sgb_s, q_s, cmp_s, slc_s, win_s, gates_s, p_s = _front(
        xs, g_mix, tabs_s, front_w, conv_mix_w[l], st_mix, db * r8)
    kcvc_s = _compress_sample(page_table, _feature_major_pages(cache_cmp_kv[l]), cw)
    keys_per_step = PAGES_PER_STEP * page_rows
    per = keys_per_step // SLC_LEN
    n_slots = past_len // keys_per_step + 1
    lane = np.arange(n_slots * LANES)
    col_block = np.where(lane % LANES < per, (lane // LANES) * per + lane % LANES, -1)
    col_block = np.where(col_block <= past_len // SLC_LEN, col_block, -1)
    ovl_s = jnp.asarray(_overlap_matrix(past_len // CMP_STRIDE, n_slots * LANES, col_block))
    wcache = cache_win_kv[l].reshape(db, WINDOW, KV_W)
    o_s = _attend_sample(page_table, q_s, gates_s, kcvc_s, _feature_major_pages(cache_slc_kv[l]), slc_s, wcache,
                         win_s, econst_t, ovl_s, past_len)
    y_s, a_s = _post(xs, o_s, ma_s, sgb_s, post_w, st_ffn, db * r8)
    tok = lambda a: a.reshape(db, r8, -1)[:, r8 - t:]
    win_new = jnp.concatenate([wcache, tok(win_s)], axis=1)[:, -WINDOW:]
    out_sample = (
        tok(y_s),
        tok(cmp_s).reshape((1, db, t) + kv_shape), tok(slc_s).reshape((1, db, t) + kv_shape),
        win_new.reshape((1, db, WINDOW) + kv_shape),
        p_s.reshape(db, r8, d)[None, :, r8 - (CONV_W - 1):], a_s.reshape(db, r8, dff)[None, :, r8 - (CONV_W - 1):])

    return (out_prompt[0], out_sample[0], out_prompt[1], out_prompt[2], out_prompt[3], out_prompt[4],
            out_prompt[5], out_sample[1], out_sample[2], out_sample[3], out_sample[4], out_sample[5])
```

```python
import functools

import numpy as np
import jax
import jax.numpy as jnp
from jax import lax
from jax.experimental import pallas as pl
from jax.experimental.pallas import tpu as pltpu

F32 = jnp.float32
BF16 = jnp.bfloat16

N_HEADS = 16
HEAD_DIM = 64
N_KV_HEADS = 4
N_BRANCH = 3
ROT_DIM = 16
ROPE_THETA = 500000.0
CMP_LEN = 32
CMP_STRIDE = 16
SLC_LEN = 64
N_SEL = 16
N_LOCAL = 2
WINDOW = 512
CONV_W = 3
NORM_EPS = 1e-6
FORCED_SCORE = 1e9

LANES = 128
SUBLANES = 8
KV_W = 2 * N_KV_HEADS * HEAD_DIM
MASK_BIG = 2.0 ** 100
LOG2_E = 1.4426950408889634
VMEM_LIMIT = 56 * 1024 * 1024

ROW_TILE = 256
CMP_ROWS = 4096
Q_TILE = 128
K_TILE = 512
SAMPLE_ROWS = 8
PAGES_PER_STEP = 32


def _cparams(sem):
    return pltpu.CompilerParams(dimension_semantics=sem, vmem_limit_bytes=VMEM_LIMIT)


def _const_spec(shape):
    nd = len(shape)
    return pl.BlockSpec(shape, lambda *_: (0,) * nd, pipeline_mode=pl.Buffered(1))


def _dot(a, b):
    return jnp.dot(a, b, preferred_element_type=F32)


def _dot_nt(a, b):
    return lax.dot_general(a, b, (((1,), (1,)), ((), ())), preferred_element_type=F32)


def _rms(x, g):
    y = x * lax.rsqrt(jnp.mean(x * x, axis=-1, keepdims=True) + NORM_EPS)
    return y * g


def _rope_block(x, cos, sa, sb):
    return x * cos + pltpu.roll(x, 8, 1) * sa + pltpu.roll(x, LANES - 8, 1) * sb


def _shifted_conv(buf, p, w_ref, rows):
    buf[SUBLANES:SUBLANES + rows, :] = p
    p1 = buf[SUBLANES - 1:SUBLANES - 1 + rows, :]
    p2 = buf[SUBLANES - 2:SUBLANES - 2 + rows, :]
    return p2 * w_ref[0:1, :] + p1 * w_ref[1:2, :] + p * w_ref[2:3, :]


def _front_kernel(sample, tiles_per_seq, *refs):
    (x_ref, g_ref, cos_ref, sa_ref, sb_ref, wc_ref, wq_ref, wkv_ref, wgl_ref, wgab_ref, cw_ref, wco_ref) = refs[:12]
    if sample:
        st_ref, ma_ref, sgb_ref, q_ref, cmp_ref, slc_ref, win_ref, gl_ref, pst_ref, pbuf = refs[12:]
    else:
        ma_ref, sgb_ref, qt_ref, cmp_ref, slc_ref, win_ref, kb_ref, vt_ref, glt_ref, pst_ref, pbuf = refs[12:]
    rows, d = x_ref.shape
    u = _rms(x_ref[...], g_ref[...]).astype(BF16)

    zc = _dot(u, wc_ref[...])
    p = zc[:, d:2 * d] * zc[:, 0:d]
    if sample:
        row = lax.broadcasted_iota(jnp.int32, (rows, 1), 0) % SAMPLE_ROWS
        is_state = (row >= SAMPLE_ROWS - 4 - (CONV_W - 1)) & (row < SAMPLE_ROWS - 4)
        p = jnp.where(is_state, st_ref[...], p)
        pbuf[0:SUBLANES, :] = jnp.zeros((SUBLANES, d), F32)
    else:
        @pl.when(pl.program_id(0) % tiles_per_seq == 0)
        def _():
            pbuf[0:SUBLANES, :] = jnp.zeros((SUBLANES, d), F32)
    yc = _shifted_conv(pbuf, p, cw_ref, rows)
    tail = pbuf[rows:rows + SUBLANES, :]
    if sample:
        pst_ref[...] = p
    else:
        pst_ref[0] = tail
        pbuf[0:SUBLANES, :] = tail
    ya = _dot((zc[:, 2 * d:3 * d] * yc).astype(BF16), wco_ref[...])

    zg = _dot(u, wgab_ref[...])
    ma_ref[...] = jax.nn.sigmoid(zg[:, 0:d]) * ya
    sgb_ref[...] = jax.nn.sigmoid(zg[:, d:2 * d])

    cos, sa, sb = cos_ref[...], sa_ref[...], sb_ref[...]
    zq = _dot(u, wq_ref[...])
    scale = HEAD_DIM ** -0.5 * LOG2_E
    for j in range(zq.shape[1] // LANES):
        blk = _rope_block(zq[:, j * LANES:(j + 1) * LANES], cos, sa, sb) * scale
        if sample:
            q_ref[:, j * LANES:(j + 1) * LANES] = blk.astype(BF16)
        else:
            qt_ref[j * LANES:(j + 1) * LANES, :] = blk.T.astype(BF16)

    zkv = _dot(u, wkv_ref[...])
    per_branch = KV_W // LANES
    half = per_branch // 2
    outs = (cmp_ref, slc_ref, win_ref)
    for j in range(zkv.shape[1] // LANES):
        br, jj = divmod(j, per_branch)
        blk = zkv[:, j * LANES:(j + 1) * LANES]
        if jj < half:
            blk = _rope_block(blk, cos, sa, sb)
        outs[br][:, jj * LANES:(jj + 1) * LANES] = blk
        if br > 0 and not sample:
            c = (br - 1) * half + jj % half
            if jj < half:
                kb_ref[:, c * LANES:(c + 1) * LANES] = blk.astype(BF16)
            else:
                vt_ref[c * LANES:(c + 1) * LANES, :] = blk.T.astype(BF16)

    gl = jax.nn.sigmoid(_dot(u, wgl_ref[...]))
    if sample:
        gl_ref[...] = gl
    else:
        for j in range(gl.shape[1] // LANES):
            glt_ref[j * LANES:(j + 1) * LANES, :] = gl[:, j * LANES:(j + 1) * LANES].T


def _front(x, g, tabs, w, conv_w, state, seq_rows):
    n, d = x.shape
    sample = state is not None
    tm = n if sample else ROW_TILE
    tiles_per_seq = seq_rows // tm
    nt = n // tm
    n_seq = n // seq_rows
    qw = N_HEADS * HEAD_DIM
    row = lambda w_: pl.BlockSpec((tm, w_), lambda i: (i, 0))
    col = lambda h_: pl.BlockSpec((h_, tm), lambda i: (0, i))
    tab = pl.BlockSpec((tm, LANES), lambda i: (i % tiles_per_seq, 0))
    args = [x, g, *tabs, *w[:5], conv_w, w[5]]
    in_specs = [row(d), _const_spec((1, d)), tab, tab, tab] + [_const_spec(a.shape) for a in w[:5]] + [
        _const_spec(conv_w.shape), _const_spec(w[5].shape)]
    f32_rows = lambda w_: jax.ShapeDtypeStruct((n, w_), F32)
    if sample:
        args.append(state)
        in_specs.append(row(d))
        out_shape = [f32_rows(d), f32_rows(d), jax.ShapeDtypeStruct((n, qw), BF16), f32_rows(KV_W), f32_rows(KV_W),
                     f32_rows(KV_W), f32_rows(2 * LANES), f32_rows(d)]
        out_specs = [row(d), row(d), row(qw), row(KV_W), row(KV_W), row(KV_W), row(2 * LANES), row(d)]
    else:
        out_shape = [f32_rows(d), f32_rows(d), jax.ShapeDtypeStruct((qw, n), BF16), f32_rows(KV_W), f32_rows(KV_W),
                     f32_rows(KV_W), jax.ShapeDtypeStruct((n, KV_W), BF16), jax.ShapeDtypeStruct((KV_W, n), BF16),
                     jax.ShapeDtypeStruct((2 * LANES, n), F32), jax.ShapeDtypeStruct((n_seq, SUBLANES, d), F32)]
        out_specs = [row(d), row(d), col(qw), row(KV_W), row(KV_W), row(KV_W), row(KV_W), col(KV_W),
                     col(2 * LANES), pl.BlockSpec((1, SUBLANES, d), lambda i: (i // tiles_per_seq, 0, 0))]
    return pl.pallas_call(
        functools.partial(_front_kernel, sample, tiles_per_seq),
        grid=(nt,), in_specs=in_specs, out_specs=out_specs, out_shape=out_shape,
        scratch_shapes=[pltpu.VMEM((tm + SUBLANES, d), F32)],
        compiler_params=_cparams(("arbitrary",)), name="front_sample" if sample else "front_prompt",
    )(*args)


def _cmp_bias_kernel(pe_ref, w1_ref, b1_ref, o_ref):
    for kv in range(2):
        a = _dot(pe_ref[kv], w1_ref[kv])
        o_ref[kv] = jnp.broadcast_to(b1_ref[kv] + a[0:1, 0:LANES] + a[1:2, LANES:2 * LANES], (SUBLANES, LANES))


def _compress_tile(lhs_of, w1_ref, hb_ref, w2_ref, b2_ref, carry_ref, sh_ref, out_ref, vt_ref=None):
    nck = out_ref.shape[0]
    for j in range(KV_W // LANES):
        kv = j // 2
        a = _dot(lhs_of(j), w1_ref[kv])
        a0 = a[:, 0:LANES]
        sh_ref[SUBLANES:SUBLANES + nck, :] = a0
        sh_ref[0:SUBLANES, :] = carry_ref[j]
        hid = sh_ref[SUBLANES - 1:SUBLANES - 1 + nck, :] + a[:, LANES:2 * LANES] + hb_ref[kv][0:1, :]
        carry_ref[j] = sh_ref[nck:nck + SUBLANES, :]
        o = _dot(jax.nn.silu(hid).astype(BF16), w2_ref[kv]) + b2_ref[kv]
        out_ref[:, j * LANES:(j + 1) * LANES] = o.astype(out_ref.dtype)
        if vt_ref is not None and kv == 1:
            vt_ref[(j - 2) * LANES:(j - 1) * LANES, :] = o.T.astype(vt_ref.dtype)


def _compress_prompt_kernel(r0, r1, r2, r3, w1_ref, hb_ref, w2_ref, b2_ref, out_ref, vt_ref, carry_ref, sh_ref):
    @pl.when(pl.program_id(1) == 0)
    def _():
        carry_ref[...] = jnp.zeros(carry_ref.shape, F32)
    planes = (r0, r1, r2, r3)
    nck = out_ref.shape[1]

    def lhs_of(j):
        return jnp.concatenate([planes[j][pl.ds(s, nck, stride=CMP_STRIDE), :].astype(BF16)
                                for s in range(CMP_STRIDE)], axis=1)

    _compress_tile(lhs_of, w1_ref, hb_ref, w2_ref, b2_ref, carry_ref, sh_ref, out_ref.at[0], vt_ref.at[0])


def _page_copies(pt_ref, cache_ref, buf_ref, sem_ref, slot, b, grp, n_pages):
    page_rows = cache_ref.shape[2]
    return [pltpu.make_async_copy(cache_ref.at[pt_ref[b, grp * n_pages + p]],
                                  buf_ref.at[slot, :, pl.ds(p * page_rows, page_rows)], sem_ref.at[slot])
            for p in range(n_pages)]


def _gather_step(pt_ref, cache_ref, buf_ref, sem_ref, n_pages):
    b, g = pl.program_id(0), pl.program_id(1)
    nb, ng = pl.num_programs(0), pl.num_programs(1)
    step = b * ng + g
    slot = step % 2

    @pl.when(step == 0)
    def _():
        for c in _page_copies(pt_ref, cache_ref, buf_ref, sem_ref, 0, 0, 0, n_pages):
            c.start()

    @pl.when(step + 1 < nb * ng)
    def _():
        nxt = step + 1
        for c in _page_copies(pt_ref, cache_ref, buf_ref, sem_ref, 1 - slot, nxt // ng, nxt % ng, n_pages):
            c.start()

    for c in _page_copies(pt_ref, cache_ref, buf_ref, sem_ref, slot, b, g, n_pages):
        c.wait()
    return slot


def _compress_sample_kernel(pt_ref, cache_ref, perm_ref, w1_ref, hb_ref, w2_ref, b2_ref, out_ref,
                            buf_ref, sem_ref, carry_ref, sh_ref):
    n_pages = buf_ref.shape[2] // cache_ref.shape[2]
    slot = _gather_step(pt_ref, cache_ref, buf_ref, sem_ref, n_pages)

    @pl.when(pl.program_id(1) == 0)
    def _():
        carry_ref[...] = jnp.zeros(carry_ref.shape, F32)

    span = perm_ref.shape[0]
    ck = span // CMP_STRIDE

    def permuted(half, g):
        x = buf_ref[slot, half * 2 * LANES:(half + 1) * 2 * LANES, g * span:(g + 1) * span]
        return _dot_nt(perm_ref[...], x.astype(BF16)).astype(BF16)

    tiles = [[permuted(half, g) for g in range(buf_ref.shape[2] // span)] for half in range(2)]

    def lhs_of(j):
        half, jj = divmod(j, 2)
        return jnp.concatenate(
            [jnp.concatenate([tl[s * ck:(s + 1) * ck, jj * LANES:(jj + 1) * LANES] for tl in tiles[half]], axis=0)
             for s in range(CMP_STRIDE)], axis=1)

    _compress_tile(lhs_of, w1_ref, hb_ref, w2_ref, b2_ref, carry_ref, sh_ref, out_ref.at[0])


def _compress_weights(cmp_pe, cmp_w1, cmp_b1, cmp_w2, cmp_b2):
    r = CMP_LEN // CMP_STRIDE
    eye2 = jnp.eye(2, dtype=F32)
    w1 = cmp_w1.reshape(2, r, CMP_STRIDE, HEAD_DIM, HEAD_DIM)
    w1bd = jnp.einsum("krsde,hg->kshdrge", w1, eye2).reshape(2, CMP_STRIDE * LANES, r * LANES).astype(BF16)
    w2bd = jnp.einsum("kde,hg->khdge", cmp_w2, eye2).reshape(2, LANES, LANES).astype(BF16)
    pe = cmp_pe.reshape(2, r, CMP_STRIDE, 1, HEAD_DIM)
    pe = jnp.broadcast_to(pe, (2, r, CMP_STRIDE, 2, HEAD_DIM)).reshape(2, r, CMP_STRIDE * LANES)
    pe = jnp.pad(pe, ((0, 0), (0, SUBLANES - r), (0, 0))).astype(BF16)
    b1 = jnp.tile(cmp_b1, (1, 2)).reshape(2, 1, LANES)
    b2 = jnp.tile(cmp_b2, (1, 2)).reshape(2, 1, LANES)
    hb = pl.pallas_call(
        _cmp_bias_kernel, out_shape=jax.ShapeDtypeStruct((2, SUBLANES, LANES), F32), name="compress_bias",
    )(pe, w1bd, b1)
    return w1bd, hb, w2bd, b2


def _compress_prompt(cmp_rows, cw, n_seq):
    n = cmp_rows.shape[0]
    s = n // n_seq
    nck = CMP_ROWS // CMP_STRIDE
    tiles = s // CMP_ROWS
    w1bd, hb, w2bd, b2 = cw
    return pl.pallas_call(
        _compress_prompt_kernel,
        grid=(n_seq, tiles),
        in_specs=[pl.BlockSpec((CMP_ROWS, LANES), functools.partial(lambda j, b, t: (b * tiles + t, j), j))
                  for j in range(KV_W // LANES)] + [
                  _const_spec(w1bd.shape), _const_spec(hb.shape), _const_spec(w2bd.shape), _const_spec(b2.shape)],
        out_specs=[pl.BlockSpec((1, nck, KV_W), lambda b, t: (b, t, 0)),
                   pl.BlockSpec((1, KV_W // 2, nck), lambda b, t: (b, 0, t))],
        out_shape=[jax.ShapeDtypeStruct((n_seq, s // CMP_STRIDE, KV_W), BF16),
                   jax.ShapeDtypeStruct((n_seq, KV_W // 2, s // CMP_STRIDE), BF16)],
        scratch_shapes=[pltpu.VMEM((KV_W // LANES, SUBLANES, LANES), F32),
                        pltpu.VMEM((nck + SUBLANES, LANES), F32)],
        compiler_params=_cparams(("arbitrary", "arbitrary")), name="compress_prompt",
    )(cmp_rows, cmp_rows, cmp_rows, cmp_rows, w1bd, hb, w2bd, b2)


def _feature_major_pages(cache):
    return cache.transpose(0, 2, 3, 4, 1).reshape(cache.shape[0], KV_W, cache.shape[1])


def _compress_sample(page_table, cache_t, cw):
    db, n_pages = page_table.shape
    page_rows = cache_t.shape[2]
    pages_per = CMP_ROWS // page_rows
    groups = n_pages // pages_per
    nck = CMP_ROWS // CMP_STRIDE
    w1bd, hb, w2bd, b2 = cw
    span = 2 * page_rows
    ck = span // CMP_STRIDE
    row = np.arange(span)
    s_of, c_of = row // ck, row % ck
    per_page = page_rows // CMP_STRIDE
    tok = (c_of // per_page) * page_rows + CMP_STRIDE * (c_of % per_page) + s_of
    perm = jnp.asarray(tok[:, None] == np.arange(span)[None, :], dtype=BF16)
    cspec = lambda a: pl.BlockSpec(a.shape, lambda b, g, pt: (0,) * a.ndim, pipeline_mode=pl.Buffered(1))
    grid_spec = pltpu.PrefetchScalarGridSpec(
        num_scalar_prefetch=1, grid=(db, groups),
        in_specs=[pl.BlockSpec(memory_space=pl.ANY), cspec(perm), cspec(w1bd), cspec(hb), cspec(w2bd), cspec(b2)],
        out_specs=pl.BlockSpec((1, nck, KV_W), lambda b, g, pt: (b, g, 0)),
        scratch_shapes=[pltpu.VMEM((2, KV_W, CMP_ROWS), F32), pltpu.SemaphoreType.DMA((2,)),
                        pltpu.VMEM((KV_W // LANES, SUBLANES, LANES), F32),
                        pltpu.VMEM((nck + SUBLANES, LANES), F32)])
    return pl.pallas_call(
        _compress_sample_kernel, grid_spec=grid_spec,
        out_shape=jax.ShapeDtypeStruct((db, n_pages * page_rows // CMP_STRIDE, KV_W), BF16),
        compiler_params=_cparams(("arbitrary", "arbitrary")), name="compress_sample",
    )(page_table, cache_t, perm, w1bd, hb, w2bd, b2)


def _masked_softmax(s, mask):
    s = jnp.where(mask, s, -jnp.inf)
    m = jnp.max(s, axis=-1, keepdims=True)
    m = jnp.where(m == -jnp.inf, 0.0, m)
    e = jnp.where(mask, jnp.exp2(s - m), 0.0)
    return e / jnp.maximum(jnp.sum(e, axis=-1, keepdims=True), 1e-30)


def _select_blocks(imp, blk, t_pos, axis):
    cur = t_pos // SLC_LEN
    valid = blk <= cur
    forced = (blk == 0) | (valid & (blk > cur - N_LOCAL))
    cand = jnp.where(forced, -jnp.inf, jnp.where(valid, imp, -FORCED_SCORE))
    cand = jnp.where(blk >= 0, cand, -jnp.inf)
    length = imp.shape[axis]
    idx = lax.broadcasted_iota(jnp.int32, imp.shape, axis).astype(F32)

    def pick(_, carry):
        cand, notsel = carry
        m = jnp.max(cand, axis=axis, keepdims=True)
        first = jnp.min(jnp.where(cand == m, idx, float(length)), axis=axis, keepdims=True)
        hit = idx == first
        return jnp.where(hit, -jnp.inf, cand), jnp.where(hit, 0.0, notsel)

    _, notsel = lax.fori_loop(0, N_SEL - (N_LOCAL + 1), pick, (cand, jnp.where(forced, 0.0, 1.0)), unroll=True)
    return notsel


def _flash_update(carry, s, v, v_transposed=False):
    m, l, acc = carry
    m_new = jnp.maximum(m, jnp.max(s, axis=-1, keepdims=True))
    alpha = jnp.exp2(m - m_new)
    p = jnp.exp2(s - m_new)
    l = alpha * l + jnp.sum(p, axis=-1, keepdims=True)
    pv = _dot_nt(p.astype(v.dtype), v) if v_transposed else _dot(p.astype(v.dtype), v)
    return m_new, l, alpha * acc + pv


def _half_mask(shape):
    return lax.broadcasted_iota(jnp.int32, shape, len(shape) - 1) < HEAD_DIM


def _stack_heads(q_blk, nr):
    first = _half_mask((q_blk.shape[0], LANES))
    zero = jnp.zeros((), q_blk.dtype)
    parts = []
    for gl in range(2):
        for r in range(nr):
            blk = q_blk[:, r * LANES:(r + 1) * LANES]
            parts.append(jnp.where(first if gl == 0 else ~first, blk, zero))
    return jnp.concatenate(parts, axis=0)


def _merge_pair(o, nr, t):
    first = _half_mask((t, LANES))
    return [jnp.where(first, o[r * t:(r + 1) * t], o[(nr + r) * t:(nr + r + 1) * t]) for r in range(nr)]


def _gate_vec(g_ref, br, r, t):
    c = br * 2 * (N_HEADS // N_KV_HEADS) + r * 2
    return jnp.where(_half_mask((t, LANES)), g_ref[:, c:c + 1], g_ref[:, c + 1:c + 2])


def _masked_exp0(s, mask):
    s = jnp.where(mask, s, -jnp.inf)
    m = jnp.max(s, axis=0, keepdims=True)
    m = jnp.where(m == -jnp.inf, 0.0, m)
    e = jnp.where(mask, jnp.exp2(s - m), 0.0)
    return e, 1.0 / jnp.maximum(jnp.sum(e, axis=0, keepdims=True), 1e-30)


def _flash_update0(carry, s, vt):
    m, l, acc = carry
    m_new = jnp.maximum(m, jnp.max(s, axis=0, keepdims=True))
    alpha = jnp.exp2(m - m_new)
    p = jnp.exp2(s - m_new)
    l = alpha * l + jnp.sum(p, axis=0, keepdims=True)
    return m_new, l, alpha * acc + _dot(vt, p.astype(vt.dtype))


def _attend_prompt_kernel(qt_ref, gt_ref, kc_ref, vct_ref, ks_ref, vst_ref, kw_ref, vwt_ref, e_ref, ovlt_ref,
                          o_ref, qa_ref, sa_ref, sb_ref, m_ref, l_ref, acc_ref, ow_ref):
    i = pl.program_id(2)
    nr = N_HEADS // N_KV_HEADS
    t = Q_TILE
    rows = 2 * nr * t
    t0 = i * t
    tq = t0 + lax.broadcasted_iota(jnp.int32, (1, t), 1)
    tq_rows = jnp.concatenate([tq] * (2 * nr), axis=1)
    first = lax.broadcasted_iota(jnp.int32, (LANES, t), 0) < HEAD_DIM
    zero = jnp.zeros((), BF16)
    qt = jnp.concatenate([jnp.where(first if gl == 0 else ~first, qt_ref[r * LANES:(r + 1) * LANES, :], zero)
                          for gl in range(2) for r in range(nr)], axis=1)

    wlen = WINDOW + t
    ws = pl.multiple_of(jnp.maximum(t0 - WINDOW, 0), t)
    wpos = ws + lax.broadcasted_iota(jnp.int32, (wlen, 1), 0)
    w_mask = (wpos <= tq_rows) & (wpos > tq_rows - WINDOW)
    e_w, inv_w = _masked_exp0(_dot(kw_ref[pl.ds(ws, wlen), :], qt), w_mask)
    ow_ref[...] = _dot(vwt_ref[:, pl.ds(ws, wlen)], e_w.astype(BF16)) * inv_w

    ncmp = kc_ref.shape[1]
    cidx = lax.broadcasted_iota(jnp.int32, (ncmp, 1), 0)
    c_mask = (cidx >= 1) & (cidx * CMP_STRIDE + (CMP_LEN - CMP_STRIDE - 1) <= tq_rows)
    e_c, inv_c = _masked_exp0(_dot(kc_ref[0], qt), c_mask)
    p_c = e_c * inv_c
    o_c = _dot(vct_ref[0], p_c.astype(BF16))

    psum = []
    for gl in range(2):
        acc = p_c[:, gl * nr * t:gl * nr * t + t]
        for r in range(1, nr):
            acc = acc + p_c[:, (gl * nr + r) * t:(gl * nr + r + 1) * t]
        psum.append(acc)
    imp = jnp.dot(ovlt_ref[...], jnp.concatenate(psum, axis=1), preferred_element_type=F32,
                  precision=lax.Precision.HIGHEST)
    nslc = ovlt_ref.shape[0]
    blk = lax.broadcasted_iota(jnp.int32, (nslc, 1), 0)
    ns = _select_blocks(imp, blk, jnp.concatenate([tq, tq], axis=1), axis=0).astype(BF16)
    qa = jnp.concatenate([qt, jnp.concatenate([ns[:, 0:t]] * nr + [ns[:, t:2 * t]] * nr, axis=1)], axis=0)

    qa_ref[...] = qa

    def produce(s_ref, j):
        off = pl.multiple_of(j * K_TILE, K_TILE)
        ka = jnp.concatenate([ks_ref[pl.ds(off, K_TILE), :], e_ref[pl.ds(off, K_TILE), :]], axis=1)
        s_ref[...] = _dot(ka, qa_ref[...])

    def consume(s_ref, j, causal):
        off = pl.multiple_of(j * K_TILE, K_TILE)
        s = s_ref[...]
        if causal:
            kpos = off + lax.broadcasted_iota(jnp.int32, (K_TILE, 1), 0)
            s = jnp.where(kpos <= tq_rows, s, -MASK_BIG)
        m_ref[...], l_ref[...], acc_ref[...] = _flash_update0(
            (m_ref[...], l_ref[...], acc_ref[...]), s, vst_ref[:, pl.ds(off, K_TILE)])

    m_ref[...] = jnp.full(m_ref.shape, -jnp.inf, F32)
    l_ref[...] = jnp.zeros(l_ref.shape, F32)
    acc_ref[...] = jnp.zeros(acc_ref.shape, F32)
    last = (t0 + t - 1) // K_TILE
    produce(sa_ref, 0)

    def pair(jj, _):
        j = 2 * jj
        produce(sb_ref, j + 1)
        consume(sa_ref, j, False)
        produce(sa_ref, j + 2)
        consume(sb_ref, j + 1, False)
        return 0

    lax.fori_loop(0, last // 2, pair, 0)

    @pl.when(last % 2 == 1)
    def _():
        produce(sb_ref, last)
        consume(sa_ref, last - 1, False)
        consume(sb_ref, last, True)

    @pl.when(last % 2 == 0)
    def _():
        consume(sa_ref, last, True)

    o_s = acc_ref[...] * (1.0 / l_ref[...])

    o_w = ow_ref[...]

    for r in range(nr):
        c0, c1 = r * t, (nr + r) * t
        o = None
        for br, ob in enumerate((o_c, o_s, o_w)):
            c = br * 2 * nr + r * 2
            gate = jnp.where(first, gt_ref[c:c + 1, :], gt_ref[c + 1:c + 2, :])
            term = gate * jnp.where(first, ob[:, c0:c0 + t], ob[:, c1:c1 + t])
            o = term if o is None else o + term
        o_ref[:, r * LANES:(r + 1) * LANES] = o.T.astype(o_ref.dtype)


def _attend_prompt(qt, gt, kcvc, vct, kb, vt, econst, ovlt, n_seq):
    n = qt.shape[1]
    s = n // n_seq
    nq = s // Q_TILE
    nr = N_HEADS // N_KV_HEADS
    ncmp = kcvc.shape[1]
    qw = nr * LANES
    rows = 2 * nr * Q_TILE
    rows_blk = lambda c0: pl.BlockSpec((s, LANES), lambda b, k, i: (b, c0 + k))
    cols_blk = lambda c0: pl.BlockSpec((LANES, s), lambda b, k, i: (c0 + k, b))
    return pl.pallas_call(
        _attend_prompt_kernel,
        grid=(n_seq, 2, nq),
        in_specs=[pl.BlockSpec((qw, Q_TILE), lambda b, k, i: (k, b * nq + i)),
                  pl.BlockSpec((LANES, Q_TILE), lambda b, k, i: (k, b * nq + i)),
                  pl.BlockSpec((1, ncmp, LANES), lambda b, k, i: (b, 0, k)),
                  pl.BlockSpec((1, LANES, ncmp), lambda b, k, i: (b, k, 0)),
                  rows_blk(0), cols_blk(0), rows_blk(2), cols_blk(2),
                  pl.BlockSpec(econst.shape, lambda b, k, i: (0, 0), pipeline_mode=pl.Buffered(1)),
                  pl.BlockSpec(ovlt.shape, lambda b, k, i: (0, 0), pipeline_mode=pl.Buffered(1))],
        out_specs=pl.BlockSpec((Q_TILE, qw), lambda b, k, i: (b * nq + i, k)),
        out_shape=jax.ShapeDtypeStruct((n, N_HEADS * HEAD_DIM), BF16),
        scratch_shapes=[pltpu.VMEM((2 * LANES, rows), BF16), pltpu.VMEM((K_TILE, rows), F32),
                        pltpu.VMEM((K_TILE, rows), F32), pltpu.VMEM((1, rows), F32), pltpu.VMEM((1, rows), F32),
                        pltpu.VMEM((LANES, rows), F32), pltpu.VMEM((LANES, rows), F32)],
        compiler_params=_cparams(("arbitrary", "arbitrary", "arbitrary")), name="attend_prompt",
    )(qt, gt, kcvc, vct, kb, vt, kb, vt, econst, ovlt)


def _attend_sample_kernel(past_len, pt_ref, q_ref, g_ref, kcvc_ref, cache_ref, slc_ref, wcache_ref, win_ref,
                          e_ref, ovl_ref, o_ref, buf_ref, sem_ref, ns_ref, oc_ref, m_ref, l_ref, acc_ref):
    g = pl.program_id(1)
    ng = pl.num_programs(1)
    nr = N_HEADS // N_KV_HEADS
    t = SAMPLE_ROWS
    rows = 2 * nr * t
    n_pages = buf_ref.shape[2] // cache_ref.shape[2]
    keys = buf_ref.shape[2]
    slot = _gather_step(pt_ref, cache_ref, buf_ref, sem_ref, n_pages)
    tq = past_len - (t - 4) + lax.broadcasted_iota(jnp.int32, (t, 1), 0)
    tq_rows = jnp.concatenate([tq] * (2 * nr), axis=0)
    qs = [_stack_heads(q_ref[:, k * nr * LANES:(k + 1) * nr * LANES], nr) for k in range(2)]

    @pl.when(g == 0)
    def _():
        ncmp = kcvc_ref.shape[1]
        cidx = lax.broadcasted_iota(jnp.int32, (1, ncmp), 1)
        c_mask = (cidx >= 1) & (cidx * CMP_STRIDE + (CMP_LEN - CMP_STRIDE - 1) <= tq_rows)
        width = ovl_ref.shape[1]
        lane = lax.broadcasted_iota(jnp.int32, (1, width), 1)
        per = keys // SLC_LEN
        blk = (lane // LANES) * per + lane % LANES
        real = (lane % LANES < per) & (blk <= past_len // SLC_LEN)
        psums = []
        for k in range(2):
            p_c = _masked_softmax(_dot_nt(qs[k], kcvc_ref[0, :, k * LANES:(k + 1) * LANES]), c_mask)
            oc_ref[k] = _dot(p_c.astype(BF16), kcvc_ref[0, :, (2 + k) * LANES:(3 + k) * LANES])
            for gl in range(2):
                psum = p_c[gl * nr * t:gl * nr * t + t]
                for r in range(1, nr):
                    psum = psum + p_c[(gl * nr + r) * t:(gl * nr + r + 1) * t]
                psums.append(psum)
        imp = jnp.dot(jnp.concatenate(psums, axis=0), ovl_ref[...], preferred_element_type=F32,
                      precision=lax.Precision.HIGHEST)
        ns = _select_blocks(imp, jnp.where(real, blk, -1), jnp.concatenate([tq] * 4, axis=0), axis=1)
        for k in range(2):
            for gl in range(2):
                for r in range(nr):
                    ns_ref[k, (gl * nr + r) * t:(gl * nr + r + 1) * t, :] = ns[(2 * k + gl) * t:(2 * k + gl + 1) * t]
        m_ref[...] = jnp.full(m_ref.shape, -jnp.inf, F32)
        l_ref[...] = jnp.zeros(l_ref.shape, F32)
        acc_ref[...] = jnp.zeros(acc_ref.shape, F32)

    page = buf_ref.at[slot]
    for k in range(2):
        ns = ns_ref[k, :, pl.ds(pl.multiple_of(g * LANES, LANES), LANES)]
        qa = jnp.concatenate([qs[k], ns.astype(BF16)], axis=1)
        ka = jnp.concatenate([page[k * LANES:(k + 1) * LANES, :].astype(BF16), e_ref[...]], axis=0)
        v = page[(2 + k) * LANES:(3 + k) * LANES, :].astype(BF16)
        m_ref[k], l_ref[k], acc_ref[k] = _flash_update((m_ref[k], l_ref[k], acc_ref[k]), _dot(qa, ka), v,
                                                       v_transposed=True)

    @pl.when(g == ng - 1)
    def _():
        kidx = lax.broadcasted_iota(jnp.int32, (1, t), 1)
        kpos = past_len - (t - 4) + kidx
        new_ok = (kidx >= t - 4) & (kpos <= tq_rows)
        wb = wcache_ref.shape[1]
        wpos = past_len - wb + lax.broadcasted_iota(jnp.int32, (1, wb), 1)
        w_old = (wpos <= tq_rows) & (wpos > tq_rows - WINDOW) & (wpos >= 0)
        w_new = new_ok & (kpos > tq_rows - WINDOW)
        for k in range(2):
            kcol, vcol = slice(k * LANES, (k + 1) * LANES), slice((2 + k) * LANES, (3 + k) * LANES)
            qf = qs[k].astype(F32)
            s_new = jnp.where(new_ok, _dot_nt(qf, slc_ref[:, kcol]), -MASK_BIG)
            _, l, acc = _flash_update((m_ref[k], l_ref[k], acc_ref[k]), s_new, slc_ref[:, vcol])
            o_s = acc / l
            so = jnp.where(w_old, _dot_nt(qs[k], wcache_ref[0, :, kcol].astype(BF16)), -jnp.inf)
            sn = jnp.where(w_new, _dot_nt(qf, win_ref[:, kcol]), -jnp.inf)
            m = jnp.maximum(jnp.max(so, axis=-1, keepdims=True), jnp.max(sn, axis=-1, keepdims=True))
            m = jnp.where(m == -jnp.inf, 0.0, m)
            eo = jnp.where(w_old, jnp.exp2(so - m), 0.0)
            en = jnp.where(w_new, jnp.exp2(sn - m), 0.0)
            den = jnp.maximum(jnp.sum(eo, axis=-1, keepdims=True) + jnp.sum(en, axis=-1, keepdims=True), 1e-30)
            o_w = (_dot(eo.astype(BF16), wcache_ref[0, :, vcol].astype(BF16)) + _dot(en, win_ref[:, vcol])) / den
            oc, os_, ow = _merge_pair(oc_ref[k], nr, t), _merge_pair(o_s, nr, t), _merge_pair(o_w, nr, t)
            gk = g_ref.at[:, k * LANES:(k + 1) * LANES]
            for r in range(nr):
                o = _gate_vec(gk, 0, r, t) * oc[r] + _gate_vec(gk, 1, r, t) * os_[r] + _gate_vec(gk, 2, r, t) * ow[r]
                o_ref[:, (k * nr + r) * LANES:(k * nr + r + 1) * LANES] = o.astype(o_ref.dtype)


def _attend_sample(page_table, q, gates, kcvc, cache_t, slc_new, wcache, win_new, econst_t, ovl, past_len):
    db, n_pages = page_table.shape
    page_rows = cache_t.shape[2]
    econst = econst_t
    groups = n_pages // PAGES_PER_STEP
    keys = PAGES_PER_STEP * page_rows
    nr = N_HEADS // N_KV_HEADS
    t = SAMPLE_ROWS
    rows = 2 * nr * t
    qw = N_HEADS * HEAD_DIM
    cmap = lambda nd: (lambda b, g, pt: (0,) * nd)
    grid_spec = pltpu.PrefetchScalarGridSpec(
        num_scalar_prefetch=1, grid=(db, groups),
        in_specs=[pl.BlockSpec((t, qw), lambda b, g, pt: (b, 0)),
                  pl.BlockSpec((t, 2 * LANES), lambda b, g, pt: (b, 0)),
                  pl.BlockSpec((1,) + kcvc.shape[1:], lambda b, g, pt: (b, 0, 0)),
                  pl.BlockSpec(memory_space=pl.ANY),
                  pl.BlockSpec((t, KV_W), lambda b, g, pt: (b, 0)),
                  pl.BlockSpec((1,) + wcache.shape[1:], lambda b, g, pt: (b, 0, 0)),
                  pl.BlockSpec((t, KV_W), lambda b, g, pt: (b, 0)),
                  pl.BlockSpec(econst.shape, cmap(2), pipeline_mode=pl.Buffered(1)),
                  pl.BlockSpec(ovl.shape, cmap(2), pipeline_mode=pl.Buffered(1))],
        out_specs=pl.BlockSpec((t, qw), lambda b, g, pt: (b, 0)),
        scratch_shapes=[pltpu.VMEM((2, KV_W, keys), F32), pltpu.SemaphoreType.DMA((2,)),
                        pltpu.VMEM((2, rows, ovl.shape[1]), F32), pltpu.VMEM((2, rows, LANES), F32),
                        pltpu.VMEM((2, rows, 1), F32), pltpu.VMEM((2, rows, 1), F32),
                        pltpu.VMEM((2, rows, LANES), F32)])
    return pl.pallas_call(
        functools.partial(_attend_sample_kernel, past_len), grid_spec=grid_spec,
        out_shape=jax.ShapeDtypeStruct((db * t, qw), BF16),
        compiler_params=_cparams(("arbitrary", "arbitrary")), name="attend_sample",
    )(page_table, q, gates, kcvc, cache_t, slc_new, wcache, win_new, econst, ovl)


def _post_kernel(sample, tiles_per_seq, *refs):
    if sample:
        (x_ref, o_ref, ma_ref, sgb_ref, wao_ref, wo_ref, gf_ref, wfi_ref, fcw_ref, wfd_ref, gfin_ref, st_ref,
         y_ref, ast_ref, abuf) = refs
    else:
        (x_ref, o_ref, ma_ref, sgb_ref, wao_ref, wo_ref, gf_ref, wfi_ref, fcw_ref, wfd_ref, gfin_ref,
         y_ref, ast_ref, abuf) = refs
    rows = x_ref.shape[0]
    dff = fcw_ref.shape[1]
    m = ma_ref[...] + sgb_ref[...] * _dot(o_ref[...], wao_ref[...])
    h = x_ref[...] + _dot(m.astype(BF16), wo_ref[...])
    ag = _dot(_rms(h, gf_ref[...]).astype(BF16), wfi_ref[...])
    a = ag[:, 0:dff]
    if sample:
        row = lax.broadcasted_iota(jnp.int32, (rows, 1), 0) % SAMPLE_ROWS
        is_state = (row >= SAMPLE_ROWS - 4 - (CONV_W - 1)) & (row < SAMPLE_ROWS - 4)
        a = jnp.where(is_state, st_ref[...], a)
        abuf[0:SUBLANES, :] = jnp.zeros((SUBLANES, dff), F32)
    else:
        @pl.when(pl.program_id(0) % tiles_per_seq == 0)
        def _():
            abuf[0:SUBLANES, :] = jnp.zeros((SUBLANES, dff), F32)
    ac = _shifted_conv(abuf, a, fcw_ref, rows)
    tail = abuf[rows:rows + SUBLANES, :]
    if sample:
        ast_ref[...] = a
    else:
        ast_ref[0] = tail
        abuf[0:SUBLANES, :] = tail
    hh = h + _dot((jax.nn.silu(ac) * ag[:, dff:2 * dff]).astype(BF16), wfd_ref[...])
    y_ref[...] = _rms(hh, gfin_ref[...])


def _post(x, o, ma, sgb, w, state, seq_rows):
    n, d = x.shape
    sample = state is not None
    tm = n if sample else ROW_TILE
    tiles_per_seq = seq_rows // tm
    n_seq = n // seq_rows
    wao, wo, gf, wfi, fcw, wfd, gfin = w
    dff = fcw.shape[1]
    row = lambda w_: pl.BlockSpec((tm, w_), lambda i: (i, 0))
    args = [x, o, ma, sgb, *w]
    in_specs = [row(d), row(o.shape[1]), row(d), row(d)] + [_const_spec(a.shape) for a in w]
    if sample:
        args.append(state)
        in_specs.append(row(dff))
        ast_shape, ast_spec = jax.ShapeDtypeStruct((n, dff), F32), row(dff)
    else:
        ast_shape = jax.ShapeDtypeStruct((n_seq, SUBLANES, dff), F32)
        ast_spec = pl.BlockSpec((1, SUBLANES, dff), lambda i: (i // tiles_per_seq, 0, 0))
    return pl.pallas_call(
        functools.partial(_post_kernel, sample, tiles_per_seq),
        grid=(n // tm,), in_specs=in_specs, out_specs=[row(d), ast_spec],
        out_shape=[jax.ShapeDtypeStruct((n, d), F32), ast_shape],
        scratch_shapes=[pltpu.VMEM((tm + SUBLANES, dff), F32)],
        compiler_params=_cparams(("arbitrary",)), name="post_sample" if sample else "post_prompt",
    )(*args)


def _pair_head_order():
    nr = N_HEADS // N_KV_HEADS
    return [2 * nr * k + nr * half + r for k in range(2) for r in range(nr) for half in range(2)]


def _rope_tables(pos):
    half = ROT_DIM // 2
    inv_freq = jnp.power(ROPE_THETA, -jnp.arange(half, dtype=F32) * (2.0 / ROT_DIM))
    ang = pos.astype(F32)[:, None] * inv_freq[None, :]
    cos, sin = jnp.cos(ang), jnp.sin(ang)
    n = pos.shape[0]
    one, zero = jnp.ones((n, HEAD_DIM - ROT_DIM), F32), jnp.zeros((n, HEAD_DIM - ROT_DIM), F32)
    zh = jnp.zeros((n, half), F32)
    tabs = (jnp.concatenate([cos, cos, one], 1), jnp.concatenate([zh, sin, zero], 1),
            jnp.concatenate([-sin, zh, zero], 1))
    return tuple(jnp.tile(a, (1, LANES // HEAD_DIM)) for a in tabs)


def _overlap_matrix(n_rows, n_cols, col_block):
    c = np.arange(n_rows)[:, None] - 1
    blk = col_block[None, :]
    cs, ss = c * CMP_STRIDE, blk * SLC_LEN
    return ((c >= 0) & (blk >= 0) & (cs < ss + SLC_LEN) & (cs + CMP_LEN > ss)).astype(np.float32)


def _block_onehot(n_keys, transposed=False):
    e = (np.arange(n_keys)[:, None] // SLC_LEN == np.arange(LANES)[None, :]).astype(np.float32)
    return jnp.asarray(-MASK_BIG * (e.T if transposed else e), dtype=BF16)


def _split_w_in(w_in, d):
    conv_dim = d
    q_dim = N_HEADS * HEAD_DIM
    kv_dim = N_BRANCH * KV_W
    o0 = 3 * conv_dim
    wc = w_in[:, 0:o0]
    wq = w_in[:, o0:o0 + q_dim].reshape(d, N_HEADS, HEAD_DIM)[:, np.array(_pair_head_order())].reshape(d, q_dim)
    o1 = o0 + q_dim
    wkv = w_in[:, o1:o1 + kv_dim]
    o2 = o1 + kv_dim
    wl = w_in[:, o2:o2 + N_BRANCH * N_HEADS].reshape(d, N_HEADS, N_BRANCH)
    wl = wl[:, np.array(_pair_head_order())].reshape(d, 2, N_HEADS // 2, N_BRANCH).transpose(0, 1, 3, 2)
    wl = jnp.pad(wl.reshape(d, 2, N_BRANCH * N_HEADS // 2), ((0, 0), (0, 0), (0, LANES - N_BRANCH * N_HEADS // 2)))
    wl = wl.reshape(d, 2 * LANES)
    o3 = o2 + N_BRANCH * N_HEADS
    wgab = w_in[:, o3:o3 + 2 * d]
    return [a.astype(BF16) for a in (wc, wq, wkv, wl, wgab)]


def kernel(x_prompt, x_sample, cache_cmp_kv, cache_slc_kv, cache_win_kv, state_conv_mix, state_conv_ffn,
           page_table, norm_mix_g, w_in, conv_mix_w, w_conv_out, cmp_pe, cmp_w1, cmp_b1, cmp_w2, cmp_b2,
           w_attn_out, w_out, norm_ffn_g, w_ff_in, ff_conv_w, w_ff_down, norm_final_g):
    depth = w_in.shape[0]
    assert depth == 1, "single-layer step"
    b, s, d = x_prompt.shape
    db, t, _ = x_sample.shape
    page_rows = cache_cmp_kv.shape[2]
    past_len = page_table.shape[1] * page_rows
    assert t == 4 and s % K_TILE == 0 and s % CMP_ROWS == 0 and s >= WINDOW + Q_TILE
    assert past_len % (PAGES_PER_STEP * page_rows) == 0 and cache_win_kv.shape[2] == WINDOW
    assert N_SEL <= s // SLC_LEN <= LANES, "prompt selection blocks fit one lane block"
    dff = ff_conv_w.shape[2]
    l = 0

    order = np.array(_pair_head_order())
    front_w = _split_w_in(w_in[l], d) + [w_conv_out[l].astype(BF16)]
    post_w = [w_attn_out[l].reshape(N_HEADS, HEAD_DIM, d)[order].reshape(N_HEADS * HEAD_DIM, d).astype(BF16),
              w_out[l].astype(BF16), norm_ffn_g[l].reshape(1, d), w_ff_in[l].astype(BF16), ff_conv_w[l],
              w_ff_down[l].astype(BF16), norm_final_g.reshape(1, d)]
    g_mix = norm_mix_g[l].reshape(1, d)
    cw = _compress_weights(cmp_pe[l], cmp_w1[l], cmp_b1[l], cmp_w2[l], cmp_b2[l])
    econst = _block_onehot(s)
    econst_t = _block_onehot(PAGES_PER_STEP * page_rows, transposed=True)

    xp = x_prompt.reshape(b * s, d)
    ma, sgb, qt, cmp_p, slc_p, win_p, kb, vt, gt, pst = _front(
        xp, g_mix, _rope_tables(jnp.arange(s, dtype=jnp.int32)), front_w, conv_mix_w[l], None, s)
    kcvc, vct = _compress_prompt(cmp_p, cw, b)
    ovlt_p = jnp.asarray(_overlap_matrix(s // CMP_STRIDE, LANES, np.arange(LANES)).T)
    o = _attend_prompt(qt, gt, kcvc, vct, kb, vt, econst, ovlt_p, b)
    y_p, ast = _post(xp, o, ma, sgb, post_w, None, s)
    kv_shape = (2, N_KV_HEADS, HEAD_DIM)
    wb_p = min(WINDOW, s)
    out_prompt = (
        y_p.reshape(b, s, d),
        cmp_p.reshape((1, b, s) + kv_shape), slc_p.reshape((1, b, s) + kv_shape),
        win_p.reshape((b, s) + kv_shape)[None, :, s - wb_p:],
        pst[None, :, SUBLANES - (CONV_W - 1):], ast[None, :, SUBLANES - (CONV_W - 1):])

    r8 = SAMPLE_ROWS
    pad_rows = lambda a, lo: jnp.pad(a, ((0, 0), (lo, r8 - lo - a.shape[1]), (0, 0))).reshape(db * r8, a.shape[2])
    xs = pad_rows(x_sample, r8 - t)
    st_mix = pad_rows(state_conv_mix[l], r8 - t - (CONV_W - 1))
    st_ffn = pad_rows(state_conv_ffn[l], r8 - t - (CONV_W - 1))
    pos_s = past_len - (r8 - t) + jnp.arange(r8, dtype=jnp.int32)
    tabs_s = tuple(jnp.tile(a, (db, 1)) for a in _rope_tables(pos_s))
    ma_s, sgb_s, q_s, cmp_s, slc_s, win_s, gates_s, p_s = _front(
        xs, g_mix, tabs_s, front_w, conv_mix_w[l], st_mix, db * r8)
    kcvc_s = _compress_sample(page_table, _feature_major_pages(cache_cmp_kv[l]), cw)
    keys_per_step = PAGES_PER_STEP * page_rows
    per = keys_per_step // SLC_LEN
    n_slots = past_len // keys_per_step + 1
    lane = np.arange(n_slots * LANES)
    col_block = np.where(lane % LANES < per, (lane // LANES) * per + lane % LANES, -1)
    col_block = np.where(col_block <= past_len // SLC_LEN, col_block, -1)
    ovl_s = jnp.asarray(_overlap_matrix(past_len // CMP_STRIDE, n_slots * LANES, col_block))
    wcache = cache_win_kv[l].reshape(db, WINDOW, KV_W)
    o_s = _attend_sample(page_table, q_s, gates_s, kcvc_s, _feature_major_pages(cache_slc_kv[l]), slc_s, wcache,
                         win_s, econst_t, ovl_s, past_len)
    y_s, a_s = _post(xs, o_s, ma_s, sgb_s, post_w, st_ffn, db * r8)
    tok = lambda a: a.reshape(db, r8, -1)[:, r8 - t:]
    win_new = jnp.concatenate([wcache, tok(win_s)], axis=1)[:, -WINDOW:]
    out_sample = (
        tok(y_s),
        tok(cmp_s).reshape((1, db, t) + kv_shape), tok(slc_s).reshape((1, db, t) + kv_shape),
        win_new.reshape((1, db, WINDOW) + kv_shape),
        p_s.reshape(db, r8, d)[None, :, r8 - (CONV_W - 1):], a_s.reshape(db, r8, dff)[None, :, r8 - (CONV_W - 1):])

    return (out_prompt[0], out_sample[0], out_prompt[1], out_prompt[2], out_prompt[3], out_prompt[4],
            out_prompt[5], out_sample[1], out_sample[2], out_sample[3], out_sample[4], out_sample[5])
```

```python
import functools

import numpy as np
import jax
import jax.numpy as jnp
from jax import lax
from jax.experimental import pallas as pl
from jax.experimental.pallas import tpu as pltpu

F32 = jnp.float32
BF16 = jnp.bfloat16

N_HEADS = 16
HEAD_DIM = 64
N_KV_HEADS = 4
N_BRANCH = 3
ROT_DIM = 16
ROPE_THETA = 500000.0
CMP_LEN = 32
CMP_STRIDE = 16
SLC_LEN = 64
N_SEL = 16
N_LOCAL = 2
WINDOW = 512
CONV_W = 3
NORM_EPS = 1e-6
FORCED_SCORE = 1e9

LANES = 128
SUBLANES = 8
KV_W = 2 * N_KV_HEADS * HEAD_DIM
MASK_BIG = 2.0 ** 100
LOG2_E = 1.4426950408889634
VMEM_LIMIT = 56 * 1024 * 1024

ROW_TILE = 256
CMP_ROWS = 4096
Q_TILE = 128
K_TILE = 512
SAMPLE_ROWS = 8
PAGES_PER_STEP = 32


def _cparams(sem):
    return pltpu.CompilerParams(dimension_semantics=sem, vmem_limit_bytes=VMEM_LIMIT)


def _const_spec(shape):
    nd = len(shape)
    return pl.BlockSpec(shape, lambda *_: (0,) * nd, pipeline_mode=pl.Buffered(1))


def _dot(a, b):
    return jnp.dot(a, b, preferred_element_type=F32)


def _dot_nt(a, b):
    return lax.dot_general(a, b, (((1,), (1,)), ((), ())), preferred_element_type=F32)


def _rms(x, g):
    y = x * lax.rsqrt(jnp.mean(x * x, axis=-1, keepdims=True) + NORM_EPS)
    return y * g


def _rope_block(x, cos, sa, sb):
    return x * cos + pltpu.roll(x, 8, 1) * sa + pltpu.roll(x, LANES - 8, 1) * sb


def _shifted_conv(buf, p, w_ref, rows):
    buf[SUBLANES:SUBLANES + rows, :] = p
    p1 = buf[SUBLANES - 1:SUBLANES - 1 + rows, :]
    p2 = buf[SUBLANES - 2:SUBLANES - 2 + rows, :]
    return p2 * w_ref[0:1, :] + p1 * w_ref[1:2, :] + p * w_ref[2:3, :]


def _front_kernel(sample, tiles_per_seq, *refs):
    (x_ref, g_ref, cos_ref, sa_ref, sb_ref, wc_ref, wq_ref, wkv_ref, wgl_ref, wgab_ref, cw_ref, wco_ref) = refs[:12]
    if sample:
        st_ref, ma_ref, sgb_ref, q_ref, cmp_ref, slc_ref, win_ref, gl_ref, pst_ref, pbuf = refs[12:]
    else:
        ma_ref, sgb_ref, qt_ref, cmp_ref, slc_ref, win_ref, kb_ref, vt_ref, glt_ref, pst_ref, pbuf = refs[12:]
    rows, d = x_ref.shape
    u = _rms(x_ref[...], g_ref[...]).astype(BF16)

    zc = _dot(u, wc_ref[...])
    p = zc[:, d:2 * d] * zc[:, 0:d]
    if sample:
        row = lax.broadcasted_iota(jnp.int32, (rows, 1), 0) % SAMPLE_ROWS
        is_state = (row >= SAMPLE_ROWS - 4 - (CONV_W - 1)) & (row < SAMPLE_ROWS - 4)
        p = jnp.where(is_state, st_ref[...], p)
        pbuf[0:SUBLANES, :] = jnp.zeros((SUBLANES, d), F32)
    else:
        @pl.when(pl.program_id(0) % tiles_per_seq == 0)
        def _():
            pbuf[0:SUBLANES, :] = jnp.zeros((SUBLANES, d), F32)
    yc = _shifted_conv(pbuf, p, cw_ref, rows)
    tail = pbuf[rows:rows + SUBLANES, :]
    if sample:
        pst_ref[...] = p
    else:
        pst_ref[0] = tail
        pbuf[0:SUBLANES, :] = tail
    ya = _dot((zc[:, 2 * d:3 * d] * yc).astype(BF16), wco_ref[...])

    zg = _dot(u, wgab_ref[...])
    ma_ref[...] = jax.nn.sigmoid(zg[:, 0:d]) * ya
    sgb_ref[...] = jax.nn.sigmoid(zg[:, d:2 * d])

    cos, sa, sb = cos_ref[...], sa_ref[...], sb_ref[...]
    zq = _dot(u, wq_ref[...])
    scale = HEAD_DIM ** -0.5 * LOG2_E
    for j in range(zq.shape[1] // LANES):
        blk = _rope_block(zq[:, j * LANES:(j + 1) * LANES], cos, sa, sb) * scale
        if sample:
            q_ref[:, j * LANES:(j + 1) * LANES] = blk.astype(BF16)
        else:
            qt_ref[j * LANES:(j + 1) * LANES, :] = blk.T.astype(BF16)

    zkv = _dot(u, wkv_ref[...])
    per_branch = KV_W // LANES
    half = per_branch // 2
    outs = (cmp_ref, slc_ref, win_ref)
    for j in range(zkv.shape[1] // LANES):
        br, jj = divmod(j, per_branch)
        blk = zkv[:, j * LANES:(j + 1) * LANES]
        if jj < half:
            blk = _rope_block(blk, cos, sa, sb)
        outs[br][:, jj * LANES:(jj + 1) * LANES] = blk
        if br > 0 and not sample:
            c = (br - 1) * half + jj % half
            if jj < half:
                kb_ref[:, c * LANES:(c + 1) * LANES] = blk.astype(BF16)
            else:
                vt_ref[c * LANES:(c + 1) * LANES, :] = blk.T.astype(BF16)

    gl = jax.nn.sigmoid(_dot(u, wgl_ref[...]))
    if sample:
        gl_ref[...] = gl
    else:
        for j in range(gl.shape[1] // LANES):
            glt_ref[j * LANES:(j + 1) * LANES, :] = gl[:, j * LANES:(j + 1) * LANES].T


def _front(x, g, tabs, w, conv_w, state, seq_rows):
    n, d = x.shape
    sample = state is not None
    tm = n if sample else ROW_TILE
    tiles_per_seq = seq_rows // tm
    nt = n // tm
    n_seq = n // seq_rows
    qw = N_HEADS * HEAD_DIM
    row = lambda w_: pl.BlockSpec((tm, w_), lambda i: (i, 0))
    col = lambda h_: pl.BlockSpec((h_, tm), lambda i: (0, i))
    tab = pl.BlockSpec((tm, LANES), lambda i: (i % tiles_per_seq, 0))
    args = [x, g, *tabs, *w[:5], conv_w, w[5]]
    in_specs = [row(d), _const_spec((1, d)), tab, tab, tab] + [_const_spec(a.shape) for a in w[:5]] + [
        _const_spec(conv_w.shape), _const_spec(w[5].shape)]
    f32_rows = lambda w_: jax.ShapeDtypeStruct((n, w_), F32)
    if sample:
        args.append(state)
        in_specs.append(row(d))
        out_shape = [f32_rows(d), f32_rows(d), jax.ShapeDtypeStruct((n, qw), BF16), f32_rows(KV_W), f32_rows(KV_W),
                     f32_rows(KV_W), f32_rows(2 * LANES), f32_rows(d)]
        out_specs = [row(d), row(d), row(qw), row(KV_W), row(KV_W), row(KV_W), row(2 * LANES), row(d)]
    else:
        out_shape = [f32_rows(d), f32_rows(d), jax.ShapeDtypeStruct((qw, n), BF16), f32_rows(KV_W), f32_rows(KV_W),
                     f32_rows(KV_W), jax.ShapeDtypeStruct((n, KV_W), BF16), jax.ShapeDtypeStruct((KV_W, n), BF16),
                     jax.ShapeDtypeStruct((2 * LANES, n), F32), jax.ShapeDtypeStruct((n_seq, SUBLANES, d), F32)]
        out_specs = [row(d), row(d), col(qw), row(KV_W), row(KV_W), row(KV_W), row(KV_W), col(KV_W),
                     col(2 * LANES), pl.BlockSpec((1, SUBLANES, d), lambda i: (i // tiles_per_seq, 0, 0))]
    return pl.pallas_call(
        functools.partial(_front_kernel, sample, tiles_per_seq),
        grid=(nt,), in_specs=in_specs, out_specs=out_specs, out_shape=out_shape,
        scratch_shapes=[pltpu.VMEM((tm + SUBLANES, d), F32)],
        compiler_params=_cparams(("arbitrary",)), name="front_sample" if sample else "front_prompt",
    )(*args)


def _cmp_bias_kernel(pe_ref, w1_ref, b1_ref, o_ref):
    for kv in range(2):
        a = _dot(pe_ref[kv], w1_ref[kv])
        o_ref[kv] = jnp.broadcast_to(b1_ref[kv] + a[0:1, 0:LANES] + a[1:2, LANES:2 * LANES], (SUBLANES, LANES))


def _compress_tile(lhs_of, w1_ref, hb_ref, w2_ref, b2_ref, carry_ref, sh_ref, out_ref, vt_ref=None):
    nck = out_ref.shape[0]
    for j in range(KV_W // LANES):
        kv = j // 2
        a = _dot(lhs_of(j), w1_ref[kv])
        a0 = a[:, 0:LANES]
        sh_ref[SUBLANES:SUBLANES + nck, :] = a0
        sh_ref[0:SUBLANES, :] = carry_ref[j]
        hid = sh_ref[SUBLANES - 1:SUBLANES - 1 + nck, :] + a[:, LANES:2 * LANES] + hb_ref[kv][0:1, :]
        carry_ref[j] = sh_ref[nck:nck + SUBLANES, :]
        o = _dot(jax.nn.silu(hid).astype(BF16), w2_ref[kv]) + b2_ref[kv]
        out_ref[:, j * LANES:(j + 1) * LANES] = o.astype(out_ref.dtype)
        if vt_ref is not None and kv == 1:
            vt_ref[(j - 2) * LANES:(j - 1) * LANES, :] = o.T.astype(vt_ref.dtype)


def _compress_prompt_kernel(r0, r1, r2, r3, w1_ref, hb_ref, w2_ref, b2_ref, out_ref, vt_ref, carry_ref, sh_ref):
    @pl.when(pl.program_id(1) == 0)
    def _():
        carry_ref[...] = jnp.zeros(carry_ref.shape, F32)
    planes = (r0, r1, r2, r3)
    nck = out_ref.shape[1]

    def lhs_of(j):
        return jnp.concatenate([planes[j][pl.ds(s, nck, stride=CMP_STRIDE), :].astype(BF16)
                                for s in range(CMP_STRIDE)], axis=1)

    _compress_tile(lhs_of, w1_ref, hb_ref, w2_ref, b2_ref, carry_ref, sh_ref, out_ref.at[0], vt_ref.at[0])


def _page_copies(pt_ref, cache_ref, buf_ref, sem_ref, slot, b, grp, n_pages):
    page_rows = cache_ref.shape[2]
    return [pltpu.make_async_copy(cache_ref.at[pt_ref[b, grp * n_pages + p]],
                                  buf_ref.at[slot, :, pl.ds(p * page_rows, page_rows)], sem_ref.at[slot])
            for p in range(n_pages)]


def _gather_step(pt_ref, cache_ref, buf_ref, sem_ref, n_pages):
    b, g = pl.program_id(0), pl.program_id(1)
    nb, ng = pl.num_programs(0), pl.num_programs(1)
    step = b * ng + g
    slot = step % 2

    @pl.when(step == 0)
    def _():
        for c in _page_copies(pt_ref, cache_ref, buf_ref, sem_ref, 0, 0, 0, n_pages):
            c.start()

    @pl.when(step + 1 < nb * ng)
    def _():
        nxt = step + 1
        for c in _page_copies(pt_ref, cache_ref, buf_ref, sem_ref, 1 - slot, nxt // ng, nxt % ng, n_pages):
            c.start()

    for c in _page_copies(pt_ref, cache_ref, buf_ref, sem_ref, slot, b, g, n_pages):
        c.wait()
    return slot


def _compress_sample_kernel(pt_ref, cache_ref, perm_ref, w1_ref, hb_ref, w2_ref, b2_ref, out_ref,
                            buf_ref, sem_ref, carry_ref, sh_ref):
    n_pages = buf_ref.shape[2] // cache_ref.shape[2]
    slot = _gather_step(pt_ref, cache_ref, buf_ref, sem_ref, n_pages)

    @pl.when(pl.program_id(1) == 0)
    def _():
        carry_ref[...] = jnp.zeros(carry_ref.shape, F32)

    span = perm_ref.shape[0]
    ck = span // CMP_STRIDE

    def permuted(half, g):
        x = buf_ref[slot, half * 2 * LANES:(half + 1) * 2 * LANES, g * span:(g + 1) * span]
        return _dot_nt(perm_ref[...], x.astype(BF16)).astype(BF16)

    tiles = [[permuted(half, g) for g in range(buf_ref.shape[2] // span)] for half in range(2)]

    def lhs_of(j):
        half, jj = divmod(j, 2)
        return jnp.concatenate(
            [jnp.concatenate([tl[s * ck:(s + 1) * ck, jj * LANES:(jj + 1) * LANES] for tl in tiles[half]], axis=0)
             for s in range(CMP_STRIDE)], axis=1)

    _compress_tile(lhs_of, w1_ref, hb_ref, w2_ref, b2_ref, carry_ref, sh_ref, out_ref.at[0])


def _compress_weights(cmp_pe, cmp_w1, cmp_b1, cmp_w2, cmp_b2):
    r = CMP_LEN // CMP_STRIDE
    eye2 = jnp.eye(2, dtype=F32)
    w1 = cmp_w1.reshape(2, r, CMP_STRIDE, HEAD_DIM, HEAD_DIM)
    w1bd = jnp.einsum("krsde,hg->kshdrge", w1, eye2).reshape(2, CMP_STRIDE * LANES, r * LANES).astype(BF16)
    w2bd = jnp.einsum("kde,hg->khdge", cmp_w2, eye2).reshape(2, LANES, LANES).astype(BF16)
    pe = cmp_pe.reshape(2, r, CMP_STRIDE, 1, HEAD_DIM)
    pe = jnp.broadcast_to(pe, (2, r, CMP_STRIDE, 2, HEAD_DIM)).reshape(2, r, CMP_STRIDE * LANES)
    pe = jnp.pad(pe, ((0, 0), (0, SUBLANES - r), (0, 0))).astype(BF16)
    b1 = jnp.tile(cmp_b1, (1, 2)).reshape(2, 1, LANES)
    b2 = jnp.tile(cmp_b2, (1, 2)).reshape(2, 1, LANES)
    hb = pl.pallas_call(
        _cmp_bias_kernel, out_shape=jax.ShapeDtypeStruct((2, SUBLANES, LANES), F32), name="compress_bias",
    )(pe, w1bd, b1)
    return w1bd, hb, w2bd, b2


def _compress_prompt(cmp_rows, cw, n_seq):
    n = cmp_rows.shape[0]
    s = n // n_seq
    nck = CMP_ROWS // CMP_STRIDE
    tiles = s // CMP_ROWS
    w1bd, hb, w2bd, b2 = cw
    return pl.pallas_call(
        _compress_prompt_kernel,
        grid=(n_seq, tiles),
        in_specs=[pl.BlockSpec((CMP_ROWS, LANES), functools.partial(lambda j, b, t: (b * tiles + t, j), j))
                  for j in range(KV_W // LANES)] + [
                  _const_spec(w1bd.shape), _const_spec(hb.shape), _const_spec(w2bd.shape), _const_spec(b2.shape)],
        out_specs=[pl.BlockSpec((1, nck, KV_W), lambda b, t: (b, t, 0)),
                   pl.BlockSpec((1, KV_W // 2, nck), lambda b, t: (b, 0, t))],
        out_shape=[jax.ShapeDtypeStruct((n_seq, s // CMP_STRIDE, KV_W), BF16),
                   jax.ShapeDtypeStruct((n_seq, KV_W // 2, s // CMP_STRIDE), BF16)],
        scratch_shapes=[pltpu.VMEM((KV_W // LANES, SUBLANES, LANES), F32),
                        pltpu.VMEM((nck + SUBLANES, LANES), F32)],
        compiler_params=_cparams(("arbitrary", "arbitrary")), name="compress_prompt",
    )(cmp_rows, cmp_rows, cmp_rows, cmp_rows, w1bd, hb, w2bd, b2)


def _feature_major_pages(cache):
    return cache.transpose(0, 2, 3, 4, 1).reshape(cache.shape[0], KV_W, cache.shape[1])


def _compress_sample(page_table, cache_t, cw):
    db, n_pages = page_table.shape
    page_rows = cache_t.shape[2]
    pages_per = CMP_ROWS // page_rows
    groups = n_pages // pages_per
    nck = CMP_ROWS // CMP_STRIDE
    w1bd, hb, w2bd, b2 = cw
    span = 2 * page_rows
    ck = span // CMP_STRIDE
    row = np.arange(span)
    s_of, c_of = row // ck, row % ck
    per_page = page_rows // CMP_STRIDE
    tok = (c_of // per_page) * page_rows + CMP_STRIDE * (c_of % per_page) + s_of
    perm = jnp.asarray(tok[:, None] == np.arange(span)[None, :], dtype=BF16)
    cspec = lambda a: pl.BlockSpec(a.shape, lambda b, g, pt: (0,) * a.ndim, pipeline_mode=pl.Buffered(1))
    grid_spec = pltpu.PrefetchScalarGridSpec(
        num_scalar_prefetch=1, grid=(db, groups),
        in_specs=[pl.BlockSpec(memory_space=pl.ANY), cspec(perm), cspec(w1bd), cspec(hb), cspec(w2bd), cspec(b2)],
        out_specs=pl.BlockSpec((1, nck, KV_W), lambda b, g, pt: (b, g, 0)),
        scratch_shapes=[pltpu.VMEM((2, KV_W, CMP_ROWS), F32), pltpu.SemaphoreType.DMA((2,)),
                        pltpu.VMEM((KV_W // LANES, SUBLANES, LANES), F32),
                        pltpu.VMEM((nck + SUBLANES, LANES), F32)])
    return pl.pallas_call(
        _compress_sample_kernel, grid_spec=grid_spec,
        out_shape=jax.ShapeDtypeStruct((db, n_pages * page_rows // CMP_STRIDE, KV_W), BF16),
        compiler_params=_cparams(("arbitrary", "arbitrary")), name="compress_sample",
    )(page_table, cache_t, perm, w1bd, hb, w2bd, b2)


def _masked_softmax(s, mask):
    s = jnp.where(mask, s, -jnp.inf)
    m = jnp.max(s, axis=-1, keepdims=True)
    m = jnp.where(m == -jnp.inf, 0.0, m)
    e = jnp.where(mask, jnp.exp2(s - m), 0.0)
    return e / jnp.maximum(jnp.sum(e, axis=-1, keepdims=True), 1e-30)


def _select_blocks(imp, blk, t_pos, axis):
    cur = t_pos // SLC_LEN
    valid = blk <= cur
    forced = (blk == 0) | (valid & (blk > cur - N_LOCAL))
    cand = jnp.where(forced, -jnp.inf, jnp.where(valid, imp, -FORCED_SCORE))
    cand = jnp.where(blk >= 0, cand, -jnp.inf)
    length = imp.shape[axis]
    idx = lax.broadcasted_iota(jnp.int32, imp.shape, axis).astype(F32)

    def pick(_, carry):
        cand, notsel = carry
        m = jnp.max(cand, axis=axis, keepdims=True)
        first = jnp.min(jnp.where(cand == m, idx, float(length)), axis=axis, keepdims=True)
        hit = idx == first
        return jnp.where(hit, -jnp.inf, cand), jnp.where(hit, 0.0, notsel)

    _, notsel = lax.fori_loop(0, N_SEL - (N_LOCAL + 1), pick, (cand, jnp.where(forced, 0.0, 1.0)), unroll=True)
    return notsel


def _flash_update(carry, s, v, v_transposed=False):
    m, l, acc = carry
    m_new = jnp.maximum(m, jnp.max(s, axis=-1, keepdims=True))
    alpha = jnp.exp2(m - m_new)
    p = jnp.exp2(s - m_new)
    l = alpha * l + jnp.sum(p, axis=-1, keepdims=True)
    pv = _dot_nt(p.astype(v.dtype), v) if v_transposed else _dot(p.astype(v.dtype), v)
    return m_new, l, alpha * acc + pv


def _half_mask(shape):
    return lax.broadcasted_iota(jnp.int32, shape, len(shape) - 1) < HEAD_DIM


def _stack_heads(q_blk, nr):
    first = _half_mask((q_blk.shape[0], LANES))
    zero = jnp.zeros((), q_blk.dtype)
    parts = []
    for gl in range(2):
        for r in range(nr):
            blk = q_blk[:, r * LANES:(r + 1) * LANES]
            parts.append(jnp.where(first if gl == 0 else ~first, blk, zero))
    return jnp.concatenate(parts, axis=0)


def _merge_pair(o, nr, t):
    first = _half_mask((t, LANES))
    return [jnp.where(first, o[r * t:(r + 1) * t], o[(nr + r) * t:(nr + r + 1) * t]) for r in range(nr)]


def _gate_vec(g_ref, br, r, t):
    c = br * 2 * (N_HEADS // N_KV_HEADS) + r * 2
    return jnp.where(_half_mask((t, LANES)), g_ref[:, c:c + 1], g_ref[:, c + 1:c + 2])


def _biased_exp0(s, bias):
    s = s + jnp.concatenate([bias] * (s.shape[1] // bias.shape[1]), axis=1)
    m = jnp.max(s, axis=0, keepdims=True)
    m = jnp.where(m < -0.5 * MASK_BIG, 0.0, m)
    e = jnp.exp2(s - m)
    return e, 1.0 / jnp.maximum(jnp.sum(e, axis=0, keepdims=True), 1e-30)


def _pv_own_head(vt, p):
    half = p.shape[1] // 2
    return jnp.concatenate([_dot(vt[0:HEAD_DIM, :], p[:, 0:half]), _dot(vt[HEAD_DIM:2 * HEAD_DIM, :], p[:, half:])],
                           axis=1)


def _flash_update0(carry, s, vt):
    m, l, acc = carry
    m_new = jnp.maximum(m, jnp.max(s, axis=0, keepdims=True))
    alpha = jnp.exp2(m - m_new)
    p = jnp.exp2(s - m_new)
    l = alpha * l + jnp.sum(p, axis=0, keepdims=True)
    return m_new, l, alpha * acc + _pv_own_head(vt, p.astype(vt.dtype))


def _attend_prompt_kernel(qt_ref, gt_ref, kc_ref, vct_ref, ks_ref, vst_ref, kw_ref, vwt_ref, e_ref, ovlt_ref,
                          cbias_ref, wbias_ref, fbias_ref,
                          o_ref, qa_ref, sa_ref, sb_ref, m_ref, l_ref, acc_ref, ow_ref):
    i = pl.program_id(2)
    nr = N_HEADS // N_KV_HEADS
    t = Q_TILE
    rows = 2 * nr * t
    t0 = i * t
    tq = t0 + lax.broadcasted_iota(jnp.int32, (1, t), 1)
    first =lax.broadcasted_iota(jnp.int32, (LANES, t), 0) < HEAD_DIM
    zero = jnp.zeros((), BF16)
    qt = jnp.concatenate([jnp.where(first if gl == 0 else ~first, qt_ref[r * LANES:(r + 1) * LANES, :], zero)
                          for gl in range(2) for r in range(nr)], axis=1)

    ncmp = kc_ref.shape[1]
    c_off = pl.multiple_of(ncmp - i * (t // CMP_STRIDE), t // CMP_STRIDE)
    s_c = _dot(kc_ref[0], qt)
    head = jnp.where(lax.broadcasted_iota(jnp.int32, (SUBLANES, 1), 0) >= 1, s_c[0:SUBLANES], -MASK_BIG)
    e_c, inv_c = _biased_exp0(jnp.concatenate([head, s_c[SUBLANES:]], axis=0), cbias_ref[pl.ds(c_off, ncmp), :])
    p_c = e_c * inv_c
    o_c = _pv_own_head(vct_ref[0], p_c.astype(BF16))

    psum = []
    for gl in range(2):
        acc = p_c[:, gl * nr * t:gl * nr * t + t]
        for r in range(1, nr):
            acc = acc + p_c[:, (gl * nr + r) * t:(gl * nr + r + 1) * t]
        psum.append(acc)
    imp = jnp.dot(ovlt_ref[...], jnp.concatenate(psum, axis=1), preferred_element_type=F32,
                  precision=lax.Precision.HIGHEST)
    wlen = WINDOW + t
    ws = pl.multiple_of(jnp.maximum(t0 - WINDOW, 0), t)
    anchor = (imp[0:1, 0:1] * 0.0).astype(BF16)
    e_w, inv_w = _biased_exp0(_dot(kw_ref[pl.ds(ws, wlen), :] + anchor, qt), wbias_ref[jnp.minimum(i, WINDOW // t)])
    ow_ref[...] = _pv_own_head(vwt_ref[:, pl.ds(ws, wlen)], e_w.astype(BF16)) * inv_w

    nslc = ovlt_ref.shape[0]
    blk = lax.broadcasted_iota(jnp.int32, (nslc, 1), 0)
    ns = _select_blocks(imp, blk, jnp.concatenate([tq, tq], axis=1), axis=0).astype(BF16)
    qa = jnp.concatenate([qt, jnp.concatenate([ns[:, 0:t]] * nr + [ns[:, t:2 * t]] * nr, axis=1)], axis=0)

    qa_ref[...] = qa

    def produce(s_ref, j):
        off = pl.multiple_of(j * K_TILE, K_TILE)
        ka = jnp.concatenate([ks_ref[pl.ds(off, K_TILE), :], e_ref[pl.ds(off, K_TILE), :]], axis=1)
        s_ref[...] = _dot(ka, qa_ref[...])

    def consume(s_ref, j, causal):
        off = pl.multiple_of(j * K_TILE, K_TILE)
        s = s_ref[...]
        if causal:
            fb = fbias_ref[(t0 - off) // t]
            s = s + jnp.concatenate([fb] * (rows // t), axis=1)
        m_ref[...], l_ref[...], acc_ref[...] = _flash_update0(
            (m_ref[...], l_ref[...], acc_ref[...]), s, vst_ref[:, pl.ds(off, K_TILE)])

    m_ref[...] = jnp.full(m_ref.shape, -jnp.inf, F32)
    l_ref[...] = jnp.zeros(l_ref.shape, F32)
    acc_ref[...] = jnp.zeros(acc_ref.shape, F32)
    last = (t0 + t - 1) // K_TILE
    produce(sa_ref, 0)

    def pair(jj, _):
        j = 2 * jj
        produce(sb_ref, j + 1)
        consume(sa_ref, j, False)
        produce(sa_ref, j + 2)
        consume(sb_ref, j + 1, False)
        return 0

    lax.fori_loop(0, last // 2, pair, 0)

    @pl.when(last % 2 == 1)
    def _():
        produce(sb_ref, last)
        consume(sa_ref, last - 1, False)
        consume(sb_ref, last, True)

    @pl.when(last % 2 == 0)
    def _():
        consume(sa_ref, last, True)

    o_s = acc_ref[...] * (1.0 / l_ref[...])

    o_w = ow_ref[...]

    for r in range(nr):
        c0, c1 = r * t, (nr + r) * t
        o = None
        for br, ob in enumerate((o_c, o_s, o_w)):
            c = br * 2 * nr + r * 2
            gate = jnp.where(first, gt_ref[c:c + 1, :], gt_ref[c + 1:c + 2, :])
            term = gate * jnp.concatenate([ob[:, c0:c0 + t], ob[:, c1:c1 + t]], axis=0)
            o = term if o is None else o + term
        o_ref[:, r * LANES:(r + 1) * LANES] = o.T.astype(o_ref.dtype)


def _mask_bias(visible):
    return jnp.asarray(np.where(visible, 0.0, -MASK_BIG), dtype=F32)


def _attend_prompt(qt, gt, kcvc, vct, kb, vt, econst, ovlt, n_seq):
    t = Q_TILE
    tcol = np.arange(t)[None, :]
    ncmp_ = kcvc.shape[1]
    x = np.arange(-ncmp_, ncmp_)[:, None]
    cbias = _mask_bias(x * CMP_STRIDE + (CMP_LEN - CMP_STRIDE - 1) <= tcol)
    r = np.arange(WINDOW + t)[None, :, None]
    d = (np.arange(WINDOW // t + 1) * t)[:, None, None]
    wbias = _mask_bias((r <= d + tcol[None]) & (r > d + tcol[None] - WINDOW))
    r = np.arange(K_TILE)[None, :, None]
    d = (np.arange(K_TILE // t) * t)[:, None, None]
    fbias = _mask_bias(r <= d + tcol[None])
    n = qt.shape[1]
    s = n // n_seq
    nq = s // Q_TILE
    nr = N_HEADS // N_KV_HEADS
    ncmp = kcvc.shape[1]
    qw = nr * LANES
    rows = 2 * nr * Q_TILE
    rows_blk = lambda c0: pl.BlockSpec((s, LANES), lambda b, k, i: (b, c0 + k))
    cols_blk = lambda c0: pl.BlockSpec((LANES, s), lambda b, k, i: (c0 + k, b))
    return pl.pallas_call(
        _attend_prompt_kernel,
        grid=(n_seq, 2, nq),
        in_specs=[pl.BlockSpec((qw, Q_TILE), lambda b, k, i: (k, b * nq + i)),
                  pl.BlockSpec((LANES, Q_TILE), lambda b, k, i: (k, b * nq + i)),
                  pl.BlockSpec((1, ncmp, LANES), lambda b, k, i: (b, 0, k)),
                  pl.BlockSpec((1, LANES, ncmp), lambda b, k, i: (b, k, 0)),
                  rows_blk(0), cols_blk(0), rows_blk(2), cols_blk(2),
                  _const_spec(econst.shape), _const_spec(ovlt.shape), _const_spec(cbias.shape),
                  _const_spec(wbias.shape), _const_spec(fbias.shape)],
        out_specs=pl.BlockSpec((Q_TILE, qw), lambda b, k, i: (b * nq + i, k)),
        out_shape=jax.ShapeDtypeStruct((n, N_HEADS * HEAD_DIM), BF16),
        scratch_shapes=[pltpu.VMEM((2 * LANES, rows), BF16), pltpu.VMEM((K_TILE, rows), F32),
                        pltpu.VMEM((K_TILE, rows), F32), pltpu.VMEM((1, rows), F32), pltpu.VMEM((1, rows), F32),
                        pltpu.VMEM((HEAD_DIM, rows), F32), pltpu.VMEM((HEAD_DIM, rows), F32)],
        compiler_params=_cparams(("arbitrary", "arbitrary", "arbitrary")), name="attend_prompt",
    )(qt, gt, kcvc, vct, kb, vt, kb, vt, econst, ovlt, cbias, wbias, fbias)


def _attend_sample_kernel(past_len, pt_ref, q_ref, g_ref, kcvc_ref, cache_ref, slc_ref, wcache_ref, win_ref,
                          e_ref, ovl_ref, o_ref, buf_ref, sem_ref, ns_ref, oc_ref, m_ref, l_ref, acc_ref):
    g = pl.program_id(1)
    ng = pl.num_programs(1)
    nr = N_HEADS // N_KV_HEADS
    t = SAMPLE_ROWS
    rows = 2 * nr * t
    n_pages = buf_ref.shape[2] // cache_ref.shape[2]
    keys = buf_ref.shape[2]
    slot = _gather_step(pt_ref, cache_ref, buf_ref, sem_ref, n_pages)
    tq = past_len - (t - 4) + lax.broadcasted_iota(jnp.int32, (t, 1), 0)
    tq_rows = jnp.concatenate([tq] * (2 * nr), axis=0)
    qs = [_stack_heads(q_ref[:, k * nr * LANES:(k + 1) * nr * LANES], nr) for k in range(2)]

    @pl.when(g == 0)
    def _():
        ncmp = kcvc_ref.shape[1]
        cidx = lax.broadcasted_iota(jnp.int32, (1, ncmp), 1)
        c_mask = (cidx >= 1) & (cidx * CMP_STRIDE + (CMP_LEN - CMP_STRIDE - 1) <= tq_rows)
        width = ovl_ref.shape[1]
        lane = lax.broadcasted_iota(jnp.int32, (1, width), 1)
        per = keys // SLC_LEN
        blk = (lane // LANES) * per + lane % LANES
        real = (lane % LANES < per) & (blk <= past_len // SLC_LEN)
        psums = []
        for k in range(2):
            p_c = _masked_softmax(_dot_nt(qs[k], kcvc_ref[0, :, k * LANES:(k + 1) * LANES]), c_mask)
            oc_ref[k] = _dot(p_c.astype(BF16), kcvc_ref[0, :, (2 + k) * LANES:(3 + k) * LANES])
            for gl in range(2):
                psum = p_c[gl * nr * t:gl * nr * t + t]
                for r in range(1, nr):
                    psum = psum + p_c[(gl * nr + r) * t:(gl * nr + r + 1) * t]
                psums.append(psum)
        imp = jnp.dot(jnp.concatenate(psums, axis=0), ovl_ref[...], preferred_element_type=F32,
                      precision=lax.Precision.HIGHEST)
        ns = _select_blocks(imp, jnp.where(real, blk, -1), jnp.concatenate([tq] * 4, axis=0), axis=1)
        for k in range(2):
            for gl in range(2):
                for r in range(nr):
                    ns_ref[k, (gl * nr + r) * t:(gl * nr + r + 1) * t, :] = ns[(2 * k + gl) * t:(2 * k + gl + 1) * t]
        m_ref[...] = jnp.full(m_ref.shape, -jnp.inf, F32)
        l_ref[...] = jnp.zeros(l_ref.shape, F32)
        acc_ref[...] = jnp.zeros(acc_ref.shape, F32)

    page = buf_ref.at[slot]
    for k in range(2):
        ns = ns_ref[k, :, pl.ds(pl.multiple_of(g * LANES, LANES), LANES)]
        qa = jnp.concatenate([qs[k], ns.astype(BF16)], axis=1)
        ka = jnp.concatenate([page[k * LANES:(k + 1) * LANES, :].astype(BF16), e_ref[...]], axis=0)
        v = page[(2 + k) * LANES:(3 + k) * LANES, :].astype(BF16)
        m_ref[k], l_ref[k], acc_ref[k] = _flash_update((m_ref[k], l_ref[k], acc_ref[k]), _dot(qa, ka), v,
                                                       v_transposed=True)

    @pl.when(g == ng - 1)
    def _():
        kidx = lax.broadcasted_iota(jnp.int32, (1, t), 1)
        kpos = past_len - (t - 4) + kidx
        new_ok = (kidx >= t - 4) & (kpos <= tq_rows)
        wb = wcache_ref.shape[1]
        wpos = past_len - wb + lax.broadcasted_iota(jnp.int32, (1, wb), 1)
        w_old = (wpos <= tq_rows) & (wpos > tq_rows - WINDOW) & (wpos >= 0)
        w_new = new_ok & (kpos > tq_rows - WINDOW)
        for k in range(2):
            kcol, vcol = slice(k * LANES, (k + 1) * LANES), slice((2 + k) * LANES, (3 + k) * LANES)
            qf = qs[k].astype(F32)
            s_new = jnp.where(new_ok, _dot_nt(qf, slc_ref[:, kcol]), -MASK_BIG)
            _, l, acc = _flash_update((m_ref[k], l_ref[k], acc_ref[k]), s_new, slc_ref[:, vcol])
            o_s = acc / l
            so = jnp.where(w_old, _dot_nt(qs[k], wcache_ref[0, :, kcol].astype(BF16)), -jnp.inf)
            sn = jnp.where(w_new, _dot_nt(qf, win_ref[:, kcol]), -jnp.inf)
            m = jnp.maximum(jnp.max(so, axis=-1, keepdims=True), jnp.max(sn, axis=-1, keepdims=True))
            m = jnp.where(m == -jnp.inf, 0.0, m)
            eo = jnp.where(w_old, jnp.exp2(so - m), 0.0)
            en = jnp.where(w_new, jnp.exp2(sn - m), 0.0)
            den = jnp.maximum(jnp.sum(eo, axis=-1, keepdims=True) + jnp.sum(en, axis=-1, keepdims=True), 1e-30)
            o_w = (_dot(eo.astype(BF16), wcache_ref[0, :, vcol].astype(BF16)) + _dot(en, win_ref[:, vcol])) / den
            oc, os_, ow = _merge_pair(oc_ref[k], nr, t), _merge_pair(o_s, nr, t), _merge_pair(o_w, nr, t)
            gk = g_ref.at[:, k * LANES:(k + 1) * LANES]
            for r in range(nr):
                o = _gate_vec(gk, 0, r, t) * oc[r] + _gate_vec(gk, 1, r, t) * os_[r] + _gate_vec(gk, 2, r, t) * ow[r]
                o_ref[:, (k * nr + r) * LANES:(k * nr + r + 1) * LANES] = o.astype(o_ref.dtype)


def _attend_sample(page_table, q, gates, kcvc, cache_t, slc_new, wcache, win_new, econst_t, ovl, past_len):
    db, n_pages = page_table.shape
    page_rows = cache_t.shape[2]
    econst = econst_t
    groups = n_pages // PAGES_PER_STEP
    keys = PAGES_PER_STEP * page_rows
    nr = N_HEADS // N_KV_HEADS
    t = SAMPLE_ROWS
    rows = 2 * nr * t
    qw = N_HEADS * HEAD_DIM
    cmap = lambda nd: (lambda b, g, pt: (0,) * nd)
    grid_spec = pltpu.PrefetchScalarGridSpec(
        num_scalar_prefetch=1, grid=(db, groups),
        in_specs=[pl.BlockSpec((t, qw), lambda b, g, pt: (b, 0)),
                  pl.BlockSpec((t, 2 * LANES), lambda b, g, pt: (b, 0)),
                  pl.BlockSpec((1,) + kcvc.shape[1:], lambda b, g, pt: (b, 0, 0)),
                  pl.BlockSpec(memory_space=pl.ANY),
                  pl.BlockSpec((t, KV_W), lambda b, g, pt: (b, 0)),
                  pl.BlockSpec((1,) + wcache.shape[1:], lambda b, g, pt: (b, 0, 0)),
                  pl.BlockSpec((t, KV_W), lambda b, g, pt: (b, 0)),
                  pl.BlockSpec(econst.shape, cmap(2), pipeline_mode=pl.Buffered(1)),
                  pl.BlockSpec(ovl.shape, cmap(2), pipeline_mode=pl.Buffered(1))],
        out_specs=pl.BlockSpec((t, qw), lambda b, g, pt: (b, 0)),
        scratch_shapes=[pltpu.VMEM((2, KV_W, keys), F32), pltpu.SemaphoreType.DMA((2,)),
                        pltpu.VMEM((2, rows, ovl.shape[1]), F32), pltpu.VMEM((2, rows, LANES), F32),
                        pltpu.VMEM((2, rows, 1), F32), pltpu.VMEM((2, rows, 1), F32),
                        pltpu.VMEM((2, rows, LANES), F32)])
    return pl.pallas_call(
        functools.partial(_attend_sample_kernel, past_len), grid_spec=grid_spec,
        out_shape=jax.ShapeDtypeStruct((db * t, qw), BF16),
        compiler_params=_cparams(("arbitrary", "arbitrary")), name="attend_sample",
    )(page_table, q, gates, kcvc, cache_t, slc_new, wcache, win_new, econst, ovl)


def _post_kernel(sample, tiles_per_seq, *refs):
    if sample:
        (x_ref, o_ref, ma_ref, sgb_ref, wao_ref, wo_ref, gf_ref, wfi_ref, fcw_ref, wfd_ref, gfin_ref, st_ref,
         y_ref, ast_ref, abuf) = refs
    else:
        (x_ref, o_ref, ma_ref, sgb_ref, wao_ref, wo_ref, gf_ref, wfi_ref, fcw_ref, wfd_ref, gfin_ref,
         y_ref, ast_ref, abuf) = refs
    rows = x_ref.shape[0]
    dff = fcw_ref.shape[1]
    m = ma_ref[...] + sgb_ref[...] * _dot(o_ref[...], wao_ref[...])
    h = x_ref[...] + _dot(m.astype(BF16), wo_ref[...])
    ag = _dot(_rms(h, gf_ref[...]).astype(BF16), wfi_ref[...])
    a = ag[:, 0:dff]
    if sample:
        row = lax.broadcasted_iota(jnp.int32, (rows, 1), 0) % SAMPLE_ROWS
        is_state = (row >= SAMPLE_ROWS - 4 - (CONV_W - 1)) & (row < SAMPLE_ROWS - 4)
        a = jnp.where(is_state, st_ref[...], a)
        abuf[0:SUBLANES, :] = jnp.zeros((SUBLANES, dff), F32)
    else:
        @pl.when(pl.program_id(0) % tiles_per_seq == 0)
        def _():
            abuf[0:SUBLANES, :] = jnp.zeros((SUBLANES, dff), F32)
    ac = _shifted_conv(abuf, a, fcw_ref, rows)
    tail = abuf[rows:rows + SUBLANES, :]
    if sample:
        ast_ref[...] = a
    else:
        ast_ref[0] = tail
        abuf[0:SUBLANES, :] = tail
    hh = h + _dot((jax.nn.silu(ac) * ag[:, dff:2 * dff]).astype(BF16), wfd_ref[...])
    y_ref[...] = _rms(hh, gfin_ref[...])


def _post(x, o, ma, sgb, w, state, seq_rows):
    n, d = x.shape
    sample = state is not None
    tm = n if sample else ROW_TILE
    tiles_per_seq = seq_rows // tm
    n_seq = n // seq_rows
    wao, wo, gf, wfi, fcw, wfd, gfin = w
    dff = fcw.shape[1]
    row = lambda w_: pl.BlockSpec((tm, w_), lambda i: (i, 0))
    args = [x, o, ma, sgb, *w]
    in_specs = [row(d), row(o.shape[1]), row(d), row(d)] + [_const_spec(a.shape) for a in w]
    if sample:
        args.append(state)
        in_specs.append(row(dff))
        ast_shape, ast_spec = jax.ShapeDtypeStruct((n, dff), F32), row(dff)
    else:
        ast_shape = jax.ShapeDtypeStruct((n_seq, SUBLANES, dff), F32)
        ast_spec = pl.BlockSpec((1, SUBLANES, dff), lambda i: (i // tiles_per_seq, 0, 0))
    return pl.pallas_call(
        functools.partial(_post_kernel, sample, tiles_per_seq),
        grid=(n // tm,), in_specs=in_specs, out_specs=[row(d), ast_spec],
        out_shape=[jax.ShapeDtypeStruct((n, d), F32), ast_shape],
        scratch_shapes=[pltpu.VMEM((tm + SUBLANES, dff), F32)],
        compiler_params=_cparams(("arbitrary",)), name="post_sample" if sample else "post_prompt",
    )(*args)


def _pair_head_order():
    nr = N_HEADS // N_KV_HEADS
    return [2 * nr * k + nr * half + r for k in range(2) for r in range(nr) for half in range(2)]


def _rope_tables(pos):
    half = ROT_DIM // 2
    inv_freq = jnp.power(ROPE_THETA, -jnp.arange(half, dtype=F32) * (2.0 / ROT_DIM))
    ang = pos.astype(F32)[:, None] * inv_freq[None, :]
    cos, sin = jnp.cos(ang), jnp.sin(ang)
    n = pos.shape[0]
    one, zero = jnp.ones((n, HEAD_DIM - ROT_DIM), F32), jnp.zeros((n, HEAD_DIM - ROT_DIM), F32)
    zh = jnp.zeros((n, half), F32)
    tabs = (jnp.concatenate([cos, cos, one], 1), jnp.concatenate([zh, sin, zero], 1),
            jnp.concatenate([-sin, zh, zero], 1))
    return tuple(jnp.tile(a, (1, LANES // HEAD_DIM)) for a in tabs)


def _overlap_matrix(n_rows, n_cols, col_block):
    c = np.arange(n_rows)[:, None] - 1
    blk = col_block[None, :]
    cs, ss = c * CMP_STRIDE, blk * SLC_LEN
    return ((c >= 0) & (blk >= 0) & (cs < ss + SLC_LEN) & (cs + CMP_LEN > ss)).astype(np.float32)


def _block_onehot(n_keys, transposed=False):
    e = (np.arange(n_keys)[:, None] // SLC_LEN == np.arange(LANES)[None, :]).astype(np.float32)
    return jnp.asarray(-MASK_BIG * (e.T if transposed else e), dtype=BF16)


def _split_w_in(w_in, d):
    conv_dim = d
    q_dim = N_HEADS * HEAD_DIM
    kv_dim = N_BRANCH * KV_W
    o0 = 3 * conv_dim
    wc = w_in[:, 0:o0]
    wq = w_in[:, o0:o0 + q_dim].reshape(d, N_HEADS, HEAD_DIM)[:, np.array(_pair_head_order())].reshape(d, q_dim)
    o1 = o0 + q_dim
    wkv = w_in[:, o1:o1 + kv_dim]
    o2 = o1 + kv_dim
    wl = w_in[:, o2:o2 + N_BRANCH * N_HEADS].reshape(d, N_HEADS, N_BRANCH)
    wl = wl[:, np.array(_pair_head_order())].reshape(d, 2, N_HEADS // 2, N_BRANCH).transpose(0, 1, 3, 2)
    wl = jnp.pad(wl.reshape(d, 2, N_BRANCH * N_HEADS // 2), ((0, 0), (0, 0), (0, LANES - N_BRANCH * N_HEADS // 2)))
    wl = wl.reshape(d, 2 * LANES)
    o3 = o2 + N_BRANCH * N_HEADS
    wgab = w_in[:, o3:o3 + 2 * d]
    return [a.astype(BF16) for a in (wc, wq, wkv, wl, wgab)]


def kernel(x_prompt, x_sample, cache_cmp_kv, cache_slc_kv, cache_win_kv, state_conv_mix, state_conv_ffn,
           page_table, norm_mix_g, w_in, conv_mix_w, w_conv_out, cmp_pe, cmp_w1, cmp_b1, cmp_w2, cmp_b2,
           w_attn_out, w_out, norm_ffn_g, w_ff_in, ff_conv_w, w_ff_down, norm_final_g):
    depth = w_in.shape[0]
    assert depth == 1, "single-layer step"
    b, s, d = x_prompt.shape
    db, t, _ = x_sample.shape
    page_rows = cache_cmp_kv.shape[2]
    past_len = page_table.shape[1] * page_rows
    assert t == 4 and s % K_TILE == 0 and s % CMP_ROWS == 0 and s >= WINDOW + Q_TILE
    assert past_len % (PAGES_PER_STEP * page_rows) == 0 and cache_win_kv.shape[2] == WINDOW
    assert N_SEL <= s // SLC_LEN <= LANES, "prompt selection blocks fit one lane block"
    dff = ff_conv_w.shape[2]
    l = 0

    order = np.array(_pair_head_order())
    front_w = _split_w_in(w_in[l], d) + [w_conv_out[l].astype(BF16)]
    post_w = [w_attn_out[l].reshape(N_HEADS, HEAD_DIM, d)[order].reshape(N_HEADS * HEAD_DIM, d).astype(BF16),
              w_out[l].astype(BF16), norm_ffn_g[l].reshape(1, d), w_ff_in[l].astype(BF16), ff_conv_w[l],
              w_ff_down[l].astype(BF16), norm_final_g.reshape(1, d)]
    g_mix = norm_mix_g[l].reshape(1, d)
    cw = _compress_weights(cmp_pe[l], cmp_w1[l], cmp_b1[l], cmp_w2[l], cmp_b2[l])
    econst = _block_onehot(s)
    econst_t = _block_onehot(PAGES_PER_STEP * page_rows, transposed=True)

    xp = x_prompt.reshape(b * s, d)
    ma, sgb, qt, cmp_p, slc_p, win_p, kb, vt, gt, pst = _front(
        xp, g_mix, _rope_tables(jnp.arange(s, dtype=jnp.int32)), front_w, conv_mix_w[l], None, s)
    kcvc, vct = _compress_prompt(cmp_p, cw, b)
    ovlt_p = jnp.asarray(_overlap_matrix(s // CMP_STRIDE, LANES, np.arange(LANES)).T)
    o = _attend_prompt(qt, gt, kcvc, vct, kb, vt, econst, ovlt_p, b)
    y_p, ast = _post(xp, o, ma, sgb, post_w, None, s)
    kv_shape = (2, N_KV_HEADS, HEAD_DIM)
    wb_p = min(WINDOW, s)
    out_prompt = (
        y_p.reshape(b, s, d),
        cmp_p.reshape((1, b, s) + kv_shape), slc_p.reshape((1, b, s) + kv_shape),
        win_p.reshape((b, s) + kv_shape)[None, :, s - wb_p:],
        pst[None, :, SUBLANES - (CONV_W - 1):], ast[None, :, SUBLANES - (CONV_W - 1):])

    r8 = SAMPLE_ROWS
    pad_rows = lambda a, lo: jnp.pad(a, ((0, 0), (lo, r8 - lo - a.shape[1]), (0, 0))).reshape(db * r8, a.shape[2])
    xs = pad_rows(x_sample, r8 - t)
    st_mix = pad_rows(state_conv_mix[l], r8 - t - (CONV_W - 1))
    st_ffn = pad_rows(state_conv_ffn[l], r8 - t - (CONV_W - 1))
    pos_s = past_len - (r8 - t) + jnp.arange(r8, dtype=jnp.int32)
    tabs_s = tuple(jnp.tile(a, (db, 1)) for a in _rope_tables(pos_s))
    ma_s, sgb_s, q_s, cmp_s, slc_s, win_s, gates_s, p_s = _front(
        xs, g_mix, tabs_s, front_w, conv_mix_w[l], st_mix, db * r8)
    kcvc_s = _compress_sample(page_table, _feature_major_pages(cache_cmp_kv[l]), cw)
    keys_per_step = PAGES_PER_STEP * page_rows
    per = keys_per_step // SLC_LEN
    n_slots = past_len // keys_per_step + 1
    lane = np.arange(n_slots * LANES)
    col_block = np.where(lane % LANES < per, (lane // LANES) * per + lane % LANES, -1)
    col_block = np.where(col_block <= past_len // SLC_LEN, col_block, -1)
    ovl_s = jnp.asarray(_overlap_matrix(past_len // CMP_STRIDE, n_slots * LANES, col_block))
    wcache = cache_win_kv[l].reshape(db, WINDOW, KV_W)
    o_s = _attend_sample(page_table, q_s, gates_s, kcvc_s, _feature_major_pages(cache_slc_kv[l]), slc_s, wcache,
                         win_s, econst_t, ovl_s, past_len)
    y_s, a_s = _post(xs, o_s, ma_s, sgb_s, post_w, st_ffn, db * r8)
    tok = lambda a: a.reshape(db, r8, -1)[:, r8 - t:]
    win_new = jnp.concatenate([wcache, tok(win_s)], axis=1)[:, -WINDOW:]
    out_sample = (
        tok(y_s),
        tok(cmp_s).reshape((1, db, t) + kv_shape), tok(slc_s).reshape((1, db, t) + kv_shape),
        win_new.reshape((1, db, WINDOW) + kv_shape),
        p_s.reshape(db, r8, d)[None, :, r8 - (CONV_W - 1):], a_s.reshape(db, r8, dff)[None, :, r8 - (CONV_W - 1):])

    return (out_prompt[0], out_sample[0], out_prompt[1], out_prompt[2], out_prompt[3], out_prompt[4],
            out_prompt[5], out_sample[1], out_sample[2], out_sample[3], out_sample[4], out_sample[5])
```

```python
import functools

import numpy as np
import jax
import jax.numpy as jnp
from jax import lax
from jax.experimental import pallas as pl
from jax.experimental.pallas import tpu as pltpu

F32 = jnp.float32
BF16 = jnp.bfloat16

N_HEADS = 16
HEAD_DIM = 64
N_KV_HEADS = 4
N_BRANCH = 3
ROT_DIM = 16
ROPE_THETA = 500000.0
CMP_LEN = 32
CMP_STRIDE = 16
SLC_LEN = 64
N_SEL = 16
N_LOCAL = 2
WINDOW = 512
CONV_W = 3
NORM_EPS = 1e-6
FORCED_SCORE = 1e9

LANES = 128
SUBLANES = 8
KV_W = 2 * N_KV_HEADS * HEAD_DIM
MASK_BIG = 2.0 ** 100
LOG2_E = 1.4426950408889634
VMEM_LIMIT = 56 * 1024 * 1024

ROW_TILE = 256
FFN_CHUNKS = 2
CMP_ROWS = 4096
Q_TILE = 128
K_TILE = 512
SAMPLE_ROWS = 8
PAGES_PER_STEP = 32


def _cparams(sem):
    return pltpu.CompilerParams(dimension_semantics=sem, vmem_limit_bytes=VMEM_LIMIT)


def _const_spec(shape):
    nd = len(shape)
    return pl.BlockSpec(shape, lambda *_: (0,) * nd, pipeline_mode=pl.Buffered(1))


def _dot(a, b):
    return jnp.dot(a, b, preferred_element_type=F32)


def _dot_nt(a, b):
    return lax.dot_general(a, b, (((1,), (1,)), ((), ())), preferred_element_type=F32)


def _rms(x, g):
    y = x * lax.rsqrt(jnp.mean(x * x, axis=-1, keepdims=True) + NORM_EPS)
    return y * g


def _rope_block(x, cos, sa, sb):
    return x * cos + pltpu.roll(x, 8, 1) * sa + pltpu.roll(x, LANES - 8, 1) * sb


def _shifted_conv(buf, p, w_ref, rows):
    buf[SUBLANES:SUBLANES + rows, :] = p
    p1 = buf[SUBLANES - 1:SUBLANES - 1 + rows, :]
    p2 = buf[SUBLANES - 2:SUBLANES - 2 + rows, :]
    return p2 * w_ref[0:1, :] + p1 * w_ref[1:2, :] + p * w_ref[2:3, :]


def _front_kernel(sample, tiles_per_seq, *refs):
    (x_ref, g_ref, cos_ref, sa_ref, sb_ref, wc_ref, wq_ref, wkv_ref, wgl_ref, wgab_ref, cw_ref, wco_ref) = refs[:12]
    if sample:
        st_ref, ma_ref, sgb_ref, q_ref, cmp_ref, slc_ref, win_ref, gl_ref, pst_ref, pbuf = refs[12:]
    else:
        ma_ref, sgb_ref, qt_ref, cmp_ref, slc_ref, win_ref, kb_ref, vt_ref, glt_ref, pst_ref, pbuf = refs[12:]
    rows, d = x_ref.shape
    u = _rms(x_ref[...], g_ref[...]).astype(BF16)

    zc = _dot(u, wc_ref[...])
    p = zc[:, d:2 * d] * zc[:, 0:d]
    if sample:
        row = lax.broadcasted_iota(jnp.int32, (rows, 1), 0) % SAMPLE_ROWS
        is_state = (row >= SAMPLE_ROWS - 4 - (CONV_W - 1)) & (row < SAMPLE_ROWS - 4)
        p = jnp.where(is_state, st_ref[...], p)
        pbuf[0:SUBLANES, :] = jnp.zeros((SUBLANES, d), F32)
    else:
        @pl.when(pl.program_id(0) % tiles_per_seq == 0)
        def _():
            pbuf[0:SUBLANES, :] = jnp.zeros((SUBLANES, d), F32)
    yc = _shifted_conv(pbuf, p, cw_ref, rows)
    tail = pbuf[rows:rows + SUBLANES, :]
    if sample:
        pst_ref[...] = p
    else:
        pst_ref[0] = tail
        pbuf[0:SUBLANES, :] = tail
    ya = _dot((zc[:, 2 * d:3 * d] * yc).astype(BF16), wco_ref[...])

    zg = _dot(u, wgab_ref[...])
    ma_ref[...] = jax.nn.sigmoid(zg[:, 0:d]) * ya
    sgb_ref[...] = jax.nn.sigmoid(zg[:, d:2 * d])

    cos, sa, sb = cos_ref[...], sa_ref[...], sb_ref[...]
    zq = _dot(u, wq_ref[...])
    scale = HEAD_DIM ** -0.5 * LOG2_E
    for j in range(zq.shape[1] // LANES):
        blk = _rope_block(zq[:, j * LANES:(j + 1) * LANES], cos, sa, sb) * scale
        if sample:
            q_ref[:, j * LANES:(j + 1) * LANES] = blk.astype(BF16)
        else:
            qt_ref[j * LANES:(j + 1) * LANES, :] = blk.T.astype(BF16)

    zkv = _dot(u, wkv_ref[...])
    per_branch = KV_W // LANES
    half = per_branch // 2
    outs = (cmp_ref, slc_ref, win_ref)
    for j in range(zkv.shape[1] // LANES):
        br, jj = divmod(j, per_branch)
        blk = zkv[:, j * LANES:(j + 1) * LANES]
        if jj < half:
            blk = _rope_block(blk, cos, sa, sb)
        outs[br][:, jj * LANES:(jj + 1) * LANES] = blk
        if br > 0 and not sample:
            c = (br - 1) * half + jj % half
            if jj < half:
                kb_ref[:, c * LANES:(c + 1) * LANES] = blk.astype(BF16)
            else:
                vt_ref[c * LANES:(c + 1) * LANES, :] = blk.T.astype(BF16)

    gl = jax.nn.sigmoid(_dot(u, wgl_ref[...]))
    if sample:
        gl_ref[...] = gl
    else:
        for j in range(gl.shape[1] // LANES):
            glt_ref[j * LANES:(j + 1) * LANES, :] = gl[:, j * LANES:(j + 1) * LANES].T


def _front(x, g, tabs, w, conv_w, state, seq_rows):
    n, d = x.shape
    sample = state is not None
    tm = n if sample else ROW_TILE
    tiles_per_seq = seq_rows // tm
    nt = n // tm
    n_seq = n // seq_rows
    qw = N_HEADS * HEAD_DIM
    row = lambda w_: pl.BlockSpec((tm, w_), lambda i: (i, 0))
    col = lambda h_: pl.BlockSpec((h_, tm), lambda i: (0, i))
    tab = pl.BlockSpec((tm, LANES), lambda i: (i % tiles_per_seq, 0))
    args = [x, g, *tabs, *w[:5], conv_w, w[5]]
    in_specs = [row(d), _const_spec((1, d)), tab, tab, tab] + [_const_spec(a.shape) for a in w[:5]] + [
        _const_spec(conv_w.shape), _const_spec(w[5].shape)]
    f32_rows = lambda w_: jax.ShapeDtypeStruct((n, w_), F32)
    if sample:
        args.append(state)
        in_specs.append(row(d))
        out_shape = [f32_rows(d), f32_rows(d), jax.ShapeDtypeStruct((n, qw), BF16), f32_rows(KV_W), f32_rows(KV_W),
                     f32_rows(KV_W), f32_rows(2 * LANES), f32_rows(d)]
        out_specs = [row(d), row(d), row(qw), row(KV_W), row(KV_W), row(KV_W), row(2 * LANES), row(d)]
    else:
        tail_tiles = WINDOW // tm
        win_tail = pl.BlockSpec((tm, KV_W), lambda i: (
            (i // tiles_per_seq) * tail_tiles + jnp.maximum(i % tiles_per_seq - (tiles_per_seq - tail_tiles), 0), 0))
        out_shape = [f32_rows(d), f32_rows(d), jax.ShapeDtypeStruct((qw, n), BF16), f32_rows(KV_W), f32_rows(KV_W),
                     jax.ShapeDtypeStruct((n_seq * WINDOW, KV_W), F32), jax.ShapeDtypeStruct((n, KV_W), BF16),
                     jax.ShapeDtypeStruct((KV_W, n), BF16), jax.ShapeDtypeStruct((2 * LANES, n), F32),
                     jax.ShapeDtypeStruct((n_seq, SUBLANES, d), F32)]
        out_specs = [row(d), row(d), col(qw), row(KV_W), row(KV_W), win_tail, row(KV_W), col(KV_W),
                     col(2 * LANES), pl.BlockSpec((1, SUBLANES, d), lambda i: (i // tiles_per_seq, 0, 0))]
    return pl.pallas_call(
        functools.partial(_front_kernel, sample, tiles_per_seq),
        grid=(nt,), in_specs=in_specs, out_specs=out_specs, out_shape=out_shape,
        scratch_shapes=[pltpu.VMEM((tm + SUBLANES, d), F32)],
        compiler_params=_cparams(("arbitrary",)), name="front_sample" if sample else "front_prompt",
    )(*args)


def _cmp_bias_kernel(pe_ref, w1_ref, b1_ref, o_ref):
    for kv in range(2):
        a = _dot(pe_ref[kv], w1_ref[kv])
        o_ref[kv] = jnp.broadcast_to(b1_ref[kv] + a[0:1, 0:LANES] + a[1:2, LANES:2 * LANES], (SUBLANES, LANES))


def _compress_tile(lhs_of, w1_ref, hb_ref, w2_ref, b2_ref, carry_ref, sh_ref, out_ref, vt_ref=None):
    nck = out_ref.shape[0]
    for j in range(KV_W // LANES):
        kv = j // 2
        a = _dot(lhs_of(j), w1_ref[kv])
        sh_ref[SUBLANES:SUBLANES + nck, :] = a[:, 0:LANES]
        sh_ref[0:SUBLANES, :] = carry_ref[j]
        hid = sh_ref[SUBLANES - 1:SUBLANES - 1 + nck, :] + a[:, LANES:2 * LANES] + hb_ref[kv][0:1, :]
        carry_ref[j] = sh_ref[nck:nck + SUBLANES, :]
        o = _dot(jax.nn.silu(hid).astype(BF16), w2_ref[kv]) + b2_ref[kv]
        out_ref[:, j * LANES:(j + 1) * LANES] = o.astype(out_ref.dtype)
        if vt_ref is not None and kv == 1:
            vt_ref[(j - 2) * LANES:(j - 1) * LANES, :] = o.T.astype(vt_ref.dtype)


def _compress_prompt_kernel(r0, r1, r2, r3, w1_ref, hb_ref, w2_ref, b2_ref, out_ref, vt_ref, carry_ref, sh_ref):
    @pl.when(pl.program_id(1) == 0)
    def _():
        carry_ref[...] = jnp.zeros(carry_ref.shape, F32)
    planes = (r0, r1, r2, r3)
    nck = out_ref.shape[1]

    def lhs_of(j):
        return jnp.concatenate([planes[j][pl.ds(s, nck, stride=CMP_STRIDE), :].astype(BF16)
                                for s in range(CMP_STRIDE)], axis=1)

    _compress_tile(lhs_of, w1_ref, hb_ref, w2_ref, b2_ref, carry_ref, sh_ref, out_ref.at[0], vt_ref.at[0])


def _page_copies(pt_ref, cache_ref, buf_ref, sem_ref, slot, b, grp, n_pages):
    page_rows = cache_ref.shape[2]
    return [pltpu.make_async_copy(cache_ref.at[pt_ref[b, grp * n_pages + p]],
                                  buf_ref.at[slot, :, pl.ds(p * page_rows, page_rows)], sem_ref.at[slot])
            for p in range(n_pages)]


def _gather_step(pt_ref, cache_ref, buf_ref, sem_ref, n_pages):
    b, g = pl.program_id(0), pl.program_id(1)
    nb, ng = pl.num_programs(0), pl.num_programs(1)
    step = b * ng + g
    slot = step % 2

    @pl.when(step == 0)
    def _():
        for c in _page_copies(pt_ref, cache_ref, buf_ref, sem_ref, 0, 0, 0, n_pages):
            c.start()

    @pl.when(step + 1 < nb * ng)
    def _():
        nxt = step + 1
        for c in _page_copies(pt_ref, cache_ref, buf_ref, sem_ref, 1 - slot, nxt // ng, nxt % ng, n_pages):
            c.start()

    for c in _page_copies(pt_ref, cache_ref, buf_ref, sem_ref, slot, b, g, n_pages):
        c.wait()
    return slot


def _compress_sample_kernel(pt_ref, cache_ref, perm_ref, w1_ref, hb_ref, w2_ref, b2_ref, out_ref,
                            buf_ref, sem_ref, carry_ref, sh_ref):
    n_pages = buf_ref.shape[2] // cache_ref.shape[2]
    slot = _gather_step(pt_ref, cache_ref, buf_ref, sem_ref, n_pages)

    @pl.when(pl.program_id(1) == 0)
    def _():
        carry_ref[...] = jnp.zeros(carry_ref.shape, F32)

    span = perm_ref.shape[0]
    ck = span // CMP_STRIDE

    def permuted(half, g):
        x = buf_ref[slot, half * 2 * LANES:(half + 1) * 2 * LANES, g * span:(g + 1) * span]
        return _dot_nt(perm_ref[...], x.astype(BF16)).astype(BF16)

    tiles = [[permuted(half, g) for g in range(buf_ref.shape[2] // span)] for half in range(2)]

    def lhs_of(j):
        half, jj = divmod(j, 2)
        return jnp.concatenate(
            [jnp.concatenate([tl[s * ck:(s + 1) * ck, jj * LANES:(jj + 1) * LANES] for tl in tiles[half]], axis=0)
             for s in range(CMP_STRIDE)], axis=1)

    _compress_tile(lhs_of, w1_ref, hb_ref, w2_ref, b2_ref, carry_ref, sh_ref, out_ref.at[0])


def _compress_weights(cmp_pe, cmp_w1, cmp_b1, cmp_w2, cmp_b2):
    r = CMP_LEN // CMP_STRIDE
    eye2 = jnp.eye(2, dtype=F32)
    w1 = cmp_w1.reshape(2, r, CMP_STRIDE, HEAD_DIM, HEAD_DIM)
    w1bd = jnp.einsum("krsde,hg->kshdrge", w1, eye2).reshape(2, CMP_STRIDE * LANES, r * LANES).astype(BF16)
    w2bd = jnp.einsum("kde,hg->khdge", cmp_w2, eye2).reshape(2, LANES, LANES).astype(BF16)
    pe = cmp_pe.reshape(2, r, CMP_STRIDE, 1, HEAD_DIM)
    pe = jnp.broadcast_to(pe, (2, r, CMP_STRIDE, 2, HEAD_DIM)).reshape(2, r, CMP_STRIDE * LANES)
    pe = jnp.pad(pe, ((0, 0), (0, SUBLANES - r), (0, 0))).astype(BF16)
    b1 = jnp.tile(cmp_b1, (1, 2)).reshape(2, 1, LANES)
    b2 = jnp.tile(cmp_b2, (1, 2)).reshape(2, 1, LANES)
    hb = pl.pallas_call(
        _cmp_bias_kernel, out_shape=jax.ShapeDtypeStruct((2, SUBLANES, LANES), F32), name="compress_bias",
    )(pe, w1bd, b1)
    return w1bd, hb, w2bd, b2


def _compress_prompt(cmp_rows, cw, n_seq):
    n = cmp_rows.shape[0]
    s = n // n_seq
    nck = CMP_ROWS // CMP_STRIDE
    tiles = s // CMP_ROWS
    w1bd, hb, w2bd, b2 = cw
    return pl.pallas_call(
        _compress_prompt_kernel,
        grid=(n_seq, tiles),
        in_specs=[pl.BlockSpec((CMP_ROWS, LANES), functools.partial(lambda j, b, t: (b * tiles + t, j), j))
                  for j in range(KV_W // LANES)] + [
                  _const_spec(w1bd.shape), _const_spec(hb.shape), _const_spec(w2bd.shape), _const_spec(b2.shape)],
        out_specs=[pl.BlockSpec((1, nck, KV_W), lambda b, t: (b, t, 0)),
                   pl.BlockSpec((1, KV_W // 2, nck), lambda b, t: (b, 0, t))],
        out_shape=[jax.ShapeDtypeStruct((n_seq, s // CMP_STRIDE, KV_W), BF16),
                   jax.ShapeDtypeStruct((n_seq, KV_W // 2, s // CMP_STRIDE), BF16)],
        scratch_shapes=[pltpu.VMEM((KV_W // LANES, SUBLANES, LANES), F32),
                        pltpu.VMEM((nck + SUBLANES, LANES), F32)],
        compiler_params=_cparams(("arbitrary", "arbitrary")), name="compress_prompt",
    )(cmp_rows, cmp_rows, cmp_rows, cmp_rows, w1bd, hb, w2bd, b2)


def _feature_major_pages(cache):
    return cache.transpose(0, 2, 3, 4, 1).reshape(cache.shape[0], KV_W, cache.shape[1])


def _compress_sample(page_table, cache_t, cw):
    db, n_pages = page_table.shape
    page_rows = cache_t.shape[2]
    pages_per = CMP_ROWS // page_rows
    groups = n_pages // pages_per
    nck = CMP_ROWS // CMP_STRIDE
    w1bd, hb, w2bd, b2 = cw
    span = 2 * page_rows
    ck = span // CMP_STRIDE
    row = np.arange(span)
    s_of, c_of = row // ck, row % ck
    per_page = page_rows // CMP_STRIDE
    tok = (c_of // per_page) * page_rows + CMP_STRIDE * (c_of % per_page) + s_of
    perm = jnp.asarray(tok[:, None] == np.arange(span)[None, :], dtype=BF16)
    cspec = lambda a: pl.BlockSpec(a.shape, lambda b, g, pt: (0,) * a.ndim, pipeline_mode=pl.Buffered(1))
    grid_spec = pltpu.PrefetchScalarGridSpec(
        num_scalar_prefetch=1, grid=(db, groups),
        in_specs=[pl.BlockSpec(memory_space=pl.ANY), cspec(perm), cspec(w1bd), cspec(hb), cspec(w2bd), cspec(b2)],
        out_specs=pl.BlockSpec((1, nck, KV_W), lambda b, g, pt: (b, g, 0)),
        scratch_shapes=[pltpu.VMEM((2, KV_W, CMP_ROWS), F32), pltpu.SemaphoreType.DMA((2,)),
                        pltpu.VMEM((KV_W // LANES, SUBLANES, LANES), F32),
                        pltpu.VMEM((nck + SUBLANES, LANES), F32)])
    return pl.pallas_call(
        _compress_sample_kernel, grid_spec=grid_spec,
        out_shape=jax.ShapeDtypeStruct((db, n_pages * page_rows // CMP_STRIDE, KV_W), BF16),
        compiler_params=_cparams(("arbitrary", "arbitrary")), name="compress_sample",
    )(page_table, cache_t, perm, w1bd, hb, w2bd, b2)


def _masked_softmax(s, mask):
    s = jnp.where(mask, s, -jnp.inf)
    m = jnp.max(s, axis=-1, keepdims=True)
    m = jnp.where(m == -jnp.inf, 0.0, m)
    e = jnp.where(mask, jnp.exp2(s - m), 0.0)
    return e / jnp.maximum(jnp.sum(e, axis=-1, keepdims=True), 1e-30)


def _select_blocks(imp, blk, t_pos, axis):
    cur = t_pos // SLC_LEN
    valid = blk <= cur
    forced = (blk == 0) | (valid & (blk > cur - N_LOCAL))
    cand = jnp.where(forced, -jnp.inf, jnp.where(valid, imp, -FORCED_SCORE))
    cand = jnp.where(blk >= 0, cand, -jnp.inf)
    length = imp.shape[axis]
    idx = lax.broadcasted_iota(jnp.int32, imp.shape, axis).astype(F32)

    def pick(_, carry):
        cand, notsel = carry
        m = jnp.max(cand, axis=axis, keepdims=True)
        first = jnp.min(jnp.where(cand == m, idx, float(length)), axis=axis, keepdims=True)
        hit = idx == first
        return jnp.where(hit, -jnp.inf, cand), jnp.where(hit, 0.0, notsel)

    _, notsel = lax.fori_loop(0, N_SEL - (N_LOCAL + 1), pick, (cand, jnp.where(forced, 0.0, 1.0)), unroll=True)
    return notsel


def _flash_update(carry, s, v, v_transposed=False):
    m, l, acc = carry
    m_new = jnp.maximum(m, jnp.max(s, axis=-1, keepdims=True))
    alpha = jnp.exp2(m - m_new)
    p = jnp.exp2(s - m_new)
    l = alpha * l + jnp.sum(p, axis=-1, keepdims=True)
    pv = _dot_nt(p.astype(v.dtype), v) if v_transposed else _dot(p.astype(v.dtype), v)
    return m_new, l, alpha * acc + pv


def _half_mask(shape):
    return lax.broadcasted_iota(jnp.int32, shape, len(shape) - 1) < HEAD_DIM


def _stack_heads(q_blk, nr):
    first = _half_mask((q_blk.shape[0], LANES))
    zero = jnp.zeros((), q_blk.dtype)
    parts = []
    for gl in range(2):
        for r in range(nr):
            blk = q_blk[:, r * LANES:(r + 1) * LANES]
            parts.append(jnp.where(first if gl == 0 else ~first, blk, zero))
    return jnp.concatenate(parts, axis=0)


def _merge_pair(o, nr, t):
    first = _half_mask((t, LANES))
    return [jnp.where(first, o[r * t:(r + 1) * t], o[(nr + r) * t:(nr + r + 1) * t]) for r in range(nr)]


def _gate_vec(g_ref, br, r, t):
    c = br * 2 * (N_HEADS // N_KV_HEADS) + r * 2
    return jnp.where(_half_mask((t, LANES)), g_ref[:, c:c + 1], g_ref[:, c + 1:c + 2])


def _biased_exp0(s, bias):
    s = s + jnp.concatenate([bias] * (s.shape[1] // bias.shape[1]), axis=1)
    m = jnp.max(s, axis=0, keepdims=True)
    m = jnp.where(m < -0.5 * MASK_BIG, 0.0, m)
    e = jnp.exp2(s - m)
    return e, 1.0 / jnp.maximum(jnp.sum(e, axis=0, keepdims=True), 1e-30)


def _pv_own_head(vt, p):
    half = p.shape[1] // 2
    return jnp.concatenate([_dot(vt[0:HEAD_DIM, :], p[:, 0:half]), _dot(vt[HEAD_DIM:2 * HEAD_DIM, :], p[:, half:])],
                           axis=1)


def _flash_update0(carry, s, vt):
    m, l, acc = carry
    m_new = jnp.maximum(m, jnp.max(s, axis=0, keepdims=True))
    alpha = jnp.exp2(m - m_new)
    p = jnp.exp2(s - m_new)
    l = alpha * l + jnp.sum(p, axis=0, keepdims=True)
    return m_new, l, alpha * acc + _pv_own_head(vt, p.astype(vt.dtype))


def _attend_prompt_kernel(qt_ref, gt_ref, kc_ref, vct_ref, kb_ref, vt_ref, e_ref, ovlt_ref,
                          cbias_ref, wbias_ref, fbias_ref,
                          o_ref, qa_ref, sa_ref, sb_ref, m_ref, l_ref, acc_ref, oc_ref, ow_ref):
    i = pl.program_id(1)
    nr = N_HEADS // N_KV_HEADS
    t = Q_TILE
    rows = 2 * nr * t
    n_pairs = N_KV_HEADS // 2
    t0 = i * t
    tq = t0 + lax.broadcasted_iota(jnp.int32, (1, t), 1)
    first = lax.broadcasted_iota(jnp.int32, (LANES, t), 0) < HEAD_DIM
    zero = jnp.zeros((), BF16)
    ncmp = kc_ref.shape[1]
    pair_lanes = lambda k: slice(k * LANES, (k + 1) * LANES)
    win_lanes = lambda k: slice((n_pairs + k) * LANES, (n_pairs + k + 1) * LANES)

    qts = [jnp.concatenate([jnp.where(first if gl == 0 else ~first,
                                      qt_ref[(k * nr + r) * LANES:(k * nr + r + 1) * LANES, :], zero)
                            for gl in range(2) for r in range(nr)], axis=1) for k in range(n_pairs)]

    c_off = pl.multiple_of(ncmp - i * (t // CMP_STRIDE), t // CMP_STRIDE)
    cbias = cbias_ref[pl.ds(c_off, ncmp), :]
    not_junk = lax.broadcasted_iota(jnp.int32, (SUBLANES, 1), 0) >= 1
    psum = []
    for k in range(n_pairs):
        s_c = _dot(kc_ref[0, :, pair_lanes(k)], qts[k])
        head = jnp.where(not_junk, s_c[0:SUBLANES], -MASK_BIG)
        e_c, inv_c = _biased_exp0(jnp.concatenate([head, s_c[SUBLANES:]], axis=0), cbias)
        p_c = e_c * inv_c
        oc_ref[k] = _pv_own_head(vct_ref[0, pair_lanes(k), :], p_c.astype(BF16))
        for gl in range(2):
            acc = p_c[:, gl * nr * t:gl * nr * t + t]
            for r in range(1, nr):
                acc = acc + p_c[:, (gl * nr + r) * t:(gl * nr + r + 1) * t]
            psum.append(acc)
    imp = jnp.dot(ovlt_ref[...], jnp.concatenate(psum, axis=1), preferred_element_type=F32,
                  precision=lax.Precision.HIGHEST)

    wlen = WINDOW + t
    ws = pl.multiple_of(jnp.maximum(t0 - WINDOW, 0), t)
    anchor = (imp[0:1, 0:1] * 0.0).astype(BF16)
    wbias = wbias_ref[jnp.minimum(i, WINDOW // t)]
    for k in range(n_pairs):
        e_w, inv_w = _biased_exp0(_dot(kb_ref[pl.ds(ws, wlen), win_lanes(k)] + anchor, qts[k]), wbias)
        ow_ref[k] = _pv_own_head(vt_ref[win_lanes(k), pl.ds(ws, wlen)], e_w.astype(BF16)) * inv_w

    nslc = ovlt_ref.shape[0]
    blk = lax.broadcasted_iota(jnp.int32, (nslc, 1), 0)
    ns = _select_blocks(imp, blk, jnp.concatenate([tq] * N_KV_HEADS, axis=1), axis=0).astype(BF16)
    for k in range(n_pairs):
        flags = [ns[:, (2 * k + gl) * t:(2 * k + gl + 1) * t] for gl in range(2)]
        qa_ref[k] = jnp.concatenate([qts[k], jnp.concatenate([flags[0]] * nr + [flags[1]] * nr, axis=1)], axis=0)

    def produce(s_ref, j):
        off = pl.multiple_of(j * K_TILE, K_TILE)
        for k in range(n_pairs):
            ka = jnp.concatenate([kb_ref[pl.ds(off, K_TILE), pair_lanes(k)], e_ref[pl.ds(off, K_TILE), :]], axis=1)
            s_ref[k] = _dot(ka, qa_ref[k])

    def consume(s_ref, j, causal):
        off = pl.multiple_of(j * K_TILE, K_TILE)
        for k in range(n_pairs):
            s = s_ref[k]
            if causal:
                fb = fbias_ref[(t0 - off) // t]
                s = s + jnp.concatenate([fb] * (rows // t), axis=1)
            m_ref[k], l_ref[k], acc_ref[k] = _flash_update0(
                (m_ref[k], l_ref[k], acc_ref[k]), s, vt_ref[pair_lanes(k), pl.ds(off, K_TILE)])

    m_ref[...] = jnp.full(m_ref.shape, -jnp.inf, F32)
    l_ref[...] = jnp.zeros(l_ref.shape, F32)
    acc_ref[...] = jnp.zeros(acc_ref.shape, F32)
    last = (t0 + t - 1) // K_TILE
    produce(sa_ref, 0)

    def pair(jj, _):
        j = 2 * jj
        produce(sb_ref, j + 1)
        consume(sa_ref, j, False)
        produce(sa_ref, j + 2)
        consume(sb_ref, j + 1, False)
        return 0

    lax.fori_loop(0, last // 2, pair, 0)

    @pl.when(last % 2 == 1)
    def _():
        produce(sb_ref, last)
        consume(sa_ref, last - 1, False)
        consume(sb_ref, last, True)

    @pl.when(last % 2 == 0)
    def _():
        consume(sa_ref, last, True)

    for k in range(n_pairs):
        branches = (oc_ref[k], acc_ref[k] * (1.0 / l_ref[k]), ow_ref[k])
        for r in range(nr):
            c0, c1 = r * t, (nr + r) * t
            o = None
            for br, ob in enumerate(branches):
                c = k * LANES + br * 2 * nr + r * 2
                gate = jnp.where(first, gt_ref[c:c + 1, :], gt_ref[c + 1:c + 2, :])
                term = gate * jnp.concatenate([ob[:, c0:c0 + t], ob[:, c1:c1 + t]], axis=0)
                o = term if o is None else o + term
            o_ref[:, (k * nr + r) * LANES:(k * nr + r + 1) * LANES] = o.T.astype(o_ref.dtype)


def _mask_bias(visible):
    return jnp.asarray(np.where(visible, 0.0, -MASK_BIG), dtype=F32)


def _attend_prompt(qt, gt, kcvc, vct, kb, vt, econst, ovlt, n_seq):
    t = Q_TILE
    tcol = np.arange(t)[None, :]
    ncmp_ = kcvc.shape[1]
    x = np.arange(-ncmp_, ncmp_)[:, None]
    cbias = _mask_bias(x * CMP_STRIDE + (CMP_LEN - CMP_STRIDE - 1) <= tcol)
    r = np.arange(WINDOW + t)[None, :, None]
    d = (np.arange(WINDOW // t + 1) * t)[:, None, None]
    wbias = _mask_bias((r <= d + tcol[None]) & (r > d + tcol[None] - WINDOW))
    r = np.arange(K_TILE)[None, :, None]
    d = (np.arange(K_TILE // t) * t)[:, None, None]
    fbias = _mask_bias(r <= d + tcol[None])
    n = qt.shape[1]
    s = n // n_seq
    nq = s // Q_TILE
    nr = N_HEADS // N_KV_HEADS
    ncmp = kcvc.shape[1]
    qw = N_HEADS * HEAD_DIM
    rows = 2 * nr * Q_TILE
    n_pairs = N_KV_HEADS // 2
    per_seq = lambda shape, imap: pl.BlockSpec(shape, imap, pipeline_mode=pl.Buffered(1))
    return pl.pallas_call(
        _attend_prompt_kernel,
        grid=(n_seq, nq),
        in_specs=[pl.BlockSpec((qw, Q_TILE), lambda b, i: (0, b * nq + i)),
                  pl.BlockSpec((n_pairs * LANES, Q_TILE), lambda b, i: (0, b * nq + i)),
                  per_seq((1, ncmp, n_pairs * LANES), lambda b, i: (b, 0, 0)),
                  per_seq((1, n_pairs * LANES, ncmp), lambda b, i: (b, 0, 0)),
                  per_seq((s, KV_W), lambda b, i: (b, 0)), per_seq((KV_W, s), lambda b, i: (0, b)),
                  _const_spec(econst.shape), _const_spec(ovlt.shape), _const_spec(cbias.shape),
                  _const_spec(wbias.shape), _const_spec(fbias.shape)],
        out_specs=pl.BlockSpec((Q_TILE, qw), lambda b, i: (b * nq + i, 0)),
        out_shape=jax.ShapeDtypeStruct((n, qw), BF16),
        scratch_shapes=[pltpu.VMEM((n_pairs, 2 * LANES, rows), BF16), pltpu.VMEM((n_pairs, K_TILE, rows), F32),
                        pltpu.VMEM((n_pairs, K_TILE, rows), F32), pltpu.VMEM((n_pairs, 1, rows), F32),
                        pltpu.VMEM((n_pairs, 1, rows), F32), pltpu.VMEM((n_pairs, HEAD_DIM, rows), F32),
                        pltpu.VMEM((n_pairs, HEAD_DIM, rows), F32), pltpu.VMEM((n_pairs, HEAD_DIM, rows), F32)],
        compiler_params=_cparams(("arbitrary", "arbitrary")), name="attend_prompt",
    )(qt, gt, kcvc, vct, kb, vt, econst, ovlt, cbias, wbias, fbias)


def _attend_sample_kernel(past_len, pt_ref, q_ref, g_ref, kcvc_ref, cache_ref, slc_ref, wcache_ref, win_ref,
                          e_ref, ovl_ref, o_ref, buf_ref, sem_ref, ns_ref, oc_ref, m_ref, l_ref, acc_ref):
    g = pl.program_id(1)
    ng = pl.num_programs(1)
    nr = N_HEADS // N_KV_HEADS
    t = SAMPLE_ROWS
    rows = 2 * nr * t
    n_pages = buf_ref.shape[2] // cache_ref.shape[2]
    keys = buf_ref.shape[2]
    slot = _gather_step(pt_ref, cache_ref, buf_ref, sem_ref, n_pages)
    tq = past_len - (t - 4) + lax.broadcasted_iota(jnp.int32, (t, 1), 0)
    tq_rows = jnp.concatenate([tq] * (2 * nr), axis=0)
    qs = [_stack_heads(q_ref[:, k * nr * LANES:(k + 1) * nr * LANES], nr) for k in range(2)]

    @pl.when(g == 0)
    def _():
        ncmp = kcvc_ref.shape[1]
        cidx = lax.broadcasted_iota(jnp.int32, (1, ncmp), 1)
        c_mask = (cidx >= 1) & (cidx * CMP_STRIDE + (CMP_LEN - CMP_STRIDE - 1) <= tq_rows)
        width = ovl_ref.shape[1]
        lane = lax.broadcasted_iota(jnp.int32, (1, width), 1)
        per = keys // SLC_LEN
        blk = (lane // LANES) * per + lane % LANES
        real = (lane % LANES < per) & (blk <= past_len // SLC_LEN)
        psums = []
        for k in range(2):
            p_c = _masked_softmax(_dot_nt(qs[k], kcvc_ref[0, :, k * LANES:(k + 1) * LANES]), c_mask)
            oc_ref[k] = _dot(p_c.astype(BF16), kcvc_ref[0, :, (2 + k) * LANES:(3 + k) * LANES])
            for gl in range(2):
                psum = p_c[gl * nr * t:gl * nr * t + t]
                for r in range(1, nr):
                    psum = psum + p_c[(gl * nr + r) * t:(gl * nr + r + 1) * t]
                psums.append(psum)
        imp = jnp.dot(jnp.concatenate(psums, axis=0), ovl_ref[...], preferred_element_type=F32,
                      precision=lax.Precision.HIGHEST)
        ns = _select_blocks(imp, jnp.where(real, blk, -1), jnp.concatenate([tq] * 4, axis=0), axis=1)
        for k in range(2):
            for gl in range(2):
                for r in range(nr):
                    ns_ref[k, (gl * nr + r) * t:(gl * nr + r + 1) * t, :] = ns[(2 * k + gl) * t:(2 * k + gl + 1) * t]
        m_ref[...] = jnp.full(m_ref.shape, -jnp.inf, F32)
        l_ref[...] = jnp.zeros(l_ref.shape, F32)
        acc_ref[...] = jnp.zeros(acc_ref.shape, F32)

    page = buf_ref.at[slot]
    for k in range(2):
        ns = ns_ref[k, :, pl.ds(pl.multiple_of(g * LANES, LANES), LANES)]
        qa = jnp.concatenate([qs[k], ns.astype(BF16)], axis=1)
        ka = jnp.concatenate([page[k * LANES:(k + 1) * LANES, :].astype(BF16), e_ref[...]], axis=0)
        v = page[(2 + k) * LANES:(3 + k) * LANES, :].astype(BF16)
        m_ref[k], l_ref[k], acc_ref[k] = _flash_update((m_ref[k], l_ref[k], acc_ref[k]), _dot(qa, ka), v,
                                                       v_transposed=True)

    @pl.when(g == ng - 1)
    def _():
        kidx = lax.broadcasted_iota(jnp.int32, (1, t), 1)
        kpos = past_len - (t - 4) + kidx
        new_ok = (kidx >= t - 4) & (kpos <= tq_rows)
        wb = wcache_ref.shape[2]
        wpos =past_len - wb + lax.broadcasted_iota(jnp.int32, (1, wb), 1)
        w_old = (wpos <= tq_rows) & (wpos > tq_rows - WINDOW) & (wpos >= 0)
        w_new = new_ok & (kpos > tq_rows - WINDOW)
        for k in range(2):
            kcol, vcol = slice(k * LANES, (k + 1) * LANES), slice((2 + k) * LANES, (3 + k) * LANES)
            qf = qs[k].astype(F32)
            s_new = jnp.where(new_ok, _dot_nt(qf, slc_ref[:, kcol]), -MASK_BIG)
            _, l, acc = _flash_update((m_ref[k], l_ref[k], acc_ref[k]), s_new, slc_ref[:, vcol])
            o_s = acc / l
            so = jnp.where(w_old, _dot(qs[k], wcache_ref[0, kcol, :].astype(BF16)), -jnp.inf)
            sn = jnp.where(w_new, _dot_nt(qf, win_ref[:, kcol]), -jnp.inf)
            m = jnp.maximum(jnp.max(so, axis=-1, keepdims=True), jnp.max(sn, axis=-1, keepdims=True))
            m = jnp.where(m == -jnp.inf, 0.0, m)
            eo = jnp.where(w_old, jnp.exp2(so - m), 0.0)
            en = jnp.where(w_new, jnp.exp2(sn - m), 0.0)
            den = jnp.maximum(jnp.sum(eo, axis=-1, keepdims=True) + jnp.sum(en, axis=-1, keepdims=True), 1e-30)
            o_w = (_dot_nt(eo.astype(BF16), wcache_ref[0, vcol, :].astype(BF16)) + _dot(en, win_ref[:, vcol])) / den
            oc, os_, ow = _merge_pair(oc_ref[k], nr, t), _merge_pair(o_s, nr, t), _merge_pair(o_w, nr, t)
            gk = g_ref.at[:, k * LANES:(k + 1) * LANES]
            for r in range(nr):
                o = _gate_vec(gk, 0, r, t) * oc[r] + _gate_vec(gk, 1, r, t) * os_[r] + _gate_vec(gk, 2, r, t) * ow[r]
                o_ref[:, (k * nr + r) * LANES:(k * nr + r + 1) * LANES] = o.astype(o_ref.dtype)


def _attend_sample(page_table, q, gates, kcvc, cache_t, slc_new, wcache, win_new, econst_t, ovl, past_len):
    db, n_pages = page_table.shape
    page_rows = cache_t.shape[2]
    econst = econst_t
    groups = n_pages // PAGES_PER_STEP
    keys = PAGES_PER_STEP * page_rows
    nr = N_HEADS // N_KV_HEADS
    t = SAMPLE_ROWS
    rows = 2 * nr * t
    qw = N_HEADS * HEAD_DIM
    cmap = lambda nd: (lambda b, g, pt: (0,) * nd)
    grid_spec = pltpu.PrefetchScalarGridSpec(
        num_scalar_prefetch=1, grid=(db, groups),
        in_specs=[pl.BlockSpec((t, qw), lambda b, g, pt: (b, 0)),
                  pl.BlockSpec((t, 2 * LANES), lambda b, g, pt: (b, 0)),
                  pl.BlockSpec((1,) + kcvc.shape[1:], lambda b, g, pt: (b, 0, 0)),
                  pl.BlockSpec(memory_space=pl.ANY),
                  pl.BlockSpec((t, KV_W), lambda b, g, pt: (b, 0)),
                  pl.BlockSpec((1,) + wcache.shape[1:], lambda b, g, pt: (b, 0, 0)),
                  pl.BlockSpec((t, KV_W), lambda b, g, pt: (b, 0)),
                  pl.BlockSpec(econst.shape, cmap(2), pipeline_mode=pl.Buffered(1)),
                  pl.BlockSpec(ovl.shape, cmap(2), pipeline_mode=pl.Buffered(1))],
        out_specs=pl.BlockSpec((t, qw), lambda b, g, pt: (b, 0)),
        scratch_shapes=[pltpu.VMEM((2, KV_W, keys), F32), pltpu.SemaphoreType.DMA((2,)),
                        pltpu.VMEM((2, rows, ovl.shape[1]), F32), pltpu.VMEM((2, rows, LANES), F32),
                        pltpu.VMEM((2, rows, 1), F32), pltpu.VMEM((2, rows, 1), F32),
                        pltpu.VMEM((2, rows, LANES), F32)])
    return pl.pallas_call(
        functools.partial(_attend_sample_kernel, past_len), grid_spec=grid_spec,
        out_shape=jax.ShapeDtypeStruct((db * t, qw), BF16),
        compiler_params=_cparams(("arbitrary", "arbitrary")), name="attend_sample",
    )(page_table, q, gates, kcvc, cache_t, slc_new, wcache, win_new, econst, ovl)


def _post_kernel(sample, tiles_per_seq, *refs):
    if sample:
        (x_ref, o_ref, ma_ref, sgb_ref, wao_ref, wo_ref, gf_ref, wfi_ref, fcw_ref, wfd_ref, gfin_ref, st_ref,
         y_ref, ast_ref, abuf) = refs
    else:
        (x_ref, o_ref, ma_ref, sgb_ref, wao_ref, wo_ref, gf_ref, wfi_ref, fcw_ref, wfd_ref, gfin_ref,
         y_ref, ast_ref, abuf) = refs
    rows = x_ref.shape[0]
    dff = fcw_ref.shape[1]
    m = ma_ref[...] + sgb_ref[...] * _dot(o_ref[...], wao_ref[...])
    h = x_ref[...] + _dot(m.astype(BF16), wo_ref[...])
    v = _rms(h, gf_ref[...]).astype(BF16)
    if sample:
        row = lax.broadcasted_iota(jnp.int32, (rows, 1), 0) % SAMPLE_ROWS
        is_state = (row >= SAMPLE_ROWS - 4 - (CONV_W - 1)) & (row < SAMPLE_ROWS - 4)
        abuf[0:SUBLANES, :] = jnp.zeros((SUBLANES, dff), F32)
    else:
        @pl.when(pl.program_id(0) % tiles_per_seq == 0)
        def _():
            abuf[0:SUBLANES, :] = jnp.zeros((SUBLANES, dff), F32)
    bounds = [LANES * ((dff // LANES) * c // FFN_CHUNKS) for c in range(FFN_CHUNKS + 1)]
    hh = h
    for lo, hi in zip(bounds[:-1], bounds[1:]):
        a = _dot(v, wfi_ref[:, lo:hi])
        g = _dot(v, wfi_ref[:, dff + lo:dff + hi])
        if sample:
            a = jnp.where(is_state, st_ref[:, lo:hi], a)
        buf = abuf.at[:, lo:hi]
        ac = _shifted_conv(buf, a, fcw_ref.at[:, lo:hi], rows)
        tail = buf[rows:rows + SUBLANES, :]
        if sample:
            ast_ref[:, lo:hi] = a
        else:
            ast_ref[0, :, lo:hi] = tail
            buf[0:SUBLANES, :] = tail
        hh = hh + _dot((jax.nn.silu(ac) * g).astype(BF16), wfd_ref[lo:hi, :])
    y_ref[...] = _rms(hh, gfin_ref[...])


def _post(x, o, ma, sgb, w, state, seq_rows):
    n, d = x.shape
    sample = state is not None
    tm = n if sample else ROW_TILE
    tiles_per_seq = seq_rows // tm
    n_seq = n // seq_rows
    wao, wo, gf, wfi, fcw, wfd, gfin = w
    dff = fcw.shape[1]
    row = lambda w_: pl.BlockSpec((tm, w_), lambda i: (i, 0))
    args = [x, o, ma, sgb, *w]
    in_specs = [row(d), row(o.shape[1]), row(d), row(d)] + [_const_spec(a.shape) for a in w]
    if sample:
        args.append(state)
        in_specs.append(row(dff))
        ast_shape, ast_spec = jax.ShapeDtypeStruct((n, dff), F32), row(dff)
    else:
        ast_shape = jax.ShapeDtypeStruct((n_seq, SUBLANES, dff), F32)
        ast_spec = pl.BlockSpec((1, SUBLANES, dff), lambda i: (i // tiles_per_seq, 0, 0))
    return pl.pallas_call(
        functools.partial(_post_kernel, sample, tiles_per_seq),
        grid=(n // tm,), in_specs=in_specs, out_specs=[row(d), ast_spec],
        out_shape=[jax.ShapeDtypeStruct((n, d), F32), ast_shape],
        scratch_shapes=[pltpu.VMEM((tm + SUBLANES, dff), F32)],
        compiler_params=_cparams(("arbitrary",)), name="post_sample" if sample else "post_prompt",
    )(*args)


def _pair_head_order():
    nr = N_HEADS // N_KV_HEADS
    return [2 * nr * k + nr * half + r for k in range(2) for r in range(nr) for half in range(2)]


def _rope_tables(pos):
    half = ROT_DIM // 2
    inv_freq = jnp.power(ROPE_THETA, -jnp.arange(half, dtype=F32) * (2.0 / ROT_DIM))
    ang = pos.astype(F32)[:, None] * inv_freq[None, :]
    cos, sin = jnp.cos(ang), jnp.sin(ang)
    n = pos.shape[0]
    one, zero = jnp.ones((n, HEAD_DIM - ROT_DIM), F32), jnp.zeros((n, HEAD_DIM - ROT_DIM), F32)
    zh = jnp.zeros((n, half), F32)
    tabs = (jnp.concatenate([cos, cos, one], 1), jnp.concatenate([zh, sin, zero], 1),
            jnp.concatenate([-sin, zh, zero], 1))
    return tuple(jnp.tile(a, (1, LANES // HEAD_DIM)) for a in tabs)


def _overlap_matrix(n_rows, n_cols, col_block):
    c = np.arange(n_rows)[:, None] - 1
    blk = col_block[None, :]
    cs, ss = c * CMP_STRIDE, blk * SLC_LEN
    return ((c >= 0) & (blk >= 0) & (cs < ss + SLC_LEN) & (cs + CMP_LEN > ss)).astype(np.float32)


def _block_onehot(n_keys, transposed=False):
    e = (np.arange(n_keys)[:, None] // SLC_LEN == np.arange(LANES)[None, :]).astype(np.float32)
    return jnp.asarray(-MASK_BIG * (e.T if transposed else e), dtype=BF16)


def _split_w_in(w_in, d):
    conv_dim = d
    q_dim = N_HEADS * HEAD_DIM
    kv_dim = N_BRANCH * KV_W
    o0 = 3 * conv_dim
    wc = w_in[:, 0:o0]
    wq = w_in[:, o0:o0 + q_dim].reshape(d, N_HEADS, HEAD_DIM)[:, np.array(_pair_head_order())].reshape(d, q_dim)
    o1 = o0 + q_dim
    wkv = w_in[:, o1:o1 + kv_dim]
    o2 = o1 + kv_dim
    wl = w_in[:, o2:o2 + N_BRANCH * N_HEADS].reshape(d, N_HEADS, N_BRANCH)
    wl = wl[:, np.array(_pair_head_order())].reshape(d, 2, N_HEADS // 2, N_BRANCH).transpose(0, 1, 3, 2)
    wl = jnp.pad(wl.reshape(d, 2, N_BRANCH * N_HEADS // 2), ((0, 0), (0, 0), (0, LANES - N_BRANCH * N_HEADS // 2)))
    wl = wl.reshape(d, 2 * LANES)
    o3 = o2 + N_BRANCH * N_HEADS
    wgab = w_in[:, o3:o3 + 2 * d]
    return [a.astype(BF16) for a in (wc, wq, wkv, wl, wgab)]


def kernel(x_prompt, x_sample, cache_cmp_kv, cache_slc_kv, cache_win_kv, state_conv_mix, state_conv_ffn,
           page_table, norm_mix_g, w_in, conv_mix_w, w_conv_out, cmp_pe, cmp_w1, cmp_b1, cmp_w2, cmp_b2,
           w_attn_out, w_out, norm_ffn_g, w_ff_in, ff_conv_w, w_ff_down, norm_final_g):
    depth = w_in.shape[0]
    assert depth == 1, "single-layer step"
    b, s, d = x_prompt.shape
    db, t, _ = x_sample.shape
    page_rows = cache_cmp_kv.shape[2]
    past_len = page_table.shape[1] * page_rows
    assert t == 4 and s % K_TILE == 0 and s % CMP_ROWS == 0 and s >= WINDOW + Q_TILE
    assert past_len % (PAGES_PER_STEP * page_rows) == 0 and cache_win_kv.shape[2] == WINDOW
    assert N_SEL <= s // SLC_LEN <= LANES, "prompt selection blocks fit one lane block"
    dff = ff_conv_w.shape[2]
    l = 0

    order = np.array(_pair_head_order())
    front_w = _split_w_in(w_in[l], d) + [w_conv_out[l].astype(BF16)]
    post_w = [w_attn_out[l].reshape(N_HEADS, HEAD_DIM, d)[order].reshape(N_HEADS * HEAD_DIM, d).astype(BF16),
              w_out[l].astype(BF16), norm_ffn_g[l].reshape(1, d), w_ff_in[l].astype(BF16), ff_conv_w[l],
              w_ff_down[l].astype(BF16), norm_final_g.reshape(1, d)]
    g_mix = norm_mix_g[l].reshape(1, d)
    cw = _compress_weights(cmp_pe[l], cmp_w1[l], cmp_b1[l], cmp_w2[l], cmp_b2[l])
    econst = _block_onehot(s)
    econst_t = _block_onehot(PAGES_PER_STEP * page_rows, transposed=True)

    xp = x_prompt.reshape(b * s, d)
    ma, sgb, qt, cmp_p, slc_p, win_p, kb, vt, gt, pst = _front(
        xp, g_mix, _rope_tables(jnp.arange(s, dtype=jnp.int32)), front_w, conv_mix_w[l], None, s)
    kcvc, vct = _compress_prompt(cmp_p, cw, b)
    ovlt_p = jnp.asarray(_overlap_matrix(s // CMP_STRIDE, LANES, np.arange(LANES)).T)
    o = _attend_prompt(qt, gt, kcvc, vct, kb, vt, econst, ovlt_p, b)
    y_p, ast = _post(xp, o, ma, sgb, post_w, None, s)
    kv_shape = (2, N_KV_HEADS, HEAD_DIM)
    out_prompt = (
        y_p.reshape(b, s, d),
        cmp_p.reshape((1, b, s) + kv_shape), slc_p.reshape((1, b, s) + kv_shape),
        win_p.reshape((1, b, WINDOW) + kv_shape),
        pst[None, :, SUBLANES - (CONV_W - 1):], ast[None, :, SUBLANES - (CONV_W - 1):])

    r8 = SAMPLE_ROWS
    pad_rows = lambda a, lo: jnp.pad(a, ((0, 0), (lo, r8 - lo - a.shape[1]), (0, 0))).reshape(db * r8, a.shape[2])
    xs = pad_rows(x_sample, r8 - t)
    st_mix = pad_rows(state_conv_mix[l], r8 - t - (CONV_W - 1))
    st_ffn = pad_rows(state_conv_ffn[l], r8 - t - (CONV_W - 1))
    pos_s = past_len - (r8 - t) + jnp.arange(r8, dtype=jnp.int32)
    tabs_s = tuple(jnp.tile(a, (db, 1)) for a in _rope_tables(pos_s))
    ma_s, sgb_s, q_s, cmp_s, slc_s, win_s, gates_s, p_s = _front(
        xs, g_mix, tabs_s, front_w, conv_mix_w[l], st_mix, db * r8)
    kcvc_s = _compress_sample(page_table, _feature_major_pages(cache_cmp_kv[l]), cw)
    keys_per_step = PAGES_PER_STEP * page_rows
    per = keys_per_step // SLC_LEN
    n_slots = past_len // keys_per_step + 1
    lane = np.arange(n_slots * LANES)
    col_block = np.where(lane % LANES < per, (lane // LANES) * per + lane % LANES, -1)
    col_block = np.where(col_block <= past_len // SLC_LEN, col_block, -1)
    ovl_s = jnp.asarray(_overlap_matrix(past_len // CMP_STRIDE, n_slots * LANES, col_block))
    o_s = _attend_sample(page_table, q_s, gates_s, kcvc_s, _feature_major_pages(cache_slc_kv[l]), slc_s,
                         _feature_major_pages(cache_win_kv[l]), win_s, econst_t, ovl_s, past_len)
    y_s, a_s = _post(xs, o_s, ma_s, sgb_s, post_w, st_ffn, db * r8)
    tok = lambda a: a.reshape(db, r8, -1)[:, r8 - t:]
    win_new = jnp.concatenate([cache_win_kv[l][:, t:], tok(win_s).reshape((db, t) + kv_shape)], axis=1)
    out_sample = (
        tok(y_s),
        tok(cmp_s).reshape((1, db, t) + kv_shape), tok(slc_s).reshape((1, db, t) + kv_shape),
        win_new.reshape((1, db, WINDOW) + kv_shape),
        p_s.reshape(db, r8, d)[None, :, r8 - (CONV_W - 1):], a_s.reshape(db, r8, dff)[None, :, r8 - (CONV_W - 1):])

    return (out_prompt[0], out_sample[0], out_prompt[1], out_prompt[2], out_prompt[3], out_prompt[4],
            out_prompt[5], out_sample[1], out_sample[2], out_sample[3], out_sample[4], out_sample[5])
```

```python
import functools

import numpy as np
import jax
import jax.numpy as jnp
from jax import lax
from jax.experimental import pallas as pl
from jax.experimental.pallas import tpu as pltpu

F32 = jnp.float32
BF16 = jnp.bfloat16

N_HEADS = 16
HEAD_DIM = 64
N_KV_HEADS = 4
N_BRANCH = 3
ROT_DIM = 16
ROPE_THETA = 500000.0
CMP_LEN = 32
CMP_STRIDE = 16
SLC_LEN = 64
N_SEL = 16
N_LOCAL = 2
WINDOW = 512
CONV_W = 3
NORM_EPS = 1e-6
FORCED_SCORE = 1e9

LANES = 128
SUBLANES = 8
KV_W = 2 * N_KV_HEADS * HEAD_DIM
MASK_BIG = 2.0 ** 100
LOG2_E = 1.4426950408889634
VMEM_LIMIT = 56 * 1024 * 1024

ROW_TILE = 256
FFN_CHUNKS = 2
CONV_CHUNKS = 2
CMP_ROWS = 4096
Q_TILE = 128
K_TILE = 512
SUM_ROWS = 16
SAMPLE_ROWS = 8
PAGES_PER_STEP = 32


def _cparams(sem):
    return pltpu.CompilerParams(dimension_semantics=sem, vmem_limit_bytes=VMEM_LIMIT)


def _const_spec(shape):
    nd = len(shape)
    return pl.BlockSpec(shape, lambda *_: (0,) * nd, pipeline_mode=pl.Buffered(1))


def _dot(a, b):
    return jnp.dot(a, b, preferred_element_type=F32)


def _dot_nt(a, b):
    return lax.dot_general(a, b, (((1,), (1,)), ((), ())), preferred_element_type=F32)


def _rms(x, g):
    y = x * lax.rsqrt(jnp.mean(x * x, axis=-1, keepdims=True) + NORM_EPS)
    return y * g


def _rope_block(x, cos, sa, sb):
    return x * cos + pltpu.roll(x, 8, 1) * sa + pltpu.roll(x, LANES - 8, 1) * sb


def _shifted_conv(buf, p, w_ref, rows):
    buf[SUBLANES:SUBLANES + rows, :] = p
    p1 = buf[SUBLANES - 1:SUBLANES - 1 + rows, :]
    p2 = buf[SUBLANES - 2:SUBLANES - 2 + rows, :]
    return p2 * w_ref[0:1, :] + p1 * w_ref[1:2, :] + p * w_ref[2:3, :]


def _front_kernel(sample, tiles_per_seq, *refs):
    (x_ref, g_ref, cos_ref, sa_ref, sb_ref, wc_ref, wq_ref, wkv_ref, wgl_ref, wgab_ref, cw_ref, wco_ref) = refs[:12]
    if sample:
        st_ref, ma_ref, sgb_ref, q_ref, cmp_ref, slc_ref, win_ref, gl_ref, pst_ref, pbuf = refs[12:]
    else:
        ma_ref, sgb_ref, qt_ref, cmp_ref, slc_ref, win_ref, kb_ref, vt_ref, glt_ref, pst_ref, pbuf = refs[12:]
    rows, d = x_ref.shape
    u = _rms(x_ref[...], g_ref[...]).astype(BF16)

    if sample:
        row = lax.broadcasted_iota(jnp.int32, (rows, 1), 0) % SAMPLE_ROWS
        is_state = (row >= SAMPLE_ROWS - 4 - (CONV_W - 1)) & (row < SAMPLE_ROWS - 4)
        pbuf[0:SUBLANES, :] = jnp.zeros((SUBLANES, d), F32)
    else:
        @pl.when(pl.program_id(0) % tiles_per_seq == 0)
        def _():
            pbuf[0:SUBLANES, :] = jnp.zeros((SUBLANES, d), F32)
    ya = None
    for c in range(CONV_CHUNKS):
        lo, hi = c * d // CONV_CHUNKS, (c + 1) * d // CONV_CHUNKS
        p = _dot(u, wc_ref[:, d + lo:d + hi]) * _dot(u, wc_ref[:, lo:hi])
        if sample:
            p = jnp.where(is_state, st_ref[:, lo:hi], p)
        buf = pbuf.at[:, lo:hi]
        yc = _shifted_conv(buf, p, cw_ref.at[:, lo:hi], rows)
        tail = buf[rows:rows + SUBLANES, :]
        if sample:
            pst_ref[:, lo:hi] = p
        else:
            pst_ref[0, :, lo:hi] = tail
            buf[0:SUBLANES, :] = tail
        part = _dot((_dot(u, wc_ref[:, 2 * d + lo:2 * d + hi]) * yc).astype(BF16), wco_ref[lo:hi, :])
        ya = part if ya is None else ya + part

    zg = _dot(u, wgab_ref[...])
    ma_ref[...] = jax.nn.sigmoid(zg[:, 0:d]) * ya
    sgb_ref[...] = jax.nn.sigmoid(zg[:, d:2 * d])

    cos, sa, sb = cos_ref[...], sa_ref[...], sb_ref[...]
    zq = _dot(u, wq_ref[...])
    scale = HEAD_DIM ** -0.5 * LOG2_E
    for j in range(zq.shape[1] // LANES):
        blk = _rope_block(zq[:, j * LANES:(j + 1) * LANES], cos, sa, sb) * scale
        if sample:
            q_ref[:, j * LANES:(j + 1) * LANES] = blk.astype(BF16)
        else:
            qt_ref[j * LANES:(j + 1) * LANES, :] = blk.T.astype(BF16)

    zkv = _dot(u, wkv_ref[...])
    per_branch = KV_W // LANES
    half = per_branch // 2
    outs = (cmp_ref, slc_ref, win_ref)
    for j in range(zkv.shape[1] // LANES):
        br, jj = divmod(j, per_branch)
        blk = zkv[:, j * LANES:(j + 1) * LANES]
        if jj < half:
            blk = _rope_block(blk, cos, sa, sb)
        outs[br][:, jj * LANES:(jj + 1) * LANES] = blk
        if br > 0 and not sample:
            c = (br - 1) * half + jj % half
            if jj < half:
                kb_ref[:, c * LANES:(c + 1) * LANES] = blk.astype(BF16)
            else:
                vt_ref[c * LANES:(c + 1) * LANES, :] = blk.T.astype(BF16)

    gl = jax.nn.sigmoid(_dot(u, wgl_ref[...]))
    if sample:
        gl_ref[...] = gl
    else:
        for j in range(gl.shape[1] // LANES):
            glt_ref[j * LANES:(j + 1) * LANES, :] = gl[:, j * LANES:(j + 1) * LANES].T


def _front(x, g, tabs, w, conv_w, state, seq_rows):
    n, d = x.shape
    sample = state is not None
    tm = n if sample else ROW_TILE
    tiles_per_seq = seq_rows // tm
    nt = n // tm
    n_seq = n // seq_rows
    qw = N_HEADS * HEAD_DIM
    row = lambda w_: pl.BlockSpec((tm, w_), lambda i: (i, 0))
    col = lambda h_: pl.BlockSpec((h_, tm), lambda i: (0, i))
    tab = pl.BlockSpec((tm, LANES), lambda i: (i % tiles_per_seq, 0))
    args = [x, g, *tabs, *w[:5], conv_w, w[5]]
    in_specs = [row(d), _const_spec((1, d)), tab, tab, tab] + [_const_spec(a.shape) for a in w[:5]] + [
        _const_spec(conv_w.shape), _const_spec(w[5].shape)]
    f32_rows = lambda w_: jax.ShapeDtypeStruct((n, w_), F32)
    if sample:
        args.append(state)
        in_specs.append(row(d))
        out_shape = [f32_rows(d), f32_rows(d), jax.ShapeDtypeStruct((n, qw), BF16), f32_rows(KV_W), f32_rows(KV_W),
                     f32_rows(KV_W), f32_rows(2 * LANES), f32_rows(d)]
        out_specs = [row(d), row(d), row(qw), row(KV_W), row(KV_W), row(KV_W), row(2 * LANES), row(d)]
    else:
        tail_tiles = WINDOW // tm
        win_tail = pl.BlockSpec((tm, KV_W), lambda i: (
            (i // tiles_per_seq) * tail_tiles + jnp.maximum(i % tiles_per_seq - (tiles_per_seq - tail_tiles), 0), 0))
        out_shape = [f32_rows(d), f32_rows(d), jax.ShapeDtypeStruct((qw, n), BF16), f32_rows(KV_W), f32_rows(KV_W),
                     jax.ShapeDtypeStruct((n_seq * WINDOW, KV_W), F32), jax.ShapeDtypeStruct((n, KV_W), BF16),
                     jax.ShapeDtypeStruct((KV_W, n), BF16), jax.ShapeDtypeStruct((2 * LANES, n), F32),
                     jax.ShapeDtypeStruct((n_seq, SUBLANES, d), F32)]
        out_specs = [row(d), row(d), col(qw), row(KV_W), row(KV_W), win_tail, row(KV_W), col(KV_W),
                     col(2 * LANES), pl.BlockSpec((1, SUBLANES, d), lambda i: (i // tiles_per_seq, 0, 0))]
    return pl.pallas_call(
        functools.partial(_front_kernel, sample, tiles_per_seq),
        grid=(nt,), in_specs=in_specs, out_specs=out_specs, out_shape=out_shape,
        scratch_shapes=[pltpu.VMEM((tm + SUBLANES, d), F32)],
        compiler_params=_cparams(("arbitrary",)), name="front_sample" if sample else "front_prompt",
    )(*args)


def _cmp_bias_kernel(pe_ref, w1_ref, b1_ref, o_ref):
    for kv in range(2):
        a = _dot(pe_ref[kv], w1_ref[kv])
        o_ref[kv] = jnp.broadcast_to(b1_ref[kv] + a[0:1, 0:LANES] + a[1:2, LANES:2 * LANES], (SUBLANES, LANES))


def _compress_tile(lhs_of, w1_ref, hb_ref, w2_ref, b2_ref, carry_ref, sh_ref, out_ref, vt_ref=None):
    nck = out_ref.shape[0]
    for j in range(KV_W // LANES):
        kv = j // 2
        a = _dot(lhs_of(j), w1_ref[kv])
        sh_ref[SUBLANES:SUBLANES + nck, :] = a[:, 0:LANES]
        sh_ref[0:SUBLANES, :] = carry_ref[j]
        hid = sh_ref[SUBLANES - 1:SUBLANES - 1 + nck, :] + a[:, LANES:2 * LANES] + hb_ref[kv][0:1, :]
        carry_ref[j] = sh_ref[nck:nck + SUBLANES, :]
        o = _dot(jax.nn.silu(hid).astype(BF16), w2_ref[kv]) + b2_ref[kv]
        out_ref[:, j * LANES:(j + 1) * LANES] = o.astype(out_ref.dtype)
        if vt_ref is not None and kv == 1:
            vt_ref[(j - 2) * LANES:(j - 1) * LANES, :] = o.T.astype(vt_ref.dtype)


def _compress_prompt_kernel(r0, r1, r2, r3, w1_ref, hb_ref, w2_ref, b2_ref, out_ref, vt_ref, carry_ref, sh_ref):
    @pl.when(pl.program_id(1) == 0)
    def _():
        carry_ref[...] = jnp.zeros(carry_ref.shape, F32)
    planes = (r0, r1, r2, r3)
    nck = out_ref.shape[1]

    def lhs_of(j):
        return jnp.concatenate([planes[j][pl.ds(s, nck, stride=CMP_STRIDE), :].astype(BF16)
                                for s in range(CMP_STRIDE)], axis=1)

    _compress_tile(lhs_of, w1_ref, hb_ref, w2_ref, b2_ref, carry_ref, sh_ref, out_ref.at[0], vt_ref.at[0])


def _page_copies(pt_ref, cache_ref, buf_ref, sem_ref, slot, b, grp, n_pages):
    page_rows = cache_ref.shape[2]
    return [pltpu.make_async_copy(cache_ref.at[pt_ref[b, grp * n_pages + p]],
                                  buf_ref.at[slot, :, pl.ds(p * page_rows, page_rows)], sem_ref.at[slot])
            for p in range(n_pages)]


def _gather_step(pt_ref, cache_ref, buf_ref, sem_ref, n_pages):
    b, g = pl.program_id(0), pl.program_id(1)
    nb, ng = pl.num_programs(0), pl.num_programs(1)
    step = b * ng + g
    slot = step % 2

    @pl.when(step == 0)
    def _():
        for c in _page_copies(pt_ref, cache_ref, buf_ref, sem_ref, 0, 0, 0, n_pages):
            c.start()

    @pl.when(step + 1 < nb * ng)
    def _():
        nxt = step + 1
        for c in _page_copies(pt_ref, cache_ref, buf_ref, sem_ref, 1 - slot, nxt // ng, nxt % ng, n_pages):
            c.start()

    for c in _page_copies(pt_ref, cache_ref, buf_ref, sem_ref, slot, b, g, n_pages):
        c.wait()
    return slot


def _compress_sample_kernel(pt_ref, cache_ref, perm_ref, w1_ref, hb_ref, w2_ref, b2_ref, out_ref,
                            buf_ref, sem_ref, carry_ref, sh_ref):
    n_pages = buf_ref.shape[2] // cache_ref.shape[2]
    slot = _gather_step(pt_ref, cache_ref, buf_ref, sem_ref, n_pages)

    @pl.when(pl.program_id(1) == 0)
    def _():
        carry_ref[...] = jnp.zeros(carry_ref.shape, F32)

    span = perm_ref.shape[0]
    ck = span // CMP_STRIDE

    def permuted(half, g):
        x = buf_ref[slot, half * 2 * LANES:(half + 1) * 2 * LANES, g * span:(g + 1) * span]
        return _dot_nt(perm_ref[...], x.astype(BF16)).astype(BF16)

    tiles = [[permuted(half, g) for g in range(buf_ref.shape[2] // span)] for half in range(2)]

    def lhs_of(j):
        half, jj = divmod(j, 2)
        return jnp.concatenate(
            [jnp.concatenate([tl[s * ck:(s + 1) * ck, jj * LANES:(jj + 1) * LANES] for tl in tiles[half]], axis=0)
             for s in range(CMP_STRIDE)], axis=1)

    _compress_tile(lhs_of, w1_ref, hb_ref, w2_ref, b2_ref, carry_ref, sh_ref, out_ref.at[0])


def _compress_weights(cmp_pe, cmp_w1, cmp_b1, cmp_w2, cmp_b2):
    r = CMP_LEN // CMP_STRIDE
    eye2 = jnp.eye(2, dtype=F32)
    w1 = cmp_w1.reshape(2, r, CMP_STRIDE, HEAD_DIM, HEAD_DIM)
    w1bd = jnp.einsum("krsde,hg->kshdrge", w1, eye2).reshape(2, CMP_STRIDE * LANES, r * LANES).astype(BF16)
    w2bd = jnp.einsum("kde,hg->khdge", cmp_w2, eye2).reshape(2, LANES, LANES).astype(BF16)
    pe = cmp_pe.reshape(2, r, CMP_STRIDE, 1, HEAD_DIM)
    pe = jnp.broadcast_to(pe, (2, r, CMP_STRIDE, 2, HEAD_DIM)).reshape(2, r, CMP_STRIDE * LANES)
    pe = jnp.pad(pe, ((0, 0), (0, SUBLANES - r), (0, 0))).astype(BF16)
    b1 = jnp.tile(cmp_b1, (1, 2)).reshape(2, 1, LANES)
    b2 = jnp.tile(cmp_b2, (1, 2)).reshape(2, 1, LANES)
    hb = pl.pallas_call(
        _cmp_bias_kernel, out_shape=jax.ShapeDtypeStruct((2, SUBLANES, LANES), F32), name="compress_bias",
    )(pe, w1bd, b1)
    return w1bd, hb, w2bd, b2


def _compress_prompt(cmp_rows, cw, n_seq):
    n = cmp_rows.shape[0]
    s = n // n_seq
    nck = CMP_ROWS // CMP_STRIDE
    tiles = s // CMP_ROWS
    w1bd, hb, w2bd, b2 = cw
    return pl.pallas_call(
        _compress_prompt_kernel,
        grid=(n_seq, tiles),
        in_specs=[pl.BlockSpec((CMP_ROWS, LANES), functools.partial(lambda j, b, t: (b * tiles + t, j), j))
                  for j in range(KV_W // LANES)] + [
                  _const_spec(w1bd.shape), _const_spec(hb.shape), _const_spec(w2bd.shape), _const_spec(b2.shape)],
        out_specs=[pl.BlockSpec((1, nck, KV_W), lambda b, t: (b, t, 0)),
                   pl.BlockSpec((1, KV_W // 2, nck), lambda b, t: (b, 0, t))],
        out_shape=[jax.ShapeDtypeStruct((n_seq, s // CMP_STRIDE, KV_W), BF16),
                   jax.ShapeDtypeStruct((n_seq, KV_W // 2, s // CMP_STRIDE), BF16)],
        scratch_shapes=[pltpu.VMEM((KV_W // LANES, SUBLANES, LANES), F32),
                        pltpu.VMEM((nck + SUBLANES, LANES), F32)],
        compiler_params=_cparams(("arbitrary", "arbitrary")), name="compress_prompt",
    )(cmp_rows, cmp_rows, cmp_rows, cmp_rows, w1bd, hb, w2bd, b2)


def _feature_major_pages(cache):
    return cache.transpose(0, 2, 3, 4, 1).reshape(cache.shape[0], KV_W, cache.shape[1])


def _compress_sample(page_table, cache_t, cw):
    db, n_pages = page_table.shape
    page_rows = cache_t.shape[2]
    pages_per = CMP_ROWS // page_rows
    groups = n_pages // pages_per
    nck = CMP_ROWS // CMP_STRIDE
    w1bd, hb, w2bd, b2 = cw
    span = 2 * page_rows
    ck = span // CMP_STRIDE
    row = np.arange(span)
    s_of, c_of = row // ck, row % ck
    per_page = page_rows // CMP_STRIDE
    tok = (c_of // per_page) * page_rows + CMP_STRIDE * (c_of % per_page) + s_of
    perm = jnp.asarray(tok[:, None] == np.arange(span)[None, :], dtype=BF16)
    cspec = lambda a: pl.BlockSpec(a.shape, lambda b, g, pt: (0,) * a.ndim, pipeline_mode=pl.Buffered(1))
    grid_spec = pltpu.PrefetchScalarGridSpec(
        num_scalar_prefetch=1, grid=(db, groups),
        in_specs=[pl.BlockSpec(memory_space=pl.ANY), cspec(perm), cspec(w1bd), cspec(hb), cspec(w2bd), cspec(b2)],
        out_specs=pl.BlockSpec((1, nck, KV_W), lambda b, g, pt: (b, g, 0)),
        scratch_shapes=[pltpu.VMEM((2, KV_W, CMP_ROWS), F32), pltpu.SemaphoreType.DMA((2,)),
                        pltpu.VMEM((KV_W // LANES, SUBLANES, LANES), F32),
                        pltpu.VMEM((nck + SUBLANES, LANES), F32)])
    return pl.pallas_call(
        _compress_sample_kernel, grid_spec=grid_spec,
        out_shape=jax.ShapeDtypeStruct((db, n_pages * page_rows // CMP_STRIDE, KV_W), BF16),
        compiler_params=_cparams(("arbitrary", "arbitrary")), name="compress_sample",
    )(page_table, cache_t, perm, w1bd, hb, w2bd, b2)


def _masked_softmax(s, mask):
    s = jnp.where(mask, s, -jnp.inf)
    m = jnp.max(s, axis=-1, keepdims=True)
    m = jnp.where(m == -jnp.inf, 0.0, m)
    e = jnp.where(mask, jnp.exp2(s - m), 0.0)
    return e / jnp.maximum(jnp.sum(e, axis=-1, keepdims=True), 1e-30)


def _select_blocks(imp, blk, t_pos, axis):
    cur = t_pos // SLC_LEN
    valid = blk <= cur
    forced = (blk == 0) | (valid & (blk > cur - N_LOCAL))
    cand = jnp.where(forced, -jnp.inf, jnp.where(valid, imp, -FORCED_SCORE))
    cand = jnp.where(blk >= 0, cand, -jnp.inf)
    length = imp.shape[axis]
    idx = lax.broadcasted_iota(jnp.int32, imp.shape, axis).astype(F32)

    def pick(_, carry):
        cand, notsel = carry
        m = jnp.max(cand, axis=axis, keepdims=True)
        first = jnp.min(jnp.where(cand == m, idx, float(length)), axis=axis, keepdims=True)
        hit = idx == first
        return jnp.where(hit, -jnp.inf, cand), jnp.where(hit, 0.0, notsel)

    _, notsel = lax.fori_loop(0, N_SEL - (N_LOCAL + 1), pick, (cand, jnp.where(forced, 0.0, 1.0)), unroll=True)
    return notsel


def _flash_update(carry, s, v, v_transposed=False):
    m, l, acc = carry
    m_new = jnp.maximum(m, jnp.max(s, axis=-1, keepdims=True))
    alpha = jnp.exp2(m - m_new)
    p = jnp.exp2(s - m_new)
    l = alpha * l + jnp.sum(p, axis=-1, keepdims=True)
    pv = _dot_nt(p.astype(v.dtype), v) if v_transposed else _dot(p.astype(v.dtype), v)
    return m_new, l, alpha * acc + pv


def _half_mask(shape):
    return lax.broadcasted_iota(jnp.int32, shape, len(shape) - 1) < HEAD_DIM


def _stack_heads(q_blk, nr):
    first = _half_mask((q_blk.shape[0], LANES))
    zero = jnp.zeros((), q_blk.dtype)
    parts = []
    for gl in range(2):
        for r in range(nr):
            blk = q_blk[:, r * LANES:(r + 1) * LANES]
            parts.append(jnp.where(first if gl == 0 else ~first, blk, zero))
    return jnp.concatenate(parts, axis=0)


def _merge_pair(o, nr, t):
    first = _half_mask((t, LANES))
    return [jnp.where(first, o[r * t:(r + 1) * t], o[(nr + r) * t:(nr + r + 1) * t]) for r in range(nr)]


def _gate_vec(g_ref, br, r, t):
    c = br * 2 * (N_HEADS // N_KV_HEADS) + r * 2
    return jnp.where(_half_mask((t, LANES)), g_ref[:, c:c + 1], g_ref[:, c + 1:c + 2])


def _biased_exp0(s, bias):
    s = s + jnp.concatenate([bias] * (s.shape[1] // bias.shape[1]), axis=1)
    m = jnp.max(s, axis=0, keepdims=True)
    m = jnp.where(m < -0.5 * MASK_BIG, 0.0, m)
    return jnp.exp2(s - m)


def _recip_sum(total):
    return 1.0 / jnp.maximum(total, 1e-30)


def _pv_own_head(vt, p, with_sum=False):
    half = p.shape[1] // 2
    heads = [vt[0:HEAD_DIM, :], vt[HEAD_DIM:2 * HEAD_DIM, :]]
    if with_sum:
        ones = jnp.ones((SUM_ROWS, vt.shape[1]), vt.dtype)
        heads = [jnp.concatenate([h, ones], axis=0) for h in heads]
    return jnp.concatenate([_dot(heads[0], p[:, 0:half]), _dot(heads[1], p[:, half:])], axis=1)


def _flash_update0(carry, s, vt):
    m, acc = carry
    m_new = jnp.maximum(m, jnp.max(s, axis=0, keepdims=True))
    alpha = jnp.exp2(m - m_new)
    p = jnp.exp2(s - m_new)
    return m_new, alpha * acc + _pv_own_head(vt, p.astype(vt.dtype), with_sum=True)


def _attend_prompt_kernel(qt_ref, gt_ref, kc_ref, vct_ref, kb_ref, vt_ref, e_ref, ovlt_ref,
                          cbias_ref, wbias_ref, fbias_ref,
                          o_ref, qa_ref, sa_ref, sb_ref, m_ref, acc_ref, oc_ref, ow_ref):
    i = pl.program_id(1)
    nr = N_HEADS // N_KV_HEADS
    t = Q_TILE
    rows = 2 * nr * t
    n_pairs = N_KV_HEADS // 2
    t0 = i * t
    tq = t0 + lax.broadcasted_iota(jnp.int32, (1, t), 1)
    first = lax.broadcasted_iota(jnp.int32, (LANES, t), 0) < HEAD_DIM
    zero = jnp.zeros((), BF16)
    ncmp = kc_ref.shape[1]
    pair_lanes = lambda k: slice(k * LANES, (k + 1) * LANES)
    win_lanes = lambda k: slice((n_pairs + k) * LANES, (n_pairs + k + 1) * LANES)

    qts = [jnp.concatenate([jnp.where(first if gl == 0 else ~first,
                                      qt_ref[(k * nr + r) * LANES:(k * nr + r + 1) * LANES, :], zero)
                            for gl in range(2) for r in range(nr)], axis=1) for k in range(n_pairs)]

    c_off = pl.multiple_of(ncmp - i * (t // CMP_STRIDE), t // CMP_STRIDE)
    cbias = cbias_ref[pl.ds(c_off, ncmp), :]
    not_junk = lax.broadcasted_iota(jnp.int32, (SUBLANES, 1), 0) >= 1
    psum = []
    for k in range(n_pairs):
        s_c = _dot(kc_ref[0, :, pair_lanes(k)], qts[k])
        head = jnp.where(not_junk, s_c[0:SUBLANES], -MASK_BIG)
        e_c = _biased_exp0(jnp.concatenate([head, s_c[SUBLANES:]], axis=0), cbias)
        p_c = e_c * _recip_sum(jnp.sum(e_c, axis=0, keepdims=True))
        oc_ref[k] = _pv_own_head(vct_ref[0, pair_lanes(k), :], p_c.astype(BF16))
        for gl in range(2):
            acc = p_c[:, gl * nr * t:gl * nr * t + t]
            for r in range(1, nr):
                acc = acc + p_c[:, (gl * nr + r) * t:(gl * nr + r + 1) * t]
            psum.append(acc)
    imp = jnp.dot(ovlt_ref[...], jnp.concatenate(psum, axis=1), preferred_element_type=F32,
                  precision=lax.Precision.HIGHEST)

    wlen = WINDOW + t
    ws = pl.multiple_of(jnp.maximum(t0 - WINDOW, 0), t)
    anchor = (imp[0:1, 0:1] * 0.0).astype(BF16)
    wbias = wbias_ref[jnp.minimum(i, WINDOW // t)]
    for k in range(n_pairs):
        e_w = _biased_exp0(_dot(kb_ref[pl.ds(ws, wlen), win_lanes(k)] + anchor, qts[k]), wbias)
        pv = _pv_own_head(vt_ref[win_lanes(k), pl.ds(ws, wlen)], e_w.astype(BF16), with_sum=True)
        ow_ref[k] = pv[0:HEAD_DIM] * _recip_sum(pv[HEAD_DIM:HEAD_DIM + 1])

    nslc = ovlt_ref.shape[0]
    blk = lax.broadcasted_iota(jnp.int32, (nslc, 1), 0)
    ns = _select_blocks(imp, blk, jnp.concatenate([tq] * N_KV_HEADS, axis=1), axis=0).astype(BF16)
    for k in range(n_pairs):
        flags = [ns[:, (2 * k + gl) * t:(2 * k + gl + 1) * t] for gl in range(2)]
        qa_ref[k] = jnp.concatenate([qts[k], jnp.concatenate([flags[0]] * nr + [flags[1]] * nr, axis=1)], axis=0)

    def produce(s_ref, j):
        off = pl.multiple_of(j * K_TILE, K_TILE)
        for k in range(n_pairs):
            ka = jnp.concatenate([kb_ref[pl.ds(off, K_TILE), pair_lanes(k)], e_ref[pl.ds(off, K_TILE), :]], axis=1)
            s_ref[k] = _dot(ka, qa_ref[k])

    def consume(s_ref, j, causal):
        off = pl.multiple_of(j * K_TILE, K_TILE)
        for k in range(n_pairs):
            s = s_ref[k]
            if causal:
                fb = fbias_ref[(t0 - off) // t]
                s = s + jnp.concatenate([fb] * (rows // t), axis=1)
            m_ref[k], acc_ref[k] = _flash_update0(
                (m_ref[k], acc_ref[k]), s, vt_ref[pair_lanes(k), pl.ds(off, K_TILE)])

    m_ref[...] = jnp.full(m_ref.shape, -jnp.inf, F32)
    acc_ref[...] = jnp.zeros(acc_ref.shape, F32)
    last = (t0 + t - 1) // K_TILE
    produce(sa_ref, 0)

    def pair(jj, _):
        j = 2 * jj
        produce(sb_ref, j + 1)
        consume(sa_ref, j, False)
        produce(sa_ref, j + 2)
        consume(sb_ref, j + 1, False)
        return 0

    lax.fori_loop(0, last // 2, pair, 0)

    @pl.when(last % 2 == 1)
    def _():
        produce(sb_ref, last)
        consume(sa_ref, last - 1, False)
        consume(sb_ref, last, True)

    @pl.when(last % 2 == 0)
    def _():
        consume(sa_ref, last, True)

    for k in range(n_pairs):
        o_s = acc_ref[k, 0:HEAD_DIM, :] * (1.0 / acc_ref[k, HEAD_DIM:HEAD_DIM + 1, :])
        branches = (oc_ref[k], o_s, ow_ref[k])
        for r in range(nr):
            c0, c1 = r * t, (nr + r) * t
            o = None
            for br, ob in enumerate(branches):
                c = k * LANES + br * 2 * nr + r * 2
                gate = jnp.where(first, gt_ref[c:c + 1, :], gt_ref[c + 1:c + 2, :])
                term = gate * jnp.concatenate([ob[:, c0:c0 + t], ob[:, c1:c1 + t]], axis=0)
                o = term if o is None else o + term
            o_ref[:, (k * nr + r) * LANES:(k * nr + r + 1) * LANES] = o.T.astype(o_ref.dtype)


def _mask_bias(visible):
    return jnp.asarray(np.where(visible, 0.0, -MASK_BIG), dtype=F32)


def _attend_prompt(qt, gt, kcvc, vct, kb, vt, econst, ovlt, n_seq):
    t = Q_TILE
    tcol = np.arange(t)[None, :]
    ncmp_ = kcvc.shape[1]
    x = np.arange(-ncmp_, ncmp_)[:, None]
    cbias = _mask_bias(x * CMP_STRIDE + (CMP_LEN - CMP_STRIDE - 1) <= tcol)
    r = np.arange(WINDOW + t)[None, :, None]
    d = (np.arange(WINDOW // t + 1) * t)[:, None, None]
    wbias = _mask_bias((r <= d + tcol[None]) & (r > d + tcol[None] - WINDOW))
    r = np.arange(K_TILE)[None, :, None]
    d = (np.arange(K_TILE // t) * t)[:, None, None]
    fbias = _mask_bias(r <= d + tcol[None])
    n = qt.shape[1]
    s = n // n_seq
    nq = s // Q_TILE
    nr = N_HEADS // N_KV_HEADS
    ncmp = kcvc.shape[1]
    qw = N_HEADS * HEAD_DIM
    rows = 2 * nr * Q_TILE
    n_pairs = N_KV_HEADS // 2
    per_seq = lambda shape, imap: pl.BlockSpec(shape, imap, pipeline_mode=pl.Buffered(1))
    return pl.pallas_call(
        _attend_prompt_kernel,
        grid=(n_seq, nq),
        in_specs=[pl.BlockSpec((qw, Q_TILE), lambda b, i: (0, b * nq + i)),
                  pl.BlockSpec((n_pairs * LANES, Q_TILE), lambda b, i: (0, b * nq + i)),
                  per_seq((1, ncmp, n_pairs * LANES), lambda b, i: (b, 0, 0)),
                  per_seq((1, n_pairs * LANES, ncmp), lambda b, i: (b, 0, 0)),
                  per_seq((s, KV_W), lambda b, i: (b, 0)), per_seq((KV_W, s), lambda b, i: (0, b)),
                  _const_spec(econst.shape), _const_spec(ovlt.shape), _const_spec(cbias.shape),
                  _const_spec(wbias.shape), _const_spec(fbias.shape)],
        out_specs=pl.BlockSpec((Q_TILE, qw), lambda b, i: (b * nq + i, 0)),
        out_shape=jax.ShapeDtypeStruct((n, qw), BF16),
        scratch_shapes=[pltpu.VMEM((n_pairs, 2 * LANES, rows), BF16), pltpu.VMEM((n_pairs, K_TILE, rows), F32),
                        pltpu.VMEM((n_pairs, K_TILE, rows), F32), pltpu.VMEM((n_pairs, 1, rows), F32),
                        pltpu.VMEM((n_pairs, HEAD_DIM + SUM_ROWS, rows), F32),
                        pltpu.VMEM((n_pairs, HEAD_DIM, rows), F32), pltpu.VMEM((n_pairs, HEAD_DIM, rows), F32)],
        compiler_params=_cparams(("arbitrary", "arbitrary")), name="attend_prompt",
    )(qt, gt, kcvc, vct, kb, vt, econst, ovlt, cbias, wbias, fbias)


def _attend_sample_kernel(past_len, pt_ref, q_ref, g_ref, kcvc_ref, cache_ref, slc_ref, wcache_ref, win_ref,
                          e_ref, ovl_ref, o_ref, buf_ref, sem_ref, ns_ref, oc_ref, m_ref, l_ref, acc_ref):
    g = pl.program_id(1)
    ng = pl.num_programs(1)
    nr = N_HEADS // N_KV_HEADS
    t = SAMPLE_ROWS
    rows = 2 * nr * t
    n_pages = buf_ref.shape[2] // cache_ref.shape[2]
    keys = buf_ref.shape[2]
    slot = _gather_step(pt_ref, cache_ref, buf_ref, sem_ref, n_pages)
    tq = past_len - (t - 4) + lax.broadcasted_iota(jnp.int32, (t, 1), 0)
    tq_rows = jnp.concatenate([tq] * (2 * nr), axis=0)
    qs = [_stack_heads(q_ref[:, k * nr * LANES:(k + 1) * nr * LANES], nr) for k in range(2)]

    @pl.when(g == 0)
    def _():
        ncmp = kcvc_ref.shape[1]
        cidx = lax.broadcasted_iota(jnp.int32, (1, ncmp), 1)
        c_mask = (cidx >= 1) & (cidx * CMP_STRIDE + (CMP_LEN - CMP_STRIDE - 1) <= tq_rows)
        width = ovl_ref.shape[1]
        lane = lax.broadcasted_iota(jnp.int32, (1, width), 1)
        per = keys // SLC_LEN
        blk = (lane // LANES) * per + lane % LANES
        real = (lane % LANES < per) & (blk <= past_len // SLC_LEN)
        psums = []
        for k in range(2):
            p_c = _masked_softmax(_dot_nt(qs[k], kcvc_ref[0, :, k * LANES:(k + 1) * LANES]), c_mask)
            oc_ref[k] = _dot(p_c.astype(BF16), kcvc_ref[0, :, (2 + k) * LANES:(3 + k) * LANES])
            for gl in range(2):
                psum = p_c[gl * nr * t:gl * nr * t + t]
                for r in range(1, nr):
                    psum = psum + p_c[(gl * nr + r) * t:(gl * nr + r + 1) * t]
                psums.append(psum)
        imp = jnp.dot(jnp.concatenate(psums, axis=0), ovl_ref[...], preferred_element_type=F32,
                      precision=lax.Precision.HIGHEST)
        ns = _select_blocks(imp, jnp.where(real, blk, -1), jnp.concatenate([tq] * 4, axis=0), axis=1)
        for k in range(2):
            for gl in range(2):
                for r in range(nr):
                    ns_ref[k, (gl * nr + r) * t:(gl * nr + r + 1) * t, :] = ns[(2 * k + gl) * t:(2 * k + gl + 1) * t]
        m_ref[...] = jnp.full(m_ref.shape, -jnp.inf, F32)
        l_ref[...] = jnp.zeros(l_ref.shape, F32)
        acc_ref[...] = jnp.zeros(acc_ref.shape, F32)

    page = buf_ref.at[slot]
    for k in range(2):
        ns = ns_ref[k, :, pl.ds(pl.multiple_of(g * LANES, LANES), LANES)]
        qa = jnp.concatenate([qs[k], ns.astype(BF16)], axis=1)
        ka = jnp.concatenate([page[k * LANES:(k + 1) * LANES, :].astype(BF16), e_ref[...]], axis=0)
        v = page[(2 + k) * LANES:(3 + k) * LANES, :].astype(BF16)
        m_ref[k], l_ref[k], acc_ref[k] = _flash_update((m_ref[k], l_ref[k], acc_ref[k]), _dot(qa, ka), v,
                                                       v_transposed=True)

    @pl.when(g == ng - 1)
    def _():
        kidx = lax.broadcasted_iota(jnp.int32, (1, t), 1)
        kpos = past_len - (t - 4) + kidx
        new_ok = (kidx >= t - 4) & (kpos <= tq_rows)
        wb = wcache_ref.shape[2]
        wpos =past_len - wb + lax.broadcasted_iota(jnp.int32, (1, wb), 1)
        w_old = (wpos <= tq_rows) & (wpos > tq_rows - WINDOW) & (wpos >= 0)
        w_new = new_ok & (kpos > tq_rows - WINDOW)
        for k in range(2):
            kcol, vcol = slice(k * LANES, (k + 1) * LANES), slice((2 + k) * LANES, (3 + k) * LANES)
            qf = qs[k].astype(F32)
            s_new = jnp.where(new_ok, _dot_nt(qf, slc_ref[:, kcol]), -MASK_BIG)
            _, l, acc = _flash_update((m_ref[k], l_ref[k], acc_ref[k]), s_new, slc_ref[:, vcol])
            o_s = acc / l
            so = jnp.where(w_old, _dot(qs[k], wcache_ref[0, kcol, :].astype(BF16)), -jnp.inf)
            sn = jnp.where(w_new, _dot_nt(qf, win_ref[:, kcol]), -jnp.inf)
            m = jnp.maximum(jnp.max(so, axis=-1, keepdims=True), jnp.max(sn, axis=-1, keepdims=True))
            m = jnp.where(m == -jnp.inf, 0.0, m)
            eo = jnp.where(w_old, jnp.exp2(so - m), 0.0)
            en = jnp.where(w_new, jnp.exp2(sn - m), 0.0)
            den = jnp.maximum(jnp.sum(eo, axis=-1, keepdims=True) + jnp.sum(en, axis=-1, keepdims=True), 1e-30)
            o_w = (_dot_nt(eo.astype(BF16), wcache_ref[0, vcol, :].astype(BF16)) + _dot(en, win_ref[:, vcol])) / den
            oc, os_, ow = _merge_pair(oc_ref[k], nr, t), _merge_pair(o_s, nr, t), _merge_pair(o_w, nr, t)
            gk = g_ref.at[:, k * LANES:(k + 1) * LANES]
            for r in range(nr):
                o = _gate_vec(gk, 0, r, t) * oc[r] + _gate_vec(gk, 1, r, t) * os_[r] + _gate_vec(gk, 2, r, t) * ow[r]
                o_ref[:, (k * nr + r) * LANES:(k * nr + r + 1) * LANES] = o.astype(o_ref.dtype)


def _attend_sample(page_table, q, gates, kcvc, cache_t, slc_new, wcache, win_new, econst_t, ovl, past_len):
    db, n_pages = page_table.shape
    page_rows = cache_t.shape[2]
    econst = econst_t
    groups = n_pages // PAGES_PER_STEP
    keys = PAGES_PER_STEP * page_rows
    nr = N_HEADS // N_KV_HEADS
    t = SAMPLE_ROWS
    rows = 2 * nr * t
    qw = N_HEADS * HEAD_DIM
    cmap = lambda nd: (lambda b, g, pt: (0,) * nd)
    grid_spec = pltpu.PrefetchScalarGridSpec(
        num_scalar_prefetch=1, grid=(db, groups),
        in_specs=[pl.BlockSpec((t, qw), lambda b, g, pt: (b, 0)),
                  pl.BlockSpec((t, 2 * LANES), lambda b, g, pt: (b, 0)),
                  pl.BlockSpec((1,) + kcvc.shape[1:], lambda b, g, pt: (b, 0, 0)),
                  pl.BlockSpec(memory_space=pl.ANY),
                  pl.BlockSpec((t, KV_W), lambda b, g, pt: (b, 0)),
                  pl.BlockSpec((1,) + wcache.shape[1:], lambda b, g, pt: (b, 0, 0)),
                  pl.BlockSpec((t, KV_W), lambda b, g, pt: (b, 0)),
                  pl.BlockSpec(econst.shape, cmap(2), pipeline_mode=pl.Buffered(1)),
                  pl.BlockSpec(ovl.shape, cmap(2), pipeline_mode=pl.Buffered(1))],
        out_specs=pl.BlockSpec((t, qw), lambda b, g, pt: (b, 0)),
        scratch_shapes=[pltpu.VMEM((2, KV_W, keys), F32), pltpu.SemaphoreType.DMA((2,)),
                        pltpu.VMEM((2, rows, ovl.shape[1]), F32), pltpu.VMEM((2, rows, LANES), F32),
                        pltpu.VMEM((2, rows, 1), F32), pltpu.VMEM((2, rows, 1), F32),
                        pltpu.VMEM((2, rows, LANES), F32)])
    return pl.pallas_call(
        functools.partial(_attend_sample_kernel, past_len), grid_spec=grid_spec,
        out_shape=jax.ShapeDtypeStruct((db * t, qw), BF16),
        compiler_params=_cparams(("arbitrary", "arbitrary")), name="attend_sample",
    )(page_table, q, gates, kcvc, cache_t, slc_new, wcache, win_new, econst, ovl)


def _post_kernel(sample, tiles_per_seq, *refs):
    if sample:
        (x_ref, o_ref, ma_ref, sgb_ref, wao_ref, wo_ref, gf_ref, wfi_ref, fcw_ref, wfd_ref, gfin_ref, st_ref,
         y_ref, ast_ref, abuf) = refs
    else:
        (x_ref, o_ref, ma_ref, sgb_ref, wao_ref, wo_ref, gf_ref, wfi_ref, fcw_ref, wfd_ref, gfin_ref,
         y_ref, ast_ref, abuf) = refs
    rows = x_ref.shape[0]
    dff = fcw_ref.shape[1]
    m = ma_ref[...] + sgb_ref[...] * _dot(o_ref[...], wao_ref[...])
    h = x_ref[...] + _dot(m.astype(BF16), wo_ref[...])
    v = _rms(h, gf_ref[...]).astype(BF16)
    if sample:
        row = lax.broadcasted_iota(jnp.int32, (rows, 1), 0) % SAMPLE_ROWS
        is_state = (row >= SAMPLE_ROWS - 4 - (CONV_W - 1)) & (row < SAMPLE_ROWS - 4)
        abuf[0:SUBLANES, :] = jnp.zeros((SUBLANES, dff), F32)
    else:
        @pl.when(pl.program_id(0) % tiles_per_seq == 0)
        def _():
            abuf[0:SUBLANES, :] = jnp.zeros((SUBLANES, dff), F32)
    bounds = [LANES * ((dff // LANES) * c // FFN_CHUNKS) for c in range(FFN_CHUNKS + 1)]
    hh = h
    for lo, hi in zip(bounds[:-1], bounds[1:]):
        a = _dot(v, wfi_ref[:, lo:hi])
        g = _dot(v, wfi_ref[:, dff + lo:dff + hi])
        if sample:
            a = jnp.where(is_state, st_ref[:, lo:hi], a)
        buf = abuf.at[:, lo:hi]
        ac = _shifted_conv(buf, a, fcw_ref.at[:, lo:hi], rows)
        tail = buf[rows:rows + SUBLANES, :]
        if sample:
            ast_ref[:, lo:hi] = a
        else:
            ast_ref[0, :, lo:hi] = tail
            buf[0:SUBLANES, :] = tail
        hh = hh + _dot((jax.nn.silu(ac) * g).astype(BF16), wfd_ref[lo:hi, :])
    y_ref[...] = _rms(hh, gfin_ref[...])


def _post(x, o, ma, sgb, w, state, seq_rows):
    n, d = x.shape
    sample = state is not None
    tm = n if sample else ROW_TILE
    tiles_per_seq = seq_rows // tm
    n_seq = n // seq_rows
    wao, wo, gf, wfi, fcw, wfd, gfin = w
    dff = fcw.shape[1]
    row = lambda w_: pl.BlockSpec((tm, w_), lambda i: (i, 0))
    args = [x, o, ma, sgb, *w]
    in_specs = [row(d), row(o.shape[1]), row(d), row(d)] + [_const_spec(a.shape) for a in w]
    if sample:
        args.append(state)
        in_specs.append(row(dff))
        ast_shape, ast_spec = jax.ShapeDtypeStruct((n, dff), F32), row(dff)
    else:
        ast_shape = jax.ShapeDtypeStruct((n_seq, SUBLANES, dff), F32)
        ast_spec = pl.BlockSpec((1, SUBLANES, dff), lambda i: (i // tiles_per_seq, 0, 0))
    return pl.pallas_call(
        functools.partial(_post_kernel, sample, tiles_per_seq),
        grid=(n // tm,), in_specs=in_specs, out_specs=[row(d), ast_spec],
        out_shape=[jax.ShapeDtypeStruct((n, d), F32), ast_shape],
        scratch_shapes=[pltpu.VMEM((tm + SUBLANES, dff), F32)],
        compiler_params=_cparams(("arbitrary",)), name="post_sample" if sample else "post_prompt",
    )(*args)


def _pair_head_order():
    nr = N_HEADS // N_KV_HEADS
    return [2 * nr * k + nr * half + r for k in range(2) for r in range(nr) for half in range(2)]


def _rope_tables(pos):
    half = ROT_DIM // 2
    inv_freq = jnp.power(ROPE_THETA, -jnp.arange(half, dtype=F32) * (2.0 / ROT_DIM))
    ang = pos.astype(F32)[:, None] * inv_freq[None, :]
    cos, sin = jnp.cos(ang), jnp.sin(ang)
    n = pos.shape[0]
    one, zero = jnp.ones((n, HEAD_DIM - ROT_DIM), F32), jnp.zeros((n, HEAD_DIM - ROT_DIM), F32)
    zh = jnp.zeros((n, half), F32)
    tabs = (jnp.concatenate([cos, cos, one], 1), jnp.concatenate([zh, sin, zero], 1),
            jnp.concatenate([-sin, zh, zero], 1))
    return tuple(jnp.tile(a, (1, LANES // HEAD_DIM)) for a in tabs)


def _overlap_matrix(n_rows, n_cols, col_block):
    c = np.arange(n_rows)[:, None] - 1
    blk = col_block[None, :]
    cs, ss = c * CMP_STRIDE, blk * SLC_LEN
    return ((c >= 0) & (blk >= 0) & (cs < ss + SLC_LEN) & (cs + CMP_LEN > ss)).astype(np.float32)


def _block_onehot(n_keys, transposed=False):
    e = (np.arange(n_keys)[:, None] // SLC_LEN == np.arange(LANES)[None, :]).astype(np.float32)
    return jnp.asarray(-MASK_BIG * (e.T if transposed else e), dtype=BF16)


def _split_w_in(w_in, d):
    conv_dim = d
    q_dim = N_HEADS * HEAD_DIM
    kv_dim = N_BRANCH * KV_W
    o0 = 3 * conv_dim
    wc = w_in[:, 0:o0]
    wq = w_in[:, o0:o0 + q_dim].reshape(d, N_HEADS, HEAD_DIM)[:, np.array(_pair_head_order())].reshape(d, q_dim)
    o1 = o0 + q_dim
    wkv = w_in[:, o1:o1 + kv_dim]
    o2 = o1 + kv_dim
    wl = w_in[:, o2:o2 + N_BRANCH * N_HEADS].reshape(d, N_HEADS, N_BRANCH)
    wl = wl[:, np.array(_pair_head_order())].reshape(d, 2, N_HEADS // 2, N_BRANCH).transpose(0, 1, 3, 2)
    wl = jnp.pad(wl.reshape(d, 2, N_BRANCH * N_HEADS // 2), ((0, 0), (0, 0), (0, LANES - N_BRANCH * N_HEADS // 2)))
    wl = wl.reshape(d, 2 * LANES)
    o3 = o2 + N_BRANCH * N_HEADS
    wgab = w_in[:, o3:o3 + 2 * d]
    return [a.astype(BF16) for a in (wc, wq, wkv, wl, wgab)]


def kernel(x_prompt, x_sample, cache_cmp_kv, cache_slc_kv, cache_win_kv, state_conv_mix, state_conv_ffn,
           page_table, norm_mix_g, w_in, conv_mix_w, w_conv_out, cmp_pe, cmp_w1, cmp_b1, cmp_w2, cmp_b2,
           w_attn_out, w_out, norm_ffn_g, w_ff_in, ff_conv_w, w_ff_down, norm_final_g):
    depth = w_in.shape[0]
    assert depth == 1, "single-layer step"
    b, s, d = x_prompt.shape
    db, t, _ = x_sample.shape
    page_rows = cache_cmp_kv.shape[2]
    past_len = page_table.shape[1] * page_rows
    assert t == 4 and s % K_TILE == 0 and s % CMP_ROWS == 0 and s >= WINDOW + Q_TILE
    assert past_len % (PAGES_PER_STEP * page_rows) == 0 and cache_win_kv.shape[2] == WINDOW
    assert N_SEL <= s // SLC_LEN <= LANES, "prompt selection blocks fit one lane block"
    dff = ff_conv_w.shape[2]
    l = 0

    order = np.array(_pair_head_order())
    front_w = _split_w_in(w_in[l], d) + [w_conv_out[l].astype(BF16)]
    post_w = [w_attn_out[l].reshape(N_HEADS, HEAD_DIM, d)[order].reshape(N_HEADS * HEAD_DIM, d).astype(BF16),
              w_out[l].astype(BF16), norm_ffn_g[l].reshape(1, d), w_ff_in[l].astype(BF16), ff_conv_w[l],
              w_ff_down[l].astype(BF16), norm_final_g.reshape(1, d)]
    g_mix = norm_mix_g[l].reshape(1, d)
    cw = _compress_weights(cmp_pe[l], cmp_w1[l], cmp_b1[l], cmp_w2[l], cmp_b2[l])
    econst = _block_onehot(s)
    econst_t = _block_onehot(PAGES_PER_STEP * page_rows, transposed=True)

    xp = x_prompt.reshape(b * s, d)
    ma, sgb, qt, cmp_p, slc_p, win_p, kb, vt, gt, pst = _front(
        xp, g_mix, _rope_tables(jnp.arange(s, dtype=jnp.int32)), front_w, conv_mix_w[l], None, s)
    kcvc, vct = _compress_prompt(cmp_p, cw, b)
    ovlt_p = jnp.asarray(_overlap_matrix(s // CMP_STRIDE, LANES, np.arange(LANES)).T)
    o = _attend_prompt(qt, gt, kcvc, vct, kb, vt, econst, ovlt_p, b)
    y_p, ast = _post(xp, o, ma, sgb, post_w, None, s)
    kv_shape = (2, N_KV_HEADS, HEAD_DIM)
    out_prompt = (
        y_p.reshape(b, s, d),
        cmp_p.reshape((1, b, s) + kv_shape), slc_p.reshape((1, b, s) + kv_shape),
        win_p.reshape((1, b, WINDOW) + kv_shape),
        pst[None, :, SUBLANES - (CONV_W - 1):], ast[None, :, SUBLANES - (CONV_W - 1):])

    r8 = SAMPLE_ROWS
    pad_rows = lambda a, lo: jnp.pad(a, ((0, 0), (lo, r8 - lo - a.shape[1]), (0, 0))).reshape(db * r8, a.shape[2])
    xs = pad_rows(x_sample, r8 - t)
    st_mix = pad_rows(state_conv_mix[l], r8 - t - (CONV_W - 1))
    st_ffn = pad_rows(state_conv_ffn[l], r8 - t - (CONV_W - 1))
    pos_s = past_len - (r8 - t) + jnp.arange(r8, dtype=jnp.int32)
    tabs_s = tuple(jnp.tile(a, (db, 1)) for a in _rope_tables(pos_s))
    ma_s, sgb_s, q_s, cmp_s, slc_s, win_s, gates_s, p_s = _front(
        xs, g_mix, tabs_s, front_w, conv_mix_w[l], st_mix, db * r8)
    kcvc_s = _compress_sample(page_table, _feature_major_pages(cache_cmp_kv[l]), cw)
    keys_per_step = PAGES_PER_STEP * page_rows
    per = keys_per_step // SLC_LEN
    n_slots = past_len // keys_per_step + 1
    lane = np.arange(n_slots * LANES)
    col_block = np.where(lane % LANES < per, (lane // LANES) * per + lane % LANES, -1)
    col_block = np.where(col_block <= past_len // SLC_LEN, col_block, -1)
    ovl_s = jnp.asarray(_overlap_matrix(past_len // CMP_STRIDE, n_slots * LANES, col_block))
    o_s = _attend_sample(page_table, q_s, gates_s, kcvc_s, _feature_major_pages(cache_slc_kv[l]), slc_s,
                         _feature_major_pages(cache_win_kv[l]), win_s, econst_t, ovl_s, past_len)
    y_s, a_s = _post(xs, o_s, ma_s, sgb_s, post_w, st_ffn, db * r8)
    tok = lambda a: a.reshape(db, r8, -1)[:, r8 - t:]
    win_new = jnp.concatenate([cache_win_kv[l][:, t:], tok(win_s).reshape((db, t) + kv_shape)], axis=1)
    out_sample = (
        tok(y_s),
        tok(cmp_s).reshape((1, db, t) + kv_shape), tok(slc_s).reshape((1, db, t) + kv_shape),
        win_new.reshape((1, db, WINDOW) + kv_shape),
        p_s.reshape(db, r8, d)[None, :, r8 - (CONV_W - 1):], a_s.reshape(db, r8, dff)[None, :, r8 - (CONV_W - 1):])

    return (out_prompt[0], out_sample[0], out_prompt[1], out_prompt[2], out_prompt[3], out_prompt[4],
            out_prompt[5], out_sample[1], out_sample[2], out_sample[3], out_sample[4], out_sample[5])
```

```python
import functools

import numpy as np
import jax
import jax.numpy as jnp
from jax import lax
from jax.experimental import pallas as pl
from jax.experimental.pallas import tpu as pltpu

F32 = jnp.float32
BF16 = jnp.bfloat16

N_HEADS = 16
HEAD_DIM = 64
N_KV_HEADS = 4
N_BRANCH = 3
ROT_DIM = 16
ROPE_THETA = 500000.0
CMP_LEN = 32
CMP_STRIDE = 16
SLC_LEN = 64
N_SEL = 16
N_LOCAL = 2
WINDOW = 512
CONV_W = 3
NORM_EPS = 1e-6
FORCED_SCORE = 1e9

LANES = 128
SUBLANES = 8
KV_W = 2 * N_KV_HEADS * HEAD_DIM
MASK_BIG = 2.0 ** 100
LOG2_E = 1.4426950408889634
VMEM_LIMIT = 56 * 1024 * 1024

ROW_TILE = 256
FFN_CHUNKS = 2
CONV_CHUNKS = 2
CMP_ROWS = 4096
Q_TILE = 128
K_TILE = 512
SUM_ROWS = 16
SAMPLE_ROWS = 8
PAGES_PER_STEP = 32


def _cparams(sem):
    return pltpu.CompilerParams(dimension_semantics=sem, vmem_limit_bytes=VMEM_LIMIT)


def _const_spec(shape):
    nd = len(shape)
    return pl.BlockSpec(shape, lambda *_: (0,) * nd, pipeline_mode=pl.Buffered(1))


def _dot(a, b):
    return jnp.dot(a, b, preferred_element_type=F32)


def _dot_nt(a, b):
    return lax.dot_general(a, b, (((1,), (1,)), ((), ())), preferred_element_type=F32)


def _rms(x, g):
    y = x * lax.rsqrt(jnp.mean(x * x, axis=-1, keepdims=True) + NORM_EPS)
    return y * g


def _rope_block(x, cos, sa, sb):
    return x * cos + pltpu.roll(x, 8, 1) * sa + pltpu.roll(x, LANES - 8, 1) * sb


def _shifted_conv(buf, p, w_ref, rows):
    buf[SUBLANES:SUBLANES + rows, :] = p
    p1 = buf[SUBLANES - 1:SUBLANES - 1 + rows, :]
    p2 = buf[SUBLANES - 2:SUBLANES - 2 + rows, :]
    return p2 * w_ref[0:1, :] + p1 * w_ref[1:2, :] + p * w_ref[2:3, :]


def _front_kernel(sample, tiles_per_seq, *refs):
    (x_ref, g_ref, cos_ref, sa_ref, sb_ref, wc_ref, wq_ref, wkv_ref, wgl_ref, wgab_ref, cw_ref, wco_ref) = refs[:12]
    if sample:
        st_ref, ma_ref, sgb_ref, q_ref, cmp_ref, slc_ref, win_ref, gl_ref, pst_ref, pbuf = refs[12:]
    else:
        ma_ref, sgb_ref, qt_ref, cmp_ref, slc_ref, win_ref, kb_ref, vt_ref, glt_ref, pst_ref, pbuf = refs[12:]
    rows, d = x_ref.shape
    u = _rms(x_ref[...], g_ref[...]).astype(BF16)

    if sample:
        row = lax.broadcasted_iota(jnp.int32, (rows, 1), 0) % SAMPLE_ROWS
        is_state = (row >= SAMPLE_ROWS - 4 - (CONV_W - 1)) & (row < SAMPLE_ROWS - 4)
        pbuf[0:SUBLANES, :] = jnp.zeros((SUBLANES, d), F32)
    else:
        @pl.when(pl.program_id(0) % tiles_per_seq == 0)
        def _():
            pbuf[0:SUBLANES, :] = jnp.zeros((SUBLANES, d), F32)
    ya = None
    for c in range(CONV_CHUNKS):
        lo, hi = c * d // CONV_CHUNKS, (c + 1) * d // CONV_CHUNKS
        p = _dot(u, wc_ref[:, d + lo:d + hi]) * _dot(u, wc_ref[:, lo:hi])
        if sample:
            p = jnp.where(is_state, st_ref[:, lo:hi], p)
        buf = pbuf.at[:, lo:hi]
        yc = _shifted_conv(buf, p, cw_ref.at[:, lo:hi], rows)
        tail = buf[rows:rows + SUBLANES, :]
        if sample:
            pst_ref[:, lo:hi] = p
        else:
            pst_ref[0, :, lo:hi] = tail
            buf[0:SUBLANES, :] = tail
        part = _dot((_dot(u, wc_ref[:, 2 * d + lo:2 * d + hi]) * yc).astype(BF16), wco_ref[lo:hi, :])
        ya = part if ya is None else ya + part

    zg = _dot(u, wgab_ref[...])
    ma_ref[...] = jax.nn.sigmoid(zg[:, 0:d]) * ya
    sgb_ref[...] = jax.nn.sigmoid(zg[:, d:2 * d])

    cos, sa, sb = cos_ref[...], sa_ref[...], sb_ref[...]
    zq = _dot(u, wq_ref[...])
    scale = HEAD_DIM ** -0.5 * LOG2_E
    for j in range(zq.shape[1] // LANES):
        blk = _rope_block(zq[:, j * LANES:(j + 1) * LANES], cos, sa, sb) * scale
        if sample:
            q_ref[:, j * LANES:(j + 1) * LANES] = blk.astype(BF16)
        else:
            qt_ref[j * LANES:(j + 1) * LANES, :] = blk.T.astype(BF16)

    zkv = _dot(u, wkv_ref[...])
    per_branch = KV_W // LANES
    half = per_branch // 2
    outs = (cmp_ref, slc_ref, win_ref)
    for j in range(zkv.shape[1] // LANES):
        br, jj = divmod(j, per_branch)
        blk = zkv[:, j * LANES:(j + 1) * LANES]
        if jj < half:
            blk = _rope_block(blk, cos, sa, sb)
        outs[br][:, jj * LANES:(jj + 1) * LANES] = blk
        if br > 0 and not sample:
            c = (br - 1) * half + jj % half
            if jj < half:
                kb_ref[:, c * LANES:(c + 1) * LANES] = blk.astype(BF16)
            else:
                vt_ref[c * LANES:(c + 1) * LANES, :] = blk.T.astype(BF16)

    gl = jax.nn.sigmoid(_dot(u, wgl_ref[...]))
    if sample:
        gl_ref[...] = gl
    else:
        for j in range(gl.shape[1] // LANES):
            glt_ref[j * LANES:(j + 1) * LANES, :] = gl[:, j * LANES:(j + 1) * LANES].T


def _front(x, g, tabs, w, conv_w, state, seq_rows):
    n, d = x.shape
    sample = state is not None
    tm = n if sample else ROW_TILE
    tiles_per_seq = seq_rows // tm
    nt = n // tm
    n_seq = n // seq_rows
    qw = N_HEADS * HEAD_DIM
    row = lambda w_: pl.BlockSpec((tm, w_), lambda i: (i, 0))
    col = lambda h_: pl.BlockSpec((h_, tm), lambda i: (0, i))
    tab = pl.BlockSpec((tm, LANES), lambda i: (i % tiles_per_seq, 0))
    args = [x, g, *tabs, *w[:5], conv_w, w[5]]
    in_specs = [row(d), _const_spec((1, d)), tab, tab, tab] + [_const_spec(a.shape) for a in w[:5]] + [
        _const_spec(conv_w.shape), _const_spec(w[5].shape)]
    f32_rows = lambda w_: jax.ShapeDtypeStruct((n, w_), F32)
    if sample:
        args.append(state)
        in_specs.append(row(d))
        out_shape = [f32_rows(d), f32_rows(d), jax.ShapeDtypeStruct((n, qw), BF16), f32_rows(KV_W), f32_rows(KV_W),
                     f32_rows(KV_W), f32_rows(2 * LANES), f32_rows(d)]
        out_specs = [row(d), row(d), row(qw), row(KV_W), row(KV_W), row(KV_W), row(2 * LANES), row(d)]
    else:
        tail_tiles = WINDOW // tm
        win_tail = pl.BlockSpec((tm, KV_W), lambda i: (
            (i // tiles_per_seq) * tail_tiles + jnp.maximum(i % tiles_per_seq - (tiles_per_seq - tail_tiles), 0), 0))
        out_shape = [f32_rows(d), f32_rows(d), jax.ShapeDtypeStruct((qw, n), BF16), f32_rows(KV_W), f32_rows(KV_W),
                     jax.ShapeDtypeStruct((n_seq * WINDOW, KV_W), F32), jax.ShapeDtypeStruct((n, KV_W), BF16),
                     jax.ShapeDtypeStruct((KV_W, n), BF16), jax.ShapeDtypeStruct((2 * LANES, n), F32),
                     jax.ShapeDtypeStruct((n_seq, SUBLANES, d), F32)]
        out_specs = [row(d), row(d), col(qw), row(KV_W), row(KV_W), win_tail, row(KV_W), col(KV_W),
                     col(2 * LANES), pl.BlockSpec((1, SUBLANES, d), lambda i: (i // tiles_per_seq, 0, 0))]
    return pl.pallas_call(
        functools.partial(_front_kernel, sample, tiles_per_seq),
        grid=(nt,), in_specs=in_specs, out_specs=out_specs, out_shape=out_shape,
        scratch_shapes=[pltpu.VMEM((tm + SUBLANES, d), F32)],
        compiler_params=_cparams(("arbitrary",)), name="front_sample" if sample else "front_prompt",
    )(*args)


def _cmp_bias_kernel(pe_ref, w1_ref, b1_ref, o_ref):
    for kv in range(2):
        a = _dot(pe_ref[kv], w1_ref[kv])
        o_ref[kv] = jnp.broadcast_to(b1_ref[kv] + a[0:1, 0:LANES] + a[1:2, LANES:2 * LANES], (SUBLANES, LANES))


def _compress_tile(lhs_of, w1_ref, hb_ref, w2_ref, b2_ref, carry_ref, sh_ref, out_ref, vt_ref=None):
    nck = out_ref.shape[0]
    for j in range(KV_W // LANES):
        kv = j // 2
        a = _dot(lhs_of(j), w1_ref[kv])
        sh_ref[SUBLANES:SUBLANES + nck, :] = a[:, 0:LANES]
        sh_ref[0:SUBLANES, :] = carry_ref[j]
        hid = sh_ref[SUBLANES - 1:SUBLANES - 1 + nck, :] + a[:, LANES:2 * LANES] + hb_ref[kv][0:1, :]
        carry_ref[j] = sh_ref[nck:nck + SUBLANES, :]
        o = _dot(jax.nn.silu(hid).astype(BF16), w2_ref[kv]) + b2_ref[kv]
        out_ref[:, j * LANES:(j + 1) * LANES] = o.astype(out_ref.dtype)
        if vt_ref is not None and kv == 1:
            vt_ref[(j - 2) * LANES:(j - 1) * LANES, :] = o.T.astype(vt_ref.dtype)


def _compress_prompt_kernel(r0, r1, r2, r3, w1_ref, hb_ref, w2_ref, b2_ref, out_ref, vt_ref, carry_ref, sh_ref):
    @pl.when(pl.program_id(1) == 0)
    def _():
        carry_ref[...] = jnp.zeros(carry_ref.shape, F32)
    planes = (r0, r1, r2, r3)
    nck = out_ref.shape[1]

    def lhs_of(j):
        return jnp.concatenate([planes[j][pl.ds(s, nck, stride=CMP_STRIDE), :].astype(BF16)
                                for s in range(CMP_STRIDE)], axis=1)

    _compress_tile(lhs_of, w1_ref, hb_ref, w2_ref, b2_ref, carry_ref, sh_ref, out_ref.at[0], vt_ref.at[0])


def _page_copies(pt_ref, cache_ref, buf_ref, sem_ref, slot, b, grp, n_pages):
    return [pltpu.make_async_copy(cache_ref.at[pt_ref[b, grp * n_pages + p]], buf_ref.at[slot, p], sem_ref.at[slot])
            for p in range(n_pages)]


def _pages_side_by_side(buf_ref, slot, rows, pages):
    return jnp.concatenate([buf_ref[slot, p, rows, :] for p in pages], axis=1)


def _gather_step(pt_ref, cache_ref, buf_ref, sem_ref, n_pages):
    b, g = pl.program_id(0), pl.program_id(1)
    nb, ng = pl.num_programs(0), pl.num_programs(1)
    step = b * ng + g
    slot = step % 2

    @pl.when(step == 0)
    def _():
        for c in _page_copies(pt_ref, cache_ref, buf_ref, sem_ref, 0, 0, 0, n_pages):
            c.start()

    @pl.when(step + 1 < nb * ng)
    def _():
        nxt = step + 1
        for c in _page_copies(pt_ref, cache_ref, buf_ref, sem_ref, 1 - slot, nxt // ng, nxt % ng, n_pages):
            c.start()

    for c in _page_copies(pt_ref, cache_ref, buf_ref, sem_ref, slot, b, g, n_pages):
        c.wait()
    return slot


def _compress_sample_kernel(pt_ref, cache_ref, perm_ref, w1_ref, hb_ref, w2_ref, b2_ref, out_ref,
                            buf_ref, sem_ref, carry_ref, sh_ref):
    n_pages = buf_ref.shape[1]
    slot = _gather_step(pt_ref, cache_ref, buf_ref, sem_ref, n_pages)

    @pl.when(pl.program_id(1) == 0)
    def _():
        carry_ref[...] = jnp.zeros(carry_ref.shape, F32)

    span = perm_ref.shape[0]
    ck = span // CMP_STRIDE

    per_span = span // buf_ref.shape[3]

    def permuted(half, g):
        x = _pages_side_by_side(buf_ref, slot, slice(half * 2 * LANES, (half + 1) * 2 * LANES),
                                range(g * per_span, (g + 1) * per_span))
        return _dot_nt(perm_ref[...], x.astype(BF16)).astype(BF16)

    tiles = [[permuted(half, g) for g in range(n_pages // per_span)] for half in range(2)]

    def lhs_of(j):
        half, jj = divmod(j, 2)
        return jnp.concatenate(
            [jnp.concatenate([tl[s * ck:(s + 1) * ck, jj * LANES:(jj + 1) * LANES] for tl in tiles[half]], axis=0)
             for s in range(CMP_STRIDE)], axis=1)

    _compress_tile(lhs_of, w1_ref, hb_ref, w2_ref, b2_ref, carry_ref, sh_ref, out_ref.at[0])


def _compress_weights(cmp_pe, cmp_w1, cmp_b1, cmp_w2, cmp_b2):
    r = CMP_LEN // CMP_STRIDE
    eye2 = jnp.eye(2, dtype=F32)
    w1 = cmp_w1.reshape(2, r, CMP_STRIDE, HEAD_DIM, HEAD_DIM)
    w1bd = jnp.einsum("krsde,hg->kshdrge", w1, eye2).reshape(2, CMP_STRIDE * LANES, r * LANES).astype(BF16)
    w2bd = jnp.einsum("kde,hg->khdge", cmp_w2, eye2).reshape(2, LANES, LANES).astype(BF16)
    pe = cmp_pe.reshape(2, r, CMP_STRIDE, 1, HEAD_DIM)
    pe = jnp.broadcast_to(pe, (2, r, CMP_STRIDE, 2, HEAD_DIM)).reshape(2, r, CMP_STRIDE * LANES)
    pe = jnp.pad(pe, ((0, 0), (0, SUBLANES - r), (0, 0))).astype(BF16)
    b1 = jnp.tile(cmp_b1, (1, 2)).reshape(2, 1, LANES)
    b2 = jnp.tile(cmp_b2, (1, 2)).reshape(2, 1, LANES)
    hb = pl.pallas_call(
        _cmp_bias_kernel, out_shape=jax.ShapeDtypeStruct((2, SUBLANES, LANES), F32), name="compress_bias",
    )(pe, w1bd, b1)
    return w1bd, hb, w2bd, b2


def _compress_prompt(cmp_rows, cw, n_seq):
    n = cmp_rows.shape[0]
    s = n // n_seq
    nck = CMP_ROWS // CMP_STRIDE
    tiles = s // CMP_ROWS
    w1bd, hb, w2bd, b2 = cw
    return pl.pallas_call(
        _compress_prompt_kernel,
        grid=(n_seq, tiles),
        in_specs=[pl.BlockSpec((CMP_ROWS, LANES), functools.partial(lambda j, b, t: (b * tiles + t, j), j))
                  for j in range(KV_W // LANES)] + [
                  _const_spec(w1bd.shape), _const_spec(hb.shape), _const_spec(w2bd.shape), _const_spec(b2.shape)],
        out_specs=[pl.BlockSpec((1, nck, KV_W), lambda b, t: (b, t, 0)),
                   pl.BlockSpec((1, KV_W // 2, nck), lambda b, t: (b, 0, t))],
        out_shape=[jax.ShapeDtypeStruct((n_seq, s // CMP_STRIDE, KV_W), BF16),
                   jax.ShapeDtypeStruct((n_seq, KV_W // 2, s // CMP_STRIDE), BF16)],
        scratch_shapes=[pltpu.VMEM((KV_W // LANES, SUBLANES, LANES), F32),
                        pltpu.VMEM((nck + SUBLANES, LANES), F32)],
        compiler_params=_cparams(("arbitrary", "arbitrary")), name="compress_prompt",
    )(cmp_rows, cmp_rows, cmp_rows, cmp_rows, w1bd, hb, w2bd, b2)


def _feature_major_pages(cache):
    return cache.transpose(0, 2, 3, 4, 1).reshape(cache.shape[0], KV_W, cache.shape[1])


def _compress_sample(page_table, cache_t, cw):
    db, n_pages = page_table.shape
    page_rows = cache_t.shape[2]
    pages_per = CMP_ROWS // page_rows
    groups = n_pages // pages_per
    nck = CMP_ROWS // CMP_STRIDE
    w1bd, hb, w2bd, b2 = cw
    span = 2 * page_rows
    ck = span // CMP_STRIDE
    row = np.arange(span)
    s_of, c_of = row // ck, row % ck
    per_page = page_rows // CMP_STRIDE
    tok = (c_of // per_page) * page_rows + CMP_STRIDE * (c_of % per_page) + s_of
    perm = jnp.asarray(tok[:, None] == np.arange(span)[None, :], dtype=BF16)
    cspec = lambda a: pl.BlockSpec(a.shape, lambda b, g, pt: (0,) * a.ndim, pipeline_mode=pl.Buffered(1))
    grid_spec = pltpu.PrefetchScalarGridSpec(
        num_scalar_prefetch=1, grid=(db, groups),
        in_specs=[pl.BlockSpec(memory_space=pl.ANY), cspec(perm), cspec(w1bd), cspec(hb), cspec(w2bd), cspec(b2)],
        out_specs=pl.BlockSpec((1, nck, KV_W), lambda b, g, pt: (b, g, 0)),
        scratch_shapes=[pltpu.VMEM((2, pages_per, KV_W, page_rows), F32), pltpu.SemaphoreType.DMA((2,)),
                        pltpu.VMEM((KV_W // LANES, SUBLANES, LANES), F32),
                        pltpu.VMEM((nck + SUBLANES, LANES), F32)])
    return pl.pallas_call(
        _compress_sample_kernel, grid_spec=grid_spec,
        out_shape=jax.ShapeDtypeStruct((db, n_pages * page_rows // CMP_STRIDE, KV_W), BF16),
        compiler_params=_cparams(("arbitrary", "arbitrary")), name="compress_sample",
    )(page_table, cache_t, perm, w1bd, hb, w2bd, b2)


def _masked_softmax(s, mask):
    s = jnp.where(mask, s, -jnp.inf)
    m = jnp.max(s, axis=-1, keepdims=True)
    m = jnp.where(m == -jnp.inf, 0.0, m)
    e = jnp.where(mask, jnp.exp2(s - m), 0.0)
    return e / jnp.maximum(jnp.sum(e, axis=-1, keepdims=True), 1e-30)


def _select_blocks(imp, blk, t_pos, axis):
    cur = t_pos // SLC_LEN
    valid = blk <= cur
    forced = (blk == 0) | (valid & (blk > cur - N_LOCAL))
    cand = jnp.where(forced, -jnp.inf, jnp.where(valid, imp, -FORCED_SCORE))
    cand = jnp.where(blk >= 0, cand, -jnp.inf)
    length = imp.shape[axis]
    idx = lax.broadcasted_iota(jnp.int32, imp.shape, axis).astype(F32)

    def pick(_, carry):
        cand, notsel = carry
        m = jnp.max(cand, axis=axis, keepdims=True)
        first = jnp.min(jnp.where(cand == m, idx, float(length)), axis=axis, keepdims=True)
        hit = idx == first
        return jnp.where(hit, -jnp.inf, cand), jnp.where(hit, 0.0, notsel)

    _, notsel = lax.fori_loop(0, N_SEL - (N_LOCAL + 1), pick, (cand, jnp.where(forced, 0.0, 1.0)), unroll=True)
    return notsel


def _flash_update(carry, s, v, v_transposed=False):
    m, l, acc = carry
    m_new = jnp.maximum(m, jnp.max(s, axis=-1, keepdims=True))
    alpha = jnp.exp2(m - m_new)
    p = jnp.exp2(s - m_new)
    l = alpha * l + jnp.sum(p, axis=-1, keepdims=True)
    pv = _dot_nt(p.astype(v.dtype), v) if v_transposed else _dot(p.astype(v.dtype), v)
    return m_new, l, alpha * acc + pv


def _half_mask(shape):
    return lax.broadcasted_iota(jnp.int32, shape, len(shape) - 1) < HEAD_DIM


def _stack_heads(q_blk, nr):
    first = _half_mask((q_blk.shape[0], LANES))
    zero = jnp.zeros((), q_blk.dtype)
    parts = []
    for gl in range(2):
        for r in range(nr):
            blk = q_blk[:, r * LANES:(r + 1) * LANES]
            parts.append(jnp.where(first if gl == 0 else ~first, blk, zero))
    return jnp.concatenate(parts, axis=0)


def _merge_pair(o, nr, t):
    first = _half_mask((t, LANES))
    return [jnp.where(first, o[r * t:(r + 1) * t], o[(nr + r) * t:(nr + r + 1) * t]) for r in range(nr)]


def _gate_vec(g_ref, br, r, t):
    c = br * 2 * (N_HEADS // N_KV_HEADS) + r * 2
    return jnp.where(_half_mask((t, LANES)), g_ref[:, c:c + 1], g_ref[:, c + 1:c + 2])


def _biased_exp0(s, bias):
    s = s + jnp.concatenate([bias] * (s.shape[1] // bias.shape[1]), axis=1)
    m = jnp.max(s, axis=0, keepdims=True)
    m = jnp.where(m < -0.5 * MASK_BIG, 0.0, m)
    return jnp.exp2(s - m)


def _recip_sum(total):
    return 1.0 / jnp.maximum(total, 1e-30)


def _pv_own_head(vt, p, with_sum=False):
    half = p.shape[1] // 2
    heads = [vt[0:HEAD_DIM, :], vt[HEAD_DIM:2 * HEAD_DIM, :]]
    if with_sum:
        ones = jnp.ones((SUM_ROWS, vt.shape[1]), vt.dtype)
        heads = [jnp.concatenate([h, ones], axis=0) for h in heads]
    return jnp.concatenate([_dot(heads[0], p[:, 0:half]), _dot(heads[1], p[:, half:])], axis=1)


def _flash_update0(carry, s, vt):
    m, acc = carry
    m_new = jnp.maximum(m, jnp.max(s, axis=0, keepdims=True))
    alpha = jnp.exp2(m - m_new)
    p = jnp.exp2(s - m_new)
    return m_new, alpha * acc + _pv_own_head(vt, p.astype(vt.dtype), with_sum=True)


def _attend_prompt_kernel(qt_ref, gt_ref, kc_ref, vct_ref, kb_ref, vt_ref, e_ref, ovlt_ref,
                          cbias_ref, wbias_ref, fbias_ref,
                          o_ref, qa_ref, sa_ref, sb_ref, m_ref, acc_ref, oc_ref, ow_ref):
    i = pl.program_id(1)
    nr = N_HEADS // N_KV_HEADS
    t = Q_TILE
    rows = 2 * nr * t
    n_pairs = N_KV_HEADS // 2
    t0 = i * t
    tq = t0 + lax.broadcasted_iota(jnp.int32, (1, t), 1)
    first = lax.broadcasted_iota(jnp.int32, (LANES, t), 0) < HEAD_DIM
    zero = jnp.zeros((), BF16)
    ncmp = kc_ref.shape[1]
    pair_lanes = lambda k: slice(k * LANES, (k + 1) * LANES)
    win_lanes = lambda k: slice((n_pairs + k) * LANES, (n_pairs + k + 1) * LANES)

    qts = [jnp.concatenate([jnp.where(first if gl == 0 else ~first,
                                      qt_ref[(k * nr + r) * LANES:(k * nr + r + 1) * LANES, :], zero)
                            for gl in range(2) for r in range(nr)], axis=1) for k in range(n_pairs)]

    c_off = pl.multiple_of(ncmp - i * (t // CMP_STRIDE), t // CMP_STRIDE)
    cbias = cbias_ref[pl.ds(c_off, ncmp), :]
    not_junk = lax.broadcasted_iota(jnp.int32, (SUBLANES, 1), 0) >= 1
    psum = []
    for k in range(n_pairs):
        s_c = _dot(kc_ref[0, :, pair_lanes(k)], qts[k])
        head = jnp.where(not_junk, s_c[0:SUBLANES], -MASK_BIG)
        e_c = _biased_exp0(jnp.concatenate([head, s_c[SUBLANES:]], axis=0), cbias)
        p_c = e_c * _recip_sum(jnp.sum(e_c, axis=0, keepdims=True))
        oc_ref[k] = _pv_own_head(vct_ref[0, pair_lanes(k), :], p_c.astype(BF16))
        for gl in range(2):
            acc = p_c[:, gl * nr * t:gl * nr * t + t]
            for r in range(1, nr):
                acc = acc + p_c[:, (gl * nr + r) * t:(gl * nr + r + 1) * t]
            psum.append(acc)
    imp = jnp.dot(ovlt_ref[...], jnp.concatenate(psum, axis=1), preferred_element_type=F32,
                  precision=lax.Precision.HIGHEST)

    wlen = WINDOW + t
    ws = pl.multiple_of(jnp.maximum(t0 - WINDOW, 0), t)
    anchor = (imp[0:1, 0:1] * 0.0).astype(BF16)
    wbias = wbias_ref[jnp.minimum(i, WINDOW // t)]
    for k in range(n_pairs):
        e_w = _biased_exp0(_dot(kb_ref[pl.ds(ws, wlen), win_lanes(k)] + anchor, qts[k]), wbias)
        pv = _pv_own_head(vt_ref[win_lanes(k), pl.ds(ws, wlen)], e_w.astype(BF16), with_sum=True)
        ow_ref[k] = pv[0:HEAD_DIM] * _recip_sum(pv[HEAD_DIM:HEAD_DIM + 1])

    nslc = ovlt_ref.shape[0]
    blk = lax.broadcasted_iota(jnp.int32, (nslc, 1), 0)
    ns = _select_blocks(imp, blk, jnp.concatenate([tq] * N_KV_HEADS, axis=1), axis=0).astype(BF16)
    for k in range(n_pairs):
        flags = [ns[:, (2 * k + gl) * t:(2 * k + gl + 1) * t] for gl in range(2)]
        qa_ref[k] = jnp.concatenate([qts[k], jnp.concatenate([flags[0]] * nr + [flags[1]] * nr, axis=1)], axis=0)

    def produce(s_ref, j):
        off = pl.multiple_of(j * K_TILE, K_TILE)
        for k in range(n_pairs):
            ka = jnp.concatenate([kb_ref[pl.ds(off, K_TILE), pair_lanes(k)], e_ref[pl.ds(off, K_TILE), :]], axis=1)
            s_ref[k] = _dot(ka, qa_ref[k])

    def consume(s_ref, j, causal):
        off = pl.multiple_of(j * K_TILE, K_TILE)
        for k in range(n_pairs):
            s = s_ref[k]
            if causal:
                fb = fbias_ref[(t0 - off) // t]
                s = s + jnp.concatenate([fb] * (rows // t), axis=1)
            m_ref[k], acc_ref[k] = _flash_update0(
                (m_ref[k], acc_ref[k]), s, vt_ref[pair_lanes(k), pl.ds(off, K_TILE)])

    m_ref[...] = jnp.full(m_ref.shape, -jnp.inf, F32)
    acc_ref[...] = jnp.zeros(acc_ref.shape, F32)
    last = (t0 + t - 1) // K_TILE
    produce(sa_ref, 0)

    def pair(jj, _):
        j = 2 * jj
        produce(sb_ref, j + 1)
        consume(sa_ref, j, False)
        produce(sa_ref, j + 2)
        consume(sb_ref, j + 1, False)
        return 0

    lax.fori_loop(0, last // 2, pair, 0)

    @pl.when(last % 2 == 1)
    def _():
        produce(sb_ref, last)
        consume(sa_ref, last - 1, False)
        consume(sb_ref, last, True)

    @pl.when(last % 2 == 0)
    def _():
        consume(sa_ref, last, True)

    for k in range(n_pairs):
        o_s = acc_ref[k, 0:HEAD_DIM, :] * (1.0 / acc_ref[k, HEAD_DIM:HEAD_DIM + 1, :])
        branches = (oc_ref[k], o_s, ow_ref[k])
        for r in range(nr):
            c0, c1 = r * t, (nr + r) * t
            o = None
            for br, ob in enumerate(branches):
                c = k * LANES + br * 2 * nr + r * 2
                gate = jnp.where(first, gt_ref[c:c + 1, :], gt_ref[c + 1:c + 2, :])
                term = gate * jnp.concatenate([ob[:, c0:c0 + t], ob[:, c1:c1 + t]], axis=0)
                o = term if o is None else o + term
            o_ref[:, (k * nr + r) * LANES:(k * nr + r + 1) * LANES] = o.T.astype(o_ref.dtype)


def _mask_bias(visible):
    return jnp.asarray(np.where(visible, 0.0, -MASK_BIG), dtype=F32)


def _attend_prompt(qt, gt, kcvc, vct, kb, vt, econst, ovlt, n_seq):
    t = Q_TILE
    tcol = np.arange(t)[None, :]
    ncmp_ = kcvc.shape[1]
    x = np.arange(-ncmp_, ncmp_)[:, None]
    cbias = _mask_bias(x * CMP_STRIDE + (CMP_LEN - CMP_STRIDE - 1) <= tcol)
    r = np.arange(WINDOW + t)[None, :, None]
    d = (np.arange(WINDOW // t + 1) * t)[:, None, None]
    wbias = _mask_bias((r <= d + tcol[None]) & (r > d + tcol[None] - WINDOW))
    r = np.arange(K_TILE)[None, :, None]
    d = (np.arange(K_TILE // t) * t)[:, None, None]
    fbias = _mask_bias(r <= d + tcol[None])
    n = qt.shape[1]
    s = n // n_seq
    nq = s // Q_TILE
    nr = N_HEADS // N_KV_HEADS
    ncmp = kcvc.shape[1]
    qw = N_HEADS * HEAD_DIM
    rows = 2 * nr * Q_TILE
    n_pairs = N_KV_HEADS // 2
    per_seq = lambda shape, imap: pl.BlockSpec(shape, imap, pipeline_mode=pl.Buffered(1))
    return pl.pallas_call(
        _attend_prompt_kernel,
        grid=(n_seq, nq),
        in_specs=[pl.BlockSpec((qw, Q_TILE), lambda b, i: (0, b * nq + i)),
                  pl.BlockSpec((n_pairs * LANES, Q_TILE), lambda b, i: (0, b * nq + i)),
                  per_seq((1, ncmp, n_pairs * LANES), lambda b, i: (b, 0, 0)),
                  per_seq((1, n_pairs * LANES, ncmp), lambda b, i: (b, 0, 0)),
                  per_seq((s, KV_W), lambda b, i: (b, 0)), per_seq((KV_W, s), lambda b, i: (0, b)),
                  _const_spec(econst.shape), _const_spec(ovlt.shape), _const_spec(cbias.shape),
                  _const_spec(wbias.shape), _const_spec(fbias.shape)],
        out_specs=pl.BlockSpec((Q_TILE, qw), lambda b, i: (b * nq + i, 0)),
        out_shape=jax.ShapeDtypeStruct((n, qw), BF16),
        scratch_shapes=[pltpu.VMEM((n_pairs, 2 * LANES, rows), BF16), pltpu.VMEM((n_pairs, K_TILE, rows), F32),
                        pltpu.VMEM((n_pairs, K_TILE, rows), F32), pltpu.VMEM((n_pairs, 1, rows), F32),
                        pltpu.VMEM((n_pairs, HEAD_DIM + SUM_ROWS, rows), F32),
                        pltpu.VMEM((n_pairs, HEAD_DIM, rows), F32), pltpu.VMEM((n_pairs, HEAD_DIM, rows), F32)],
        compiler_params=_cparams(("arbitrary", "arbitrary")), name="attend_prompt",
    )(qt, gt, kcvc, vct, kb, vt, econst, ovlt, cbias, wbias, fbias)


def _attend_sample_kernel(past_len, pt_ref, q_ref, g_ref, kcvc_ref, cache_ref, slc_ref, wcache_ref, win_ref,
                          e_ref, ovl_ref, o_ref, buf_ref, sem_ref, ns_ref, oc_ref, m_ref, l_ref, acc_ref):
    g = pl.program_id(1)
    ng = pl.num_programs(1)
    nr = N_HEADS // N_KV_HEADS
    t = SAMPLE_ROWS
    rows = 2 * nr * t
    n_pages = buf_ref.shape[1]
    keys = n_pages * buf_ref.shape[3]
    slot = _gather_step(pt_ref, cache_ref, buf_ref, sem_ref, n_pages)
    tq = past_len - (t - 4) + lax.broadcasted_iota(jnp.int32, (t, 1), 0)
    tq_rows = jnp.concatenate([tq] * (2 * nr), axis=0)
    qs = [_stack_heads(q_ref[:, k * nr * LANES:(k + 1) * nr * LANES], nr) for k in range(2)]

    @pl.when(g == 0)
    def _():
        ncmp = kcvc_ref.shape[1]
        cidx = lax.broadcasted_iota(jnp.int32, (1, ncmp), 1)
        c_mask = (cidx >= 1) & (cidx * CMP_STRIDE + (CMP_LEN - CMP_STRIDE - 1) <= tq_rows)
        width = ovl_ref.shape[1]
        lane = lax.broadcasted_iota(jnp.int32, (1, width), 1)
        per = keys // SLC_LEN
        blk = (lane // LANES) * per + lane % LANES
        real = (lane % LANES < per) & (blk <= past_len // SLC_LEN)
        psums = []
        for k in range(2):
            p_c = _masked_softmax(_dot_nt(qs[k], kcvc_ref[0, :, k * LANES:(k + 1) * LANES]), c_mask)
            oc_ref[k] = _dot(p_c.astype(BF16), kcvc_ref[0, :, (2 + k) * LANES:(3 + k) * LANES])
            for gl in range(2):
                psum = p_c[gl * nr * t:gl * nr * t + t]
                for r in range(1, nr):
                    psum = psum + p_c[(gl * nr + r) * t:(gl * nr + r + 1) * t]
                psums.append(psum)
        imp = jnp.dot(jnp.concatenate(psums, axis=0), ovl_ref[...], preferred_element_type=F32,
                      precision=lax.Precision.HIGHEST)
        ns = _select_blocks(imp, jnp.where(real, blk, -1), jnp.concatenate([tq] * 4, axis=0), axis=1)
        for k in range(2):
            for gl in range(2):
                for r in range(nr):
                    ns_ref[k, (gl * nr + r) * t:(gl * nr + r + 1) * t, :] = ns[(2 * k + gl) * t:(2 * k + gl + 1) * t]
        m_ref[...] = jnp.full(m_ref.shape, -jnp.inf, F32)
        l_ref[...] = jnp.zeros(l_ref.shape, F32)
        acc_ref[...] = jnp.zeros(acc_ref.shape, F32)

    feat = lambda blk: _pages_side_by_side(buf_ref, slot, slice(blk * LANES, (blk + 1) * LANES), range(n_pages))
    for k in range(2):
        ns = ns_ref[k, :, pl.ds(pl.multiple_of(g * LANES, LANES), LANES)]
        qa = jnp.concatenate([qs[k], ns.astype(BF16)], axis=1)
        ka = jnp.concatenate([feat(k).astype(BF16), e_ref[...]], axis=0)
        v = feat(2 + k).astype(BF16)
        m_ref[k], l_ref[k], acc_ref[k] = _flash_update((m_ref[k], l_ref[k], acc_ref[k]), _dot(qa, ka), v,
                                                       v_transposed=True)

    @pl.when(g == ng - 1)
    def _():
        kidx = lax.broadcasted_iota(jnp.int32, (1, t), 1)
        kpos = past_len - (t - 4) + kidx
        new_ok = (kidx >= t - 4) & (kpos <= tq_rows)
        wb = wcache_ref.shape[2]
        wpos =past_len - wb + lax.broadcasted_iota(jnp.int32, (1, wb), 1)
        w_old = (wpos <= tq_rows) & (wpos > tq_rows - WINDOW) & (wpos >= 0)
        w_new = new_ok & (kpos > tq_rows - WINDOW)
        for k in range(2):
            kcol, vcol = slice(k * LANES, (k + 1) * LANES), slice((2 + k) * LANES, (3 + k) * LANES)
            qf = qs[k].astype(F32)
            s_new = jnp.where(new_ok, _dot_nt(qf, slc_ref[:, kcol]), -MASK_BIG)
            _, l, acc = _flash_update((m_ref[k], l_ref[k], acc_ref[k]), s_new, slc_ref[:, vcol])
            o_s = acc / l
            so = jnp.where(w_old, _dot(qs[k], wcache_ref[0, kcol, :].astype(BF16)), -jnp.inf)
            sn = jnp.where(w_new, _dot_nt(qf, win_ref[:, kcol]), -jnp.inf)
            m = jnp.maximum(jnp.max(so, axis=-1, keepdims=True), jnp.max(sn, axis=-1, keepdims=True))
            m = jnp.where(m == -jnp.inf, 0.0, m)
            eo = jnp.where(w_old, jnp.exp2(so - m), 0.0)
            en = jnp.where(w_new, jnp.exp2(sn - m), 0.0)
            den = jnp.maximum(jnp.sum(eo, axis=-1, keepdims=True) + jnp.sum(en, axis=-1, keepdims=True), 1e-30)
            o_w = (_dot_nt(eo.astype(BF16), wcache_ref[0, vcol, :].astype(BF16)) + _dot(en, win_ref[:, vcol])) / den
            oc, os_, ow = _merge_pair(oc_ref[k], nr, t), _merge_pair(o_s, nr, t), _merge_pair(o_w, nr, t)
            gk = g_ref.at[:, k * LANES:(k + 1) * LANES]
            for r in range(nr):
                o = _gate_vec(gk, 0, r, t) * oc[r] + _gate_vec(gk, 1, r, t) * os_[r] + _gate_vec(gk, 2, r, t) * ow[r]
                o_ref[:, (k * nr + r) * LANES:(k * nr + r + 1) * LANES] = o.astype(o_ref.dtype)


def _attend_sample(page_table, q, gates, kcvc, cache_t, slc_new, wcache, win_new, econst_t, ovl, past_len):
    db, n_pages = page_table.shape
    page_rows = cache_t.shape[2]
    econst = econst_t
    groups = n_pages // PAGES_PER_STEP
    keys = PAGES_PER_STEP * page_rows
    nr = N_HEADS // N_KV_HEADS
    t = SAMPLE_ROWS
    rows = 2 * nr * t
    qw = N_HEADS * HEAD_DIM
    cmap = lambda nd: (lambda b, g, pt: (0,) * nd)
    grid_spec = pltpu.PrefetchScalarGridSpec(
        num_scalar_prefetch=1, grid=(db, groups),
        in_specs=[pl.BlockSpec((t, qw), lambda b, g, pt: (b, 0)),
                  pl.BlockSpec((t, 2 * LANES), lambda b, g, pt: (b, 0)),
                  pl.BlockSpec((1,) + kcvc.shape[1:], lambda b, g, pt: (b, 0, 0)),
                  pl.BlockSpec(memory_space=pl.ANY),
                  pl.BlockSpec((t, KV_W), lambda b, g, pt: (b, 0)),
                  pl.BlockSpec((1,) + wcache.shape[1:], lambda b, g, pt: (b, 0, 0)),
                  pl.BlockSpec((t, KV_W), lambda b, g, pt: (b, 0)),
                  pl.BlockSpec(econst.shape, cmap(2), pipeline_mode=pl.Buffered(1)),
                  pl.BlockSpec(ovl.shape, cmap(2), pipeline_mode=pl.Buffered(1))],
        out_specs=pl.BlockSpec((t, qw), lambda b, g, pt: (b, 0)),
        scratch_shapes=[pltpu.VMEM((2, PAGES_PER_STEP, KV_W, page_rows), F32), pltpu.SemaphoreType.DMA((2,)),
                        pltpu.VMEM((2, rows, ovl.shape[1]), F32), pltpu.VMEM((2, rows, LANES), F32),
                        pltpu.VMEM((2, rows, 1), F32), pltpu.VMEM((2, rows, 1), F32),
                        pltpu.VMEM((2, rows, LANES), F32)])
    return pl.pallas_call(
        functools.partial(_attend_sample_kernel, past_len), grid_spec=grid_spec,
        out_shape=jax.ShapeDtypeStruct((db * t, qw), BF16),
        compiler_params=_cparams(("arbitrary", "arbitrary")), name="attend_sample",
    )(page_table, q, gates, kcvc, cache_t, slc_new, wcache, win_new, econst, ovl)


def _post_kernel(sample, tiles_per_seq, *refs):
    if sample:
        (x_ref, o_ref, ma_ref, sgb_ref, wao_ref, wo_ref, gf_ref, wfi_ref, fcw_ref, wfd_ref, gfin_ref, st_ref,
         y_ref, ast_ref, abuf) = refs
    else:
        (x_ref, o_ref, ma_ref, sgb_ref, wao_ref, wo_ref, gf_ref, wfi_ref, fcw_ref, wfd_ref, gfin_ref,
         y_ref, ast_ref, abuf) = refs
    rows = x_ref.shape[0]
    dff = fcw_ref.shape[1]
    m = ma_ref[...] + sgb_ref[...] * _dot(o_ref[...], wao_ref[...])
    h = x_ref[...] + _dot(m.astype(BF16), wo_ref[...])
    v = _rms(h, gf_ref[...]).astype(BF16)
    if sample:
        row = lax.broadcasted_iota(jnp.int32, (rows, 1), 0) % SAMPLE_ROWS
        is_state = (row >= SAMPLE_ROWS - 4 - (CONV_W - 1)) & (row < SAMPLE_ROWS - 4)
        abuf[0:SUBLANES, :] = jnp.zeros((SUBLANES, dff), F32)
    else:
        @pl.when(pl.program_id(0) % tiles_per_seq == 0)
        def _():
            abuf[0:SUBLANES, :] = jnp.zeros((SUBLANES, dff), F32)
    bounds = [LANES * ((dff // LANES) * c // FFN_CHUNKS) for c in range(FFN_CHUNKS + 1)]
    hh = h
    for lo, hi in zip(bounds[:-1], bounds[1:]):
        a = _dot(v, wfi_ref[:, lo:hi])
        g = _dot(v, wfi_ref[:, dff + lo:dff + hi])
        if sample:
            a = jnp.where(is_state, st_ref[:, lo:hi], a)
        buf = abuf.at[:, lo:hi]
        ac = _shifted_conv(buf, a, fcw_ref.at[:, lo:hi], rows)
        tail = buf[rows:rows + SUBLANES, :]
        if sample:
            ast_ref[:, lo:hi] = a
        else:
            ast_ref[0, :, lo:hi] = tail
            buf[0:SUBLANES, :] = tail
        hh = hh + _dot((jax.nn.silu(ac) * g).astype(BF16), wfd_ref[lo:hi, :])
    y_ref[...] = _rms(hh, gfin_ref[...])


def _post(x, o, ma, sgb, w, state, seq_rows):
    n, d = x.shape
    sample = state is not None
    tm = n if sample else ROW_TILE
    tiles_per_seq = seq_rows // tm
    n_seq = n // seq_rows
    wao, wo, gf, wfi, fcw, wfd, gfin = w
    dff = fcw.shape[1]
    row = lambda w_: pl.BlockSpec((tm, w_), lambda i: (i, 0))
    args = [x, o, ma, sgb, *w]
    in_specs = [row(d), row(o.shape[1]), row(d), row(d)] + [_const_spec(a.shape) for a in w]
    if sample:
        args.append(state)
        in_specs.append(row(dff))
        ast_shape, ast_spec = jax.ShapeDtypeStruct((n, dff), F32), row(dff)
    else:
        ast_shape = jax.ShapeDtypeStruct((n_seq, SUBLANES, dff), F32)
        ast_spec = pl.BlockSpec((1, SUBLANES, dff), lambda i: (i // tiles_per_seq, 0, 0))
    return pl.pallas_call(
        functools.partial(_post_kernel, sample, tiles_per_seq),
        grid=(n // tm,), in_specs=in_specs, out_specs=[row(d), ast_spec],
        out_shape=[jax.ShapeDtypeStruct((n, d), F32), ast_shape],
        scratch_shapes=[pltpu.VMEM((tm + SUBLANES, dff), F32)],
        compiler_params=_cparams(("arbitrary",)), name="post_sample" if sample else "post_prompt",
    )(*args)


def _pair_head_order():
    nr = N_HEADS // N_KV_HEADS
    return [2 * nr * k + nr * half + r for k in range(2) for r in range(nr) for half in range(2)]


def _rope_tables(pos):
    half = ROT_DIM // 2
    inv_freq = jnp.power(ROPE_THETA, -jnp.arange(half, dtype=F32) * (2.0 / ROT_DIM))
    ang = pos.astype(F32)[:, None] * inv_freq[None, :]
    cos, sin = jnp.cos(ang), jnp.sin(ang)
    n = pos.shape[0]
    one, zero = jnp.ones((n, HEAD_DIM - ROT_DIM), F32), jnp.zeros((n, HEAD_DIM - ROT_DIM), F32)
    zh = jnp.zeros((n, half), F32)
    tabs = (jnp.concatenate([cos, cos, one], 1), jnp.concatenate([zh, sin, zero], 1),
            jnp.concatenate([-sin, zh, zero], 1))
    return tuple(jnp.tile(a, (1, LANES // HEAD_DIM)) for a in tabs)


def _overlap_matrix(n_rows, n_cols, col_block):
    c = np.arange(n_rows)[:, None] - 1
    blk = col_block[None, :]
    cs, ss = c * CMP_STRIDE, blk * SLC_LEN
    return ((c >= 0) & (blk >= 0) & (cs < ss + SLC_LEN) & (cs + CMP_LEN > ss)).astype(np.float32)


def _block_onehot(n_keys, transposed=False):
    e = (np.arange(n_keys)[:, None] // SLC_LEN == np.arange(LANES)[None, :]).astype(np.float32)
    return jnp.asarray(-MASK_BIG * (e.T if transposed else e), dtype=BF16)


def _split_w_in(w_in, d):
    conv_dim = d
    q_dim = N_HEADS * HEAD_DIM
    kv_dim = N_BRANCH * KV_W
    o0 = 3 * conv_dim
    wc = w_in[:, 0:o0]
    wq = w_in[:, o0:o0 + q_dim].reshape(d, N_HEADS, HEAD_DIM)[:, np.array(_pair_head_order())].reshape(d, q_dim)
    o1 = o0 + q_dim
    wkv = w_in[:, o1:o1 + kv_dim]
    o2 = o1 + kv_dim
    wl = w_in[:, o2:o2 + N_BRANCH * N_HEADS].reshape(d, N_HEADS, N_BRANCH)
    wl = wl[:, np.array(_pair_head_order())].reshape(d, 2, N_HEADS // 2, N_BRANCH).transpose(0, 1, 3, 2)
    wl = jnp.pad(wl.reshape(d, 2, N_BRANCH * N_HEADS // 2), ((0, 0), (0, 0), (0, LANES - N_BRANCH * N_HEADS // 2)))
    wl = wl.reshape(d, 2 * LANES)
    o3 = o2 + N_BRANCH * N_HEADS
    wgab = w_in[:, o3:o3 + 2 * d]
    return [a.astype(BF16) for a in (wc, wq, wkv, wl, wgab)]


def kernel(x_prompt, x_sample, cache_cmp_kv, cache_slc_kv, cache_win_kv, state_conv_mix, state_conv_ffn,
           page_table, norm_mix_g, w_in, conv_mix_w, w_conv_out, cmp_pe, cmp_w1, cmp_b1, cmp_w2, cmp_b2,
           w_attn_out, w_out, norm_ffn_g, w_ff_in, ff_conv_w, w_ff_down, norm_final_g):
    depth = w_in.shape[0]
    assert depth == 1, "single-layer step"
    b, s, d = x_prompt.shape
    db, t, _ = x_sample.shape
    page_rows = cache_cmp_kv.shape[2]
    past_len = page_table.shape[1] * page_rows
    assert t == 4 and s % K_TILE == 0 and s % CMP_ROWS == 0 and s >= WINDOW + Q_TILE
    assert past_len % (PAGES_PER_STEP * page_rows) == 0 and cache_win_kv.shape[2] == WINDOW
    assert N_SEL <= s // SLC_LEN <= LANES, "prompt selection blocks fit one lane block"
    dff = ff_conv_w.shape[2]
    l = 0

    order = np.array(_pair_head_order())
    front_w = _split_w_in(w_in[l], d) + [w_conv_out[l].astype(BF16)]
    post_w = [w_attn_out[l].reshape(N_HEADS, HEAD_DIM, d)[order].reshape(N_HEADS * HEAD_DIM, d).astype(BF16),
              w_out[l].astype(BF16), norm_ffn_g[l].reshape(1, d), w_ff_in[l].astype(BF16), ff_conv_w[l],
              w_ff_down[l].astype(BF16), norm_final_g.reshape(1, d)]
    g_mix = norm_mix_g[l].reshape(1, d)
    cw = _compress_weights(cmp_pe[l], cmp_w1[l], cmp_b1[l], cmp_w2[l], cmp_b2[l])
    econst = _block_onehot(s)
    econst_t = _block_onehot(PAGES_PER_STEP * page_rows, transposed=True)

    xp = x_prompt.reshape(b * s, d)
    ma, sgb, qt, cmp_p, slc_p, win_p, kb, vt, gt, pst = _front(
        xp, g_mix, _rope_tables(jnp.arange(s, dtype=jnp.int32)), front_w, conv_mix_w[l], None, s)
    kcvc, vct = _compress_prompt(cmp_p, cw, b)
    ovlt_p = jnp.asarray(_overlap_matrix(s // CMP_STRIDE, LANES, np.arange(LANES)).T)
    o = _attend_prompt(qt, gt, kcvc, vct, kb, vt, econst, ovlt_p, b)
    y_p, ast = _post(xp, o, ma, sgb, post_w, None, s)
    kv_shape = (2, N_KV_HEADS, HEAD_DIM)
    out_prompt = (
        y_p.reshape(b, s, d),
        cmp_p.reshape((1, b, s) + kv_shape), slc_p.reshape((1, b, s) + kv_shape),
        win_p.reshape((1, b, WINDOW) + kv_shape),
        pst[None, :, SUBLANES - (CONV_W - 1):], ast[None, :, SUBLANES - (CONV_W - 1):])

    r8 = SAMPLE_ROWS
    pad_rows = lambda a, lo: jnp.pad(a, ((0, 0), (lo, r8 - lo - a.shape[1]), (0, 0))).reshape(db * r8, a.shape[2])
    xs = pad_rows(x_sample, r8 - t)
    st_mix = pad_rows(state_conv_mix[l], r8 - t - (CONV_W - 1))
    st_ffn = pad_rows(state_conv_ffn[l], r8 - t - (CONV_W - 1))
    pos_s = past_len - (r8 - t) + jnp.arange(r8, dtype=jnp.int32)
    tabs_s = tuple(jnp.tile(a, (db, 1)) for a in _rope_tables(pos_s))
    ma_s, sgb_s, q_s, cmp_s, slc_s, win_s, gates_s, p_s = _front(
        xs, g_mix, tabs_s, front_w, conv_mix_w[l], st_mix, db * r8)
    kcvc_s = _compress_sample(page_table, _feature_major_pages(cache_cmp_kv[l]), cw)
    keys_per_step = PAGES_PER_STEP * page_rows
    per = keys_per_step // SLC_LEN
    n_slots = past_len // keys_per_step + 1
    lane = np.arange(n_slots * LANES)
    col_block = np.where(lane % LANES < per, (lane // LANES) * per + lane % LANES, -1)
    col_block = np.where(col_block <= past_len // SLC_LEN, col_block, -1)
    ovl_s = jnp.asarray(_overlap_matrix(past_len // CMP_STRIDE, n_slots * LANES, col_block))
    o_s = _attend_sample(page_table, q_s, gates_s, kcvc_s, _feature_major_pages(cache_slc_kv[l]), slc_s,
                         _feature_major_pages(cache_win_kv[l]), win_s, econst_t, ovl_s, past_len)
    y_s, a_s = _post(xs, o_s, ma_s, sgb_s, post_w, st_ffn, db * r8)
    tok = lambda a: a.reshape(db, r8, -1)[:, r8 - t:]
    win_new = jnp.concatenate([cache_win_kv[l][:, t:], tok(win_s).reshape((db, t) + kv_shape)], axis=1)
    out_sample = (
        tok(y_s),
        tok(cmp_s).reshape((1, db, t) + kv_shape), tok(slc_s).reshape((1, db, t) + kv_shape),
        win_new.reshape((1, db, WINDOW) + kv_shape),
        p_s.reshape(db, r8, d)[None, :, r8 - (CONV_W - 1):], a_s.reshape(db, r8, dff)[None, :, r8 - (CONV_W - 1):])

    return (out_prompt[0], out_sample[0], out_prompt[1], out_prompt[2], out_prompt[3], out_prompt[4],
            out_prompt[5], out_sample[1], out_sample[2], out_sample[3], out_sample[4], out_sample[5])
```

```python
import functools

import numpy as np
import jax
import jax.numpy as jnp
from jax import lax
from jax.experimental import pallas as pl
from jax.experimental.pallas import tpu as pltpu

F32 = jnp.float32
BF16 = jnp.bfloat16

N_HEADS = 16
HEAD_DIM = 64
N_KV_HEADS = 4
N_BRANCH = 3
ROT_DIM = 16
ROPE_THETA = 500000.0
CMP_LEN = 32
CMP_STRIDE = 16
SLC_LEN = 64
N_SEL = 16
N_LOCAL = 2
WINDOW = 512
CONV_W = 3
NORM_EPS = 1e-6
FORCED_SCORE = 1e9

LANES = 128
SUBLANES = 8
KV_W = 2 * N_KV_HEADS * HEAD_DIM
MASK_BIG = 2.0 ** 100
LOG2_E = 1.4426950408889634
VMEM_LIMIT = 56 * 1024 * 1024

ROW_TILE = 256
FFN_CHUNKS = 2
CONV_CHUNKS = 2
CMP_ROWS = 4096
Q_TILE = 128
K_TILE = 512
SUM_ROWS = 16
SAMPLE_ROWS = 8
PAGES_PER_STEP = 32


def _cparams(sem):
    return pltpu.CompilerParams(dimension_semantics=sem, vmem_limit_bytes=VMEM_LIMIT)


def _const_spec(shape):
    nd = len(shape)
    return pl.BlockSpec(shape, lambda *_: (0,) * nd, pipeline_mode=pl.Buffered(1))


def _dot(a, b):
    return jnp.dot(a, b, preferred_element_type=F32)


def _dot_nt(a, b):
    return lax.dot_general(a, b, (((1,), (1,)), ((), ())), preferred_element_type=F32)


def _rms(x, g):
    y = x * lax.rsqrt(jnp.mean(x * x, axis=-1, keepdims=True) + NORM_EPS)
    return y * g


def _rope_block(x, cos, sa, sb):
    return x * cos + pltpu.roll(x, 8, 1) * sa + pltpu.roll(x, LANES - 8, 1) * sb


def _shifted_conv(buf, p, w_ref, rows):
    buf[SUBLANES:SUBLANES + rows, :] = p
    p1 = buf[SUBLANES - 1:SUBLANES - 1 + rows, :]
    p2 = buf[SUBLANES - 2:SUBLANES - 2 + rows, :]
    return p2 * w_ref[0:1, :] + p1 * w_ref[1:2, :] + p * w_ref[2:3, :]


def _front_kernel(sample, tiles_per_seq, *refs):
    (x_ref, g_ref, cos_ref, sa_ref, sb_ref, wc_ref, wq_ref, wkv_ref, wgl_ref, wgab_ref, cw_ref, wco_ref) = refs[:12]
    if sample:
        st_ref, ma_ref, sgb_ref, q_ref, cmp_ref, slc_ref, win_ref, gl_ref, pst_ref, pbuf = refs[12:]
    else:
        ma_ref, sgb_ref, qt_ref, cmp_ref, slc_ref, win_ref, kb_ref, vt_ref, glt_ref, pst_ref, pbuf = refs[12:]
    rows, d = x_ref.shape
    u = _rms(x_ref[...], g_ref[...]).astype(BF16)

    if sample:
        row = lax.broadcasted_iota(jnp.int32, (rows, 1), 0) % SAMPLE_ROWS
        is_state = (row >= SAMPLE_ROWS - 4 - (CONV_W - 1)) & (row < SAMPLE_ROWS - 4)
        pbuf[0:SUBLANES, :] = jnp.zeros((SUBLANES, d), F32)
    else:
        @pl.when(pl.program_id(0) % tiles_per_seq == 0)
        def _():
            pbuf[0:SUBLANES, :] = jnp.zeros((SUBLANES, d), F32)
    ya = None
    for c in range(CONV_CHUNKS):
        lo, hi = c * d // CONV_CHUNKS, (c + 1) * d // CONV_CHUNKS
        p = _dot(u, wc_ref[:, d + lo:d + hi]) * _dot(u, wc_ref[:, lo:hi])
        if sample:
            p = jnp.where(is_state, st_ref[:, lo:hi], p)
        buf = pbuf.at[:, lo:hi]
        yc = _shifted_conv(buf, p, cw_ref.at[:, lo:hi], rows)
        tail = buf[rows:rows + SUBLANES, :]
        if sample:
            pst_ref[:, lo:hi] = p
        else:
            pst_ref[0, :, lo:hi] = tail
            buf[0:SUBLANES, :] = tail
        part = _dot((_dot(u, wc_ref[:, 2 * d + lo:2 * d + hi]) * yc).astype(BF16), wco_ref[lo:hi, :])
        ya = part if ya is None else ya + part

    zg = _dot(u, wgab_ref[...])
    ma_ref[...] = jax.nn.sigmoid(zg[:, 0:d]) * ya
    sgb_ref[...] = jax.nn.sigmoid(zg[:, d:2 * d])

    cos, sa, sb = cos_ref[...], sa_ref[...], sb_ref[...]
    zq = _dot(u, wq_ref[...])
    scale = HEAD_DIM ** -0.5 * LOG2_E
    for j in range(zq.shape[1] // LANES):
        blk = _rope_block(zq[:, j * LANES:(j + 1) * LANES], cos, sa, sb) * scale
        if sample:
            q_ref[:, j * LANES:(j + 1) * LANES] = blk.astype(BF16)
        else:
            qt_ref[j * LANES:(j + 1) * LANES, :] = blk.T.astype(BF16)

    zkv = _dot(u, wkv_ref[...])
    per_branch = KV_W // LANES
    half = per_branch // 2
    outs = (cmp_ref, slc_ref, win_ref)
    for j in range(zkv.shape[1] // LANES):
        br, jj = divmod(j, per_branch)
        blk = zkv[:, j * LANES:(j + 1) * LANES]
        if jj < half:
            blk = _rope_block(blk, cos, sa, sb)
        outs[br][:, jj * LANES:(jj + 1) * LANES] = blk
        if br > 0 and not sample:
            c = (br - 1) * half + jj % half
            if jj < half:
                kb_ref[:, c * LANES:(c + 1) * LANES] = blk.astype(BF16)
            else:
                vt_ref[c * LANES:(c + 1) * LANES, :] = blk.T.astype(BF16)

    gl = jax.nn.sigmoid(_dot(u, wgl_ref[...]))
    if sample:
        gl_ref[...] = gl
    else:
        for j in range(gl.shape[1] // LANES):
            glt_ref[j * LANES:(j + 1) * LANES, :] = gl[:, j * LANES:(j + 1) * LANES].T


def _front(x, g, tabs, w, conv_w, state, seq_rows):
    n, d = x.shape
    sample = state is not None
    tm = n if sample else ROW_TILE
    tiles_per_seq = seq_rows // tm
    nt = n // tm
    n_seq = n // seq_rows
    qw = N_HEADS * HEAD_DIM
    row = lambda w_: pl.BlockSpec((tm, w_), lambda i: (i, 0))
    col = lambda h_: pl.BlockSpec((h_, tm), lambda i: (0, i))
    tab = pl.BlockSpec((tm, LANES), lambda i: (i % tiles_per_seq, 0))
    args = [x, g, *tabs, *w[:5], conv_w, w[5]]
    in_specs = [row(d), _const_spec((1, d)), tab, tab, tab] + [_const_spec(a.shape) for a in w[:5]] + [
        _const_spec(conv_w.shape), _const_spec(w[5].shape)]
    f32_rows = lambda w_: jax.ShapeDtypeStruct((n, w_), F32)
    if sample:
        args.append(state)
        in_specs.append(row(d))
        out_shape = [f32_rows(d), f32_rows(d), jax.ShapeDtypeStruct((n, qw), BF16), f32_rows(KV_W), f32_rows(KV_W),
                     f32_rows(KV_W), f32_rows(2 * LANES), f32_rows(d)]
        out_specs = [row(d), row(d), row(qw), row(KV_W), row(KV_W), row(KV_W), row(2 * LANES), row(d)]
    else:
        tail_tiles = WINDOW // tm
        win_tail = pl.BlockSpec((tm, KV_W), lambda i: (
            (i // tiles_per_seq) * tail_tiles + jnp.maximum(i % tiles_per_seq - (tiles_per_seq - tail_tiles), 0), 0))
        out_shape = [f32_rows(d), f32_rows(d), jax.ShapeDtypeStruct((qw, n), BF16), f32_rows(KV_W), f32_rows(KV_W),
                     jax.ShapeDtypeStruct((n_seq * WINDOW, KV_W), F32), jax.ShapeDtypeStruct((n, KV_W), BF16),
                     jax.ShapeDtypeStruct((KV_W, n), BF16), jax.ShapeDtypeStruct((2 * LANES, n), F32),
                     jax.ShapeDtypeStruct((n_seq, SUBLANES, d), F32)]
        out_specs = [row(d), row(d), col(qw), row(KV_W), row(KV_W), win_tail, row(KV_W), col(KV_W),
                     col(2 * LANES), pl.BlockSpec((1, SUBLANES, d), lambda i: (i // tiles_per_seq, 0, 0))]
    return pl.pallas_call(
        functools.partial(_front_kernel, sample, tiles_per_seq),
        grid=(nt,), in_specs=in_specs, out_specs=out_specs, out_shape=out_shape,
        scratch_shapes=[pltpu.VMEM((tm + SUBLANES, d), F32)],
        compiler_params=_cparams(("arbitrary",)), name="front_sample" if sample else "front_prompt",
    )(*args)


def _cmp_bias_kernel(pe_ref, w1_ref, b1_ref, o_ref):
    for kv in range(2):
        a = _dot(pe_ref[kv], w1_ref[kv])
        o_ref[kv] = jnp.broadcast_to(b1_ref[kv] + a[0:1, 0:LANES] + a[1:2, LANES:2 * LANES], (SUBLANES, LANES))


def _compress_tile(lhs_of, w1_ref, hb_ref, w2_ref, b2_ref, carry_ref, sh_ref, out_ref, vt_ref=None):
    nck = out_ref.shape[0]
    for j in range(KV_W // LANES):
        kv = j // 2
        a = _dot(lhs_of(j), w1_ref[kv])
        sh_ref[SUBLANES:SUBLANES + nck, :] = a[:, 0:LANES]
        sh_ref[0:SUBLANES, :] = carry_ref[j]
        hid = sh_ref[SUBLANES - 1:SUBLANES - 1 + nck, :] + a[:, LANES:2 * LANES] + hb_ref[kv][0:1, :]
        carry_ref[j] = sh_ref[nck:nck + SUBLANES, :]
        o = _dot(jax.nn.silu(hid).astype(BF16), w2_ref[kv]) + b2_ref[kv]
        out_ref[:, j * LANES:(j + 1) * LANES] = o.astype(out_ref.dtype)
        if vt_ref is not None and kv == 1:
            vt_ref[(j - 2) * LANES:(j - 1) * LANES, :] = o.T.astype(vt_ref.dtype)


def _compress_prompt_kernel(r0, r1, r2, r3, w1_ref, hb_ref, w2_ref, b2_ref, out_ref, vt_ref, carry_ref, sh_ref):
    @pl.when(pl.program_id(1) == 0)
    def _():
        carry_ref[...] = jnp.zeros(carry_ref.shape, F32)
    planes = (r0, r1, r2, r3)
    nck = out_ref.shape[1]

    def lhs_of(j):
        return jnp.concatenate([planes[j][pl.ds(s, nck, stride=CMP_STRIDE), :].astype(BF16)
                                for s in range(CMP_STRIDE)], axis=1)

    _compress_tile(lhs_of, w1_ref, hb_ref, w2_ref, b2_ref, carry_ref, sh_ref, out_ref.at[0], vt_ref.at[0])


def _page_copies(pt_ref, cache_ref, buf_ref, sem_ref, slot, b, grp, n_pages):
    return [pltpu.make_async_copy(cache_ref.at[pt_ref[b, grp * n_pages + p]], buf_ref.at[slot, p], sem_ref.at[slot])
            for p in range(n_pages)]


def _pages_side_by_side(buf_ref, slot, rows, pages):
    return jnp.concatenate([buf_ref[slot, p, rows, :] for p in pages], axis=1)


def _gather_step(pt_ref, cache_ref, buf_ref, sem_ref, n_pages):
    b, g = pl.program_id(0), pl.program_id(1)
    nb, ng = pl.num_programs(0), pl.num_programs(1)
    step = b * ng + g
    slot = step % 2

    @pl.when(step == 0)
    def _():
        for p, c in enumerate(_page_copies(pt_ref, cache_ref, buf_ref, sem_ref, 0, 0, 0, n_pages)):
            c.start(priority=p % 2)

    @pl.when(step + 1 < nb * ng)
    def _():
        nxt = step + 1
        copies = _page_copies(pt_ref, cache_ref, buf_ref, sem_ref, 1 - slot, nxt // ng, nxt % ng, n_pages)
        for p, c in enumerate(copies):
            c.start(priority=p % 2)

    for c in _page_copies(pt_ref, cache_ref, buf_ref, sem_ref, slot, b, g, n_pages):
        c.wait()
    return slot


def _compress_sample_kernel(pt_ref, cache_ref, perm_ref, w1_ref, hb_ref, w2_ref, b2_ref, out_ref,
                            buf_ref, sem_ref, carry_ref, sh_ref):
    n_pages = buf_ref.shape[1]
    slot = _gather_step(pt_ref, cache_ref, buf_ref, sem_ref, n_pages)

    @pl.when(pl.program_id(1) == 0)
    def _():
        carry_ref[...] = jnp.zeros(carry_ref.shape, F32)

    span = perm_ref.shape[0]
    ck = span // CMP_STRIDE

    per_span = span // buf_ref.shape[3]

    def permuted(half, g):
        x = _pages_side_by_side(buf_ref, slot, slice(half * 2 * LANES, (half + 1) * 2 * LANES),
                                range(g * per_span, (g + 1) * per_span))
        return _dot_nt(perm_ref[...], x.astype(BF16)).astype(BF16)

    tiles = [[permuted(half, g) for g in range(n_pages // per_span)] for half in range(2)]

    def lhs_of(j):
        half, jj = divmod(j, 2)
        return jnp.concatenate(
            [jnp.concatenate([tl[s * ck:(s + 1) * ck, jj * LANES:(jj + 1) * LANES] for tl in tiles[half]], axis=0)
             for s in range(CMP_STRIDE)], axis=1)

    _compress_tile(lhs_of, w1_ref, hb_ref, w2_ref, b2_ref, carry_ref, sh_ref, out_ref.at[0])


def _compress_weights(cmp_pe, cmp_w1, cmp_b1, cmp_w2, cmp_b2):
    r = CMP_LEN // CMP_STRIDE
    eye2 = jnp.eye(2, dtype=F32)
    w1 = cmp_w1.reshape(2, r, CMP_STRIDE, HEAD_DIM, HEAD_DIM)
    w1bd = jnp.einsum("krsde,hg->kshdrge", w1, eye2).reshape(2, CMP_STRIDE * LANES, r * LANES).astype(BF16)
    w2bd = jnp.einsum("kde,hg->khdge", cmp_w2, eye2).reshape(2, LANES, LANES).astype(BF16)
    pe = cmp_pe.reshape(2, r, CMP_STRIDE, 1, HEAD_DIM)
    pe = jnp.broadcast_to(pe, (2, r, CMP_STRIDE, 2, HEAD_DIM)).reshape(2, r, CMP_STRIDE * LANES)
    pe = jnp.pad(pe, ((0, 0), (0, SUBLANES - r), (0, 0))).astype(BF16)
    b1 = jnp.tile(cmp_b1, (1, 2)).reshape(2, 1, LANES)
    b2 = jnp.tile(cmp_b2, (1, 2)).reshape(2, 1, LANES)
    hb = pl.pallas_call(
        _cmp_bias_kernel, out_shape=jax.ShapeDtypeStruct((2, SUBLANES, LANES), F32), name="compress_bias",
    )(pe, w1bd, b1)
    return w1bd, hb, w2bd, b2


def _compress_prompt(cmp_rows, cw, n_seq):
    n = cmp_rows.shape[0]
    s = n // n_seq
    nck = CMP_ROWS // CMP_STRIDE
    tiles = s // CMP_ROWS
    w1bd, hb, w2bd, b2 = cw
    return pl.pallas_call(
        _compress_prompt_kernel,
        grid=(n_seq, tiles),
        in_specs=[pl.BlockSpec((CMP_ROWS, LANES), functools.partial(lambda j, b, t: (b * tiles + t, j), j))
                  for j in range(KV_W // LANES)] + [
                  _const_spec(w1bd.shape), _const_spec(hb.shape), _const_spec(w2bd.shape), _const_spec(b2.shape)],
        out_specs=[pl.BlockSpec((1, nck, KV_W), lambda b, t: (b, t, 0)),
                   pl.BlockSpec((1, KV_W // 2, nck), lambda b, t: (b, 0, t))],
        out_shape=[jax.ShapeDtypeStruct((n_seq, s // CMP_STRIDE, KV_W), BF16),
                   jax.ShapeDtypeStruct((n_seq, KV_W // 2, s // CMP_STRIDE), BF16)],
        scratch_shapes=[pltpu.VMEM((KV_W // LANES, SUBLANES, LANES), F32),
                        pltpu.VMEM((nck + SUBLANES, LANES), F32)],
        compiler_params=_cparams(("arbitrary", "arbitrary")), name="compress_prompt",
    )(cmp_rows, cmp_rows, cmp_rows, cmp_rows, w1bd, hb, w2bd, b2)


def _feature_major_pages(cache):
    return cache.transpose(0, 2, 3, 4, 1).reshape(cache.shape[0], KV_W, cache.shape[1])


def _compress_sample(page_table, cache_t, cw):
    db, n_pages = page_table.shape
    page_rows = cache_t.shape[2]
    pages_per = CMP_ROWS // page_rows
    groups = n_pages // pages_per
    nck = CMP_ROWS // CMP_STRIDE
    w1bd, hb, w2bd, b2 = cw
    span = 2 * page_rows
    ck = span // CMP_STRIDE
    row = np.arange(span)
    s_of, c_of = row // ck, row % ck
    per_page = page_rows // CMP_STRIDE
    tok = (c_of // per_page) * page_rows + CMP_STRIDE * (c_of % per_page) + s_of
    perm = jnp.asarray(tok[:, None] == np.arange(span)[None, :], dtype=BF16)
    cspec = lambda a: pl.BlockSpec(a.shape, lambda b, g, pt: (0,) * a.ndim, pipeline_mode=pl.Buffered(1))
    grid_spec = pltpu.PrefetchScalarGridSpec(
        num_scalar_prefetch=1, grid=(db, groups),
        in_specs=[pl.BlockSpec(memory_space=pl.ANY), cspec(perm), cspec(w1bd), cspec(hb), cspec(w2bd), cspec(b2)],
        out_specs=pl.BlockSpec((1, nck, KV_W), lambda b, g, pt: (b, g, 0)),
        scratch_shapes=[pltpu.VMEM((2, pages_per, KV_W, page_rows), F32), pltpu.SemaphoreType.DMA((2,)),
                        pltpu.VMEM((KV_W // LANES, SUBLANES, LANES), F32),
                        pltpu.VMEM((nck + SUBLANES, LANES), F32)])
    return pl.pallas_call(
        _compress_sample_kernel, grid_spec=grid_spec,
        out_shape=jax.ShapeDtypeStruct((db, n_pages * page_rows // CMP_STRIDE, KV_W), BF16),
        compiler_params=_cparams(("arbitrary", "arbitrary")), name="compress_sample",
    )(page_table, cache_t, perm, w1bd, hb, w2bd, b2)


def _masked_softmax(s, mask):
    s = jnp.where(mask, s, -jnp.inf)
    m = jnp.max(s, axis=-1, keepdims=True)
    m = jnp.where(m == -jnp.inf, 0.0, m)
    e = jnp.where(mask, jnp.exp2(s - m), 0.0)
    return e / jnp.maximum(jnp.sum(e, axis=-1, keepdims=True), 1e-30)


def _select_blocks(imp, blk, t_pos, axis):
    cur = t_pos // SLC_LEN
    valid = blk <= cur
    forced = (blk == 0) | (valid & (blk > cur - N_LOCAL))
    cand = jnp.where(forced, -jnp.inf, jnp.where(valid, imp, -FORCED_SCORE))
    cand = jnp.where(blk >= 0, cand, -jnp.inf)
    length = imp.shape[axis]
    idx = lax.broadcasted_iota(jnp.int32, imp.shape, axis).astype(F32)

    def pick(_, carry):
        cand, notsel = carry
        m = jnp.max(cand, axis=axis, keepdims=True)
        first = jnp.min(jnp.where(cand == m, idx, float(length)), axis=axis, keepdims=True)
        hit = idx == first
        return jnp.where(hit, -jnp.inf, cand), jnp.where(hit, 0.0, notsel)

    _, notsel = lax.fori_loop(0, N_SEL - (N_LOCAL + 1), pick, (cand, jnp.where(forced, 0.0, 1.0)), unroll=True)
    return notsel


def _flash_update(carry, s, v, v_transposed=False):
    m, l, acc = carry
    m_new = jnp.maximum(m, jnp.max(s, axis=-1, keepdims=True))
    alpha = jnp.exp2(m - m_new)
    p = jnp.exp2(s - m_new)
    l = alpha * l + jnp.sum(p, axis=-1, keepdims=True)
    pv = _dot_nt(p.astype(v.dtype), v) if v_transposed else _dot(p.astype(v.dtype), v)
    return m_new, l, alpha * acc + pv


def _half_mask(shape):
    return lax.broadcasted_iota(jnp.int32, shape, len(shape) - 1) < HEAD_DIM


def _stack_heads(q_blk, nr):
    first = _half_mask((q_blk.shape[0], LANES))
    zero = jnp.zeros((), q_blk.dtype)
    parts = []
    for gl in range(2):
        for r in range(nr):
            blk = q_blk[:, r * LANES:(r + 1) * LANES]
            parts.append(jnp.where(first if gl == 0 else ~first, blk, zero))
    return jnp.concatenate(parts, axis=0)


def _merge_pair(o, nr, t):
    first = _half_mask((t, LANES))
    return [jnp.where(first, o[r * t:(r + 1) * t], o[(nr + r) * t:(nr + r + 1) * t]) for r in range(nr)]


def _gate_vec(g_ref, br, r, t):
    c = br * 2 * (N_HEADS // N_KV_HEADS) + r * 2
    return jnp.where(_half_mask((t, LANES)), g_ref[:, c:c + 1], g_ref[:, c + 1:c + 2])


def _biased_exp0(s, bias):
    s = s + jnp.concatenate([bias] * (s.shape[1] // bias.shape[1]), axis=1)
    m = jnp.max(s, axis=0, keepdims=True)
    m = jnp.where(m < -0.5 * MASK_BIG, 0.0, m)
    return jnp.exp2(s - m)


def _recip_sum(total):
    return 1.0 / jnp.maximum(total, 1e-30)


def _pv_own_head(vt, p, with_sum=False):
    half = p.shape[1] // 2
    heads = [vt[0:HEAD_DIM, :], vt[HEAD_DIM:2 * HEAD_DIM, :]]
    if with_sum:
        ones = jnp.ones((SUM_ROWS, vt.shape[1]), vt.dtype)
        heads = [jnp.concatenate([h, ones], axis=0) for h in heads]
    return jnp.concatenate([_dot(heads[0], p[:, 0:half]), _dot(heads[1], p[:, half:])], axis=1)


def _flash_update0(carry, s, vt):
    m, acc = carry
    m_new = jnp.maximum(m, jnp.max(s, axis=0, keepdims=True))
    alpha = jnp.exp2(m - m_new)
    p = jnp.exp2(s - m_new)
    return m_new, alpha * acc + _pv_own_head(vt, p.astype(vt.dtype), with_sum=True)


def _attend_prompt_kernel(qt_ref, gt_ref, kc_ref, vct_ref, kb_ref, vt_ref, e_ref, ovlt_ref,
                          cbias_ref, wbias_ref, fbias_ref,
                          o_ref, qa_ref, sa_ref, sb_ref, m_ref, acc_ref, oc_ref, ow_ref):
    i = pl.program_id(1)
    nr = N_HEADS // N_KV_HEADS
    t = Q_TILE
    rows = 2 * nr * t
    n_pairs = N_KV_HEADS // 2
    t0 = i * t
    tq = t0 + lax.broadcasted_iota(jnp.int32, (1, t), 1)
    first = lax.broadcasted_iota(jnp.int32, (LANES, t), 0) < HEAD_DIM
    zero = jnp.zeros((), BF16)
    ncmp = kc_ref.shape[1]
    pair_lanes = lambda k: slice(k * LANES, (k + 1) * LANES)
    win_lanes = lambda k: slice((n_pairs + k) * LANES, (n_pairs + k + 1) * LANES)

    qts = [jnp.concatenate([jnp.where(first if gl == 0 else ~first,
                                      qt_ref[(k * nr + r) * LANES:(k * nr + r + 1) * LANES, :], zero)
                            for gl in range(2) for r in range(nr)], axis=1) for k in range(n_pairs)]

    c_off = pl.multiple_of(ncmp - i * (t // CMP_STRIDE), t // CMP_STRIDE)
    cbias = cbias_ref[pl.ds(c_off, ncmp), :]
    not_junk = lax.broadcasted_iota(jnp.int32, (SUBLANES, 1), 0) >= 1
    psum = []
    for k in range(n_pairs):
        s_c = _dot(kc_ref[0, :, pair_lanes(k)], qts[k])
        head = jnp.where(not_junk, s_c[0:SUBLANES], -MASK_BIG)
        e_c = _biased_exp0(jnp.concatenate([head, s_c[SUBLANES:]], axis=0), cbias)
        p_c = e_c * _recip_sum(jnp.sum(e_c, axis=0, keepdims=True))
        oc_ref[k] = _pv_own_head(vct_ref[0, pair_lanes(k), :], p_c.astype(BF16))
        for gl in range(2):
            acc = p_c[:, gl * nr * t:gl * nr * t + t]
            for r in range(1, nr):
                acc = acc + p_c[:, (gl * nr + r) * t:(gl * nr + r + 1) * t]
            psum.append(acc)
    imp = jnp.dot(ovlt_ref[...], jnp.concatenate(psum, axis=1), preferred_element_type=F32,
                  precision=lax.Precision.HIGHEST)

    wlen = WINDOW + t
    ws = pl.multiple_of(jnp.maximum(t0 - WINDOW, 0), t)
    anchor = (imp[0:1, 0:1] * 0.0).astype(BF16)
    wbias = wbias_ref[jnp.minimum(i, WINDOW // t)]
    for k in range(n_pairs):
        e_w = _biased_exp0(_dot(kb_ref[pl.ds(ws, wlen), win_lanes(k)] + anchor, qts[k]), wbias)
        pv = _pv_own_head(vt_ref[win_lanes(k), pl.ds(ws, wlen)], e_w.astype(BF16), with_sum=True)
        ow_ref[k] = pv[0:HEAD_DIM] * _recip_sum(pv[HEAD_DIM:HEAD_DIM + 1])

    nslc = ovlt_ref.shape[0]
    blk = lax.broadcasted_iota(jnp.int32, (nslc, 1), 0)
    ns = _select_blocks(imp, blk, jnp.concatenate([tq] * N_KV_HEADS, axis=1), axis=0).astype(BF16)
    for k in range(n_pairs):
        flags = [ns[:, (2 * k + gl) * t:(2 * k + gl + 1) * t] for gl in range(2)]
        qa_ref[k] = jnp.concatenate([qts[k], jnp.concatenate([flags[0]] * nr + [flags[1]] * nr, axis=1)], axis=0)

    def produce(s_ref, j):
        off = pl.multiple_of(j * K_TILE, K_TILE)
        for k in range(n_pairs):
            ka = jnp.concatenate([kb_ref[pl.ds(off, K_TILE), pair_lanes(k)], e_ref[pl.ds(off, K_TILE), :]], axis=1)
            s_ref[k] = _dot(ka, qa_ref[k])

    def consume(s_ref, j, causal):
        off = pl.multiple_of(j * K_TILE, K_TILE)
        for k in range(n_pairs):
            s = s_ref[k]
            if causal:
                fb = fbias_ref[(t0 - off) // t]
                s = s + jnp.concatenate([fb] * (rows // t), axis=1)
            m_ref[k], acc_ref[k] = _flash_update0(
                (m_ref[k], acc_ref[k]), s, vt_ref[pair_lanes(k), pl.ds(off, K_TILE)])

    m_ref[...] = jnp.full(m_ref.shape, -jnp.inf, F32)
    acc_ref[...] = jnp.zeros(acc_ref.shape, F32)
    last = (t0 + t - 1) // K_TILE
    produce(sa_ref, 0)

    def pair(jj, _):
        j = 2 * jj
        produce(sb_ref, j + 1)
        consume(sa_ref, j, False)
        produce(sa_ref, j + 2)
        consume(sb_ref, j + 1, False)
        return 0

    lax.fori_loop(0, last // 2, pair, 0)

    @pl.when(last % 2 == 1)
    def _():
        produce(sb_ref, last)
        consume(sa_ref, last - 1, False)
        consume(sb_ref, last, True)

    @pl.when(last % 2 == 0)
    def _():
        consume(sa_ref, last, True)

    for k in range(n_pairs):
        o_s = acc_ref[k, 0:HEAD_DIM, :] * (1.0 / acc_ref[k, HEAD_DIM:HEAD_DIM + 1, :])
        branches = (oc_ref[k], o_s, ow_ref[k])
        for r in range(nr):
            c0, c1 = r * t, (nr + r) * t
            o = None
            for br, ob in enumerate(branches):
                c = k * LANES + br * 2 * nr + r * 2
                gate = jnp.where(first, gt_ref[c:c + 1, :], gt_ref[c + 1:c + 2, :])
                term = gate * jnp.concatenate([ob[:, c0:c0 + t], ob[:, c1:c1 + t]], axis=0)
                o = term if o is None else o + term
            o_ref[:, (k * nr + r) * LANES:(k * nr + r + 1) * LANES] = o.T.astype(o_ref.dtype)


def _mask_bias(visible):
    return jnp.asarray(np.where(visible, 0.0, -MASK_BIG), dtype=F32)


def _attend_prompt(qt, gt, kcvc, vct, kb, vt, econst, ovlt, n_seq):
    t = Q_TILE
    tcol = np.arange(t)[None, :]
    ncmp_ = kcvc.shape[1]
    x = np.arange(-ncmp_, ncmp_)[:, None]
    cbias = _mask_bias(x * CMP_STRIDE + (CMP_LEN - CMP_STRIDE - 1) <= tcol)
    r = np.arange(WINDOW + t)[None, :, None]
    d = (np.arange(WINDOW // t + 1) * t)[:, None, None]
    wbias = _mask_bias((r <= d + tcol[None]) & (r > d + tcol[None] - WINDOW))
    r = np.arange(K_TILE)[None, :, None]
    d = (np.arange(K_TILE // t) * t)[:, None, None]
    fbias = _mask_bias(r <= d + tcol[None])
    n = qt.shape[1]
    s = n // n_seq
    nq = s // Q_TILE
    nr = N_HEADS // N_KV_HEADS
    ncmp = kcvc.shape[1]
    qw = N_HEADS * HEAD_DIM
    rows = 2 * nr * Q_TILE
    n_pairs = N_KV_HEADS // 2
    per_seq = lambda shape, imap: pl.BlockSpec(shape, imap, pipeline_mode=pl.Buffered(1))
    return pl.pallas_call(
        _attend_prompt_kernel,
        grid=(n_seq, nq),
        in_specs=[pl.BlockSpec((qw, Q_TILE), lambda b, i: (0, b * nq + i)),
                  pl.BlockSpec((n_pairs * LANES, Q_TILE), lambda b, i: (0, b * nq + i)),
                  per_seq((1, ncmp, n_pairs * LANES), lambda b, i: (b, 0, 0)),
                  per_seq((1, n_pairs * LANES, ncmp), lambda b, i: (b, 0, 0)),
                  per_seq((s, KV_W), lambda b, i: (b, 0)), per_seq((KV_W, s), lambda b, i: (0, b)),
                  _const_spec(econst.shape), _const_spec(ovlt.shape), _const_spec(cbias.shape),
                  _const_spec(wbias.shape), _const_spec(fbias.shape)],
        out_specs=pl.BlockSpec((Q_TILE, qw), lambda b, i: (b * nq + i, 0)),
        out_shape=jax.ShapeDtypeStruct((n, qw), BF16),
        scratch_shapes=[pltpu.VMEM((n_pairs, 2 * LANES, rows), BF16), pltpu.VMEM((n_pairs, K_TILE, rows), F32),
                        pltpu.VMEM((n_pairs, K_TILE, rows), F32), pltpu.VMEM((n_pairs, 1, rows), F32),
                        pltpu.VMEM((n_pairs, HEAD_DIM + SUM_ROWS, rows), F32),
                        pltpu.VMEM((n_pairs, HEAD_DIM, rows), F32), pltpu.VMEM((n_pairs, HEAD_DIM, rows), F32)],
        compiler_params=_cparams(("arbitrary", "arbitrary")), name="attend_prompt",
    )(qt, gt, kcvc, vct, kb, vt, econst, ovlt, cbias, wbias, fbias)


def _attend_sample_kernel(past_len, pt_ref, q_ref, g_ref, kcvc_ref, cache_ref, slc_ref, wcache_ref, win_ref,
                          e_ref, ovl_ref, o_ref, buf_ref, sem_ref, ns_ref, oc_ref, m_ref, l_ref, acc_ref):
    g = pl.program_id(1)
    ng = pl.num_programs(1)
    nr = N_HEADS // N_KV_HEADS
    t = SAMPLE_ROWS
    rows = 2 * nr * t
    n_pages = buf_ref.shape[1]
    keys = n_pages * buf_ref.shape[3]
    slot = _gather_step(pt_ref, cache_ref, buf_ref, sem_ref, n_pages)
    tq = past_len - (t - 4) + lax.broadcasted_iota(jnp.int32, (t, 1), 0)
    tq_rows = jnp.concatenate([tq] * (2 * nr), axis=0)
    qs = [_stack_heads(q_ref[:, k * nr * LANES:(k + 1) * nr * LANES], nr) for k in range(2)]

    @pl.when(g == 0)
    def _():
        ncmp = kcvc_ref.shape[1]
        cidx = lax.broadcasted_iota(jnp.int32, (1, ncmp), 1)
        c_mask = (cidx >= 1) & (cidx * CMP_STRIDE + (CMP_LEN - CMP_STRIDE - 1) <= tq_rows)
        width = ovl_ref.shape[1]
        lane = lax.broadcasted_iota(jnp.int32, (1, width), 1)
        per = keys // SLC_LEN
        blk = (lane // LANES) * per + lane % LANES
        real = (lane % LANES < per) & (blk <= past_len // SLC_LEN)
        psums = []
        for k in range(2):
            p_c = _masked_softmax(_dot_nt(qs[k], kcvc_ref[0, :, k * LANES:(k + 1) * LANES]), c_mask)
            oc_ref[k] = _dot(p_c.astype(BF16), kcvc_ref[0, :, (2 + k) * LANES:(3 + k) * LANES])
            for gl in range(2):
                psum = p_c[gl * nr * t:gl * nr * t + t]
                for r in range(1, nr):
                    psum = psum + p_c[(gl * nr + r) * t:(gl * nr + r + 1) * t]
                psums.append(psum)
        imp = jnp.dot(jnp.concatenate(psums, axis=0), ovl_ref[...], preferred_element_type=F32,
                      precision=lax.Precision.HIGHEST)
        ns = _select_blocks(imp, jnp.where(real, blk, -1), jnp.concatenate([tq] * 4, axis=0), axis=1)
        for k in range(2):
            for gl in range(2):
                for r in range(nr):
                    ns_ref[k, (gl * nr + r) * t:(gl * nr + r + 1) * t, :] = ns[(2 * k + gl) * t:(2 * k + gl + 1) * t]
        m_ref[...] = jnp.full(m_ref.shape, -jnp.inf, F32)
        l_ref[...] = jnp.zeros(l_ref.shape, F32)
        acc_ref[...] = jnp.zeros(acc_ref.shape, F32)

    feat = lambda blk: _pages_side_by_side(buf_ref, slot, slice(blk * LANES, (blk + 1) * LANES), range(n_pages))
    for k in range(2):
        ns = ns_ref[k, :, pl.ds(pl.multiple_of(g * LANES, LANES), LANES)]
        qa = jnp.concatenate([qs[k], ns.astype(BF16)], axis=1)
        ka = jnp.concatenate([feat(k).astype(BF16), e_ref[...]], axis=0)
        v = feat(2 + k).astype(BF16)
        m_ref[k], l_ref[k], acc_ref[k] = _flash_update((m_ref[k], l_ref[k], acc_ref[k]), _dot(qa, ka), v,
                                                       v_transposed=True)

    @pl.when(g == ng - 1)
    def _():
        kidx = lax.broadcasted_iota(jnp.int32, (1, t), 1)
        kpos = past_len - (t - 4) + kidx
        new_ok = (kidx >= t - 4) & (kpos <= tq_rows)
        wb = wcache_ref.shape[2]
        wpos =past_len - wb + lax.broadcasted_iota(jnp.int32, (1, wb), 1)
        w_old = (wpos <= tq_rows) & (wpos > tq_rows - WINDOW) & (wpos >= 0)
        w_new = new_ok & (kpos > tq_rows - WINDOW)
        for k in range(2):
            kcol, vcol = slice(k * LANES, (k + 1) * LANES), slice((2 + k) * LANES, (3 + k) * LANES)
            qf = qs[k].astype(F32)
            s_new = jnp.where(new_ok, _dot_nt(qf, slc_ref[:, kcol]), -MASK_BIG)
            _, l, acc = _flash_update((m_ref[k], l_ref[k], acc_ref[k]), s_new, slc_ref[:, vcol])
            o_s = acc / l
            so = jnp.where(w_old, _dot(qs[k], wcache_ref[0, kcol, :].astype(BF16)), -jnp.inf)
            sn = jnp.where(w_new, _dot_nt(qf, win_ref[:, kcol]), -jnp.inf)
            m = jnp.maximum(jnp.max(so, axis=-1, keepdims=True), jnp.max(sn, axis=-1, keepdims=True))
            m = jnp.where(m == -jnp.inf, 0.0, m)
            eo = jnp.where(w_old, jnp.exp2(so - m), 0.0)
            en = jnp.where(w_new, jnp.exp2(sn - m), 0.0)
            den = jnp.maximum(jnp.sum(eo, axis=-1, keepdims=True) + jnp.sum(en, axis=-1, keepdims=True), 1e-30)
            o_w = (_dot_nt(eo.astype(BF16), wcache_ref[0, vcol, :].astype(BF16)) + _dot(en, win_ref[:, vcol])) / den
            oc, os_, ow = _merge_pair(oc_ref[k], nr, t), _merge_pair(o_s, nr, t), _merge_pair(o_w, nr, t)
            gk = g_ref.at[:, k * LANES:(k + 1) * LANES]
            for r in range(nr):
                o = _gate_vec(gk, 0, r, t) * oc[r] + _gate_vec(gk, 1, r, t) * os_[r] + _gate_vec(gk, 2, r, t) * ow[r]
                o_ref[:, (k * nr + r) * LANES:(k * nr + r + 1) * LANES] = o.astype(o_ref.dtype)


def _attend_sample(page_table, q, gates, kcvc, cache_t, slc_new, wcache, win_new, econst_t, ovl, past_len):
    db, n_pages = page_table.shape
    page_rows = cache_t.shape[2]
    econst = econst_t
    groups = n_pages // PAGES_PER_STEP
    keys = PAGES_PER_STEP * page_rows
    nr = N_HEADS // N_KV_HEADS
    t = SAMPLE_ROWS
    rows = 2 * nr * t
    qw = N_HEADS * HEAD_DIM
    cmap = lambda nd: (lambda b, g, pt: (0,) * nd)
    grid_spec = pltpu.PrefetchScalarGridSpec(
        num_scalar_prefetch=1, grid=(db, groups),
        in_specs=[pl.BlockSpec((t, qw), lambda b, g, pt: (b, 0)),
                  pl.BlockSpec((t, 2 * LANES), lambda b, g, pt: (b, 0)),
                  pl.BlockSpec((1,) + kcvc.shape[1:], lambda b, g, pt: (b, 0, 0)),
                  pl.BlockSpec(memory_space=pl.ANY),
                  pl.BlockSpec((t, KV_W), lambda b, g, pt: (b, 0)),
                  pl.BlockSpec((1,) + wcache.shape[1:], lambda b, g, pt: (b, 0, 0)),
                  pl.BlockSpec((t, KV_W), lambda b, g, pt: (b, 0)),
                  pl.BlockSpec(econst.shape, cmap(2), pipeline_mode=pl.Buffered(1)),
                  pl.BlockSpec(ovl.shape, cmap(2), pipeline_mode=pl.Buffered(1))],
        out_specs=pl.BlockSpec((t, qw), lambda b, g, pt: (b, 0)),
        scratch_shapes=[pltpu.VMEM((2, PAGES_PER_STEP, KV_W, page_rows), F32), pltpu.SemaphoreType.DMA((2,)),
                        pltpu.VMEM((2, rows, ovl.shape[1]), F32), pltpu.VMEM((2, rows, LANES), F32),
                        pltpu.VMEM((2, rows, 1), F32), pltpu.VMEM((2, rows, 1), F32),
                        pltpu.VMEM((2, rows, LANES), F32)])
    return pl.pallas_call(
        functools.partial(_attend_sample_kernel, past_len), grid_spec=grid_spec,
        out_shape=jax.ShapeDtypeStruct((db * t, qw), BF16),
        compiler_params=_cparams(("arbitrary", "arbitrary")), name="attend_sample",
    )(page_table, q, gates, kcvc, cache_t, slc_new, wcache, win_new, econst, ovl)


def _post_kernel(sample, tiles_per_seq, *refs):
    if sample:
        (x_ref, o_ref, ma_ref, sgb_ref, wao_ref, wo_ref, gf_ref, wfi_ref, fcw_ref, wfd_ref, gfin_ref, st_ref,
         y_ref, ast_ref, abuf) = refs
    else:
        (x_ref, o_ref, ma_ref, sgb_ref, wao_ref, wo_ref, gf_ref, wfi_ref, fcw_ref, wfd_ref, gfin_ref,
         y_ref, ast_ref, abuf) = refs
    rows = x_ref.shape[0]
    dff = fcw_ref.shape[1]
    m = ma_ref[...] + sgb_ref[...] * _dot(o_ref[...], wao_ref[...])
    h = x_ref[...] + _dot(m.astype(BF16), wo_ref[...])
    v = _rms(h, gf_ref[...]).astype(BF16)
    if sample:
        row = lax.broadcasted_iota(jnp.int32, (rows, 1), 0) % SAMPLE_ROWS
        is_state = (row >= SAMPLE_ROWS - 4 - (CONV_W - 1)) & (row < SAMPLE_ROWS - 4)
        abuf[0:SUBLANES, :] = jnp.zeros((SUBLANES, dff), F32)
    else:
        @pl.when(pl.program_id(0) % tiles_per_seq == 0)
        def _():
            abuf[0:SUBLANES, :] = jnp.zeros((SUBLANES, dff), F32)
    bounds = [LANES * ((dff // LANES) * c // FFN_CHUNKS) for c in range(FFN_CHUNKS + 1)]
    hh = h
    for lo, hi in zip(bounds[:-1], bounds[1:]):
        a = _dot(v, wfi_ref[:, lo:hi])
        g = _dot(v, wfi_ref[:, dff + lo:dff + hi])
        if sample:
            a = jnp.where(is_state, st_ref[:, lo:hi], a)
        buf = abuf.at[:, lo:hi]
        ac = _shifted_conv(buf, a, fcw_ref.at[:, lo:hi], rows)
        tail = buf[rows:rows + SUBLANES, :]
        if sample:
            ast_ref[:, lo:hi] = a
        else:
            ast_ref[0, :, lo:hi] = tail
            buf[0:SUBLANES, :] = tail
        hh = hh + _dot((jax.nn.silu(ac) * g).astype(BF16), wfd_ref[lo:hi, :])
    y_ref[...] = _rms(hh, gfin_ref[...])


def _post(x, o, ma, sgb, w, state, seq_rows):
    n, d = x.shape
    sample = state is not None
    tm = n if sample else ROW_TILE
    tiles_per_seq = seq_rows // tm
    n_seq = n // seq_rows
    wao, wo, gf, wfi, fcw, wfd, gfin = w
    dff = fcw.shape[1]
    row = lambda w_: pl.BlockSpec((tm, w_), lambda i: (i, 0))
    args = [x, o, ma, sgb, *w]
    in_specs = [row(d), row(o.shape[1]), row(d), row(d)] + [_const_spec(a.shape) for a in w]
    if sample:
        args.append(state)
        in_specs.append(row(dff))
        ast_shape, ast_spec = jax.ShapeDtypeStruct((n, dff), F32), row(dff)
    else:
        ast_shape = jax.ShapeDtypeStruct((n_seq, SUBLANES, dff), F32)
        ast_spec = pl.BlockSpec((1, SUBLANES, dff), lambda i: (i // tiles_per_seq, 0, 0))
    return pl.pallas_call(
        functools.partial(_post_kernel, sample, tiles_per_seq),
        grid=(n // tm,), in_specs=in_specs, out_specs=[row(d), ast_spec],
        out_shape=[jax.ShapeDtypeStruct((n, d), F32), ast_shape],
        scratch_shapes=[pltpu.VMEM((tm + SUBLANES, dff), F32)],
        compiler_params=_cparams(("arbitrary",)), name="post_sample" if sample else "post_prompt",
    )(*args)


def _pair_head_order():
    nr = N_HEADS // N_KV_HEADS
    return [2 * nr * k + nr * half + r for k in range(2) for r in range(nr) for half in range(2)]


def _rope_tables(pos):
    half = ROT_DIM // 2
    inv_freq = jnp.power(ROPE_THETA, -jnp.arange(half, dtype=F32) * (2.0 / ROT_DIM))
    ang = pos.astype(F32)[:, None] * inv_freq[None, :]
    cos, sin = jnp.cos(ang), jnp.sin(ang)
    n = pos.shape[0]
    one, zero = jnp.ones((n, HEAD_DIM - ROT_DIM), F32), jnp.zeros((n, HEAD_DIM - ROT_DIM), F32)
    zh = jnp.zeros((n, half), F32)
    tabs = (jnp.concatenate([cos, cos, one], 1), jnp.concatenate([zh, sin, zero], 1),
            jnp.concatenate([-sin, zh, zero], 1))
    return tuple(jnp.tile(a, (1, LANES // HEAD_DIM)) for a in tabs)


def _overlap_matrix(n_rows, n_cols, col_block):
    c = np.arange(n_rows)[:, None] - 1
    blk = col_block[None, :]
    cs, ss = c * CMP_STRIDE, blk * SLC_LEN
    return ((c >= 0) & (blk >= 0) & (cs < ss + SLC_LEN) & (cs + CMP_LEN > ss)).astype(np.float32)


def _block_onehot(n_keys, transposed=False):
    e = (np.arange(n_keys)[:, None] // SLC_LEN == np.arange(LANES)[None, :]).astype(np.float32)
    return jnp.asarray(-MASK_BIG * (e.T if transposed else e), dtype=BF16)


def _split_w_in(w_in, d):
    conv_dim = d
    q_dim = N_HEADS * HEAD_DIM
    kv_dim = N_BRANCH * KV_W
    o0 = 3 * conv_dim
    wc = w_in[:, 0:o0]
    wq = w_in[:, o0:o0 + q_dim].reshape(d, N_HEADS, HEAD_DIM)[:, np.array(_pair_head_order())].reshape(d, q_dim)
    o1 = o0 + q_dim
    wkv = w_in[:, o1:o1 + kv_dim]
    o2 = o1 + kv_dim
    wl = w_in[:, o2:o2 + N_BRANCH * N_HEADS].reshape(d, N_HEADS, N_BRANCH)
    wl = wl[:, np.array(_pair_head_order())].reshape(d, 2, N_HEADS // 2, N_BRANCH).transpose(0, 1, 3, 2)
    wl = jnp.pad(wl.reshape(d, 2, N_BRANCH * N_HEADS // 2), ((0, 0), (0, 0), (0, LANES - N_BRANCH * N_HEADS // 2)))
    wl = wl.reshape(d, 2 * LANES)
    o3 = o2 + N_BRANCH * N_HEADS
    wgab = w_in[:, o3:o3 + 2 * d]
    return [a.astype(BF16) for a in (wc, wq, wkv, wl, wgab)]


def kernel(x_prompt, x_sample, cache_cmp_kv, cache_slc_kv, cache_win_kv, state_conv_mix, state_conv_ffn,
           page_table, norm_mix_g, w_in, conv_mix_w, w_conv_out, cmp_pe, cmp_w1, cmp_b1, cmp_w2, cmp_b2,
           w_attn_out, w_out, norm_ffn_g, w_ff_in, ff_conv_w, w_ff_down, norm_final_g):
    depth = w_in.shape[0]
    assert depth == 1, "single-layer step"
    b, s, d = x_prompt.shape
    db, t, _ = x_sample.shape
    page_rows = cache_cmp_kv.shape[2]
    past_len = page_table.shape[1] * page_rows
    assert t == 4 and s % K_TILE == 0 and s % CMP_ROWS == 0 and s >= WINDOW + Q_TILE
    assert past_len % (PAGES_PER_STEP * page_rows) == 0 and cache_win_kv.shape[2] == WINDOW
    assert N_SEL <= s // SLC_LEN <= LANES, "prompt selection blocks fit one lane block"
    dff = ff_conv_w.shape[2]
    l = 0

    order = np.array(_pair_head_order())
    front_w = _split_w_in(w_in[l], d) + [w_conv_out[l].astype(BF16)]
    post_w = [w_attn_out[l].reshape(N_HEADS, HEAD_DIM, d)[order].reshape(N_HEADS * HEAD_DIM, d).astype(BF16),
              w_out[l].astype(BF16), norm_ffn_g[l].reshape(1, d), w_ff_in[l].astype(BF16), ff_conv_w[l],
              w_ff_down[l].astype(BF16), norm_final_g.reshape(1, d)]
    g_mix = norm_mix_g[l].reshape(1, d)
    cw = _compress_weights(cmp_pe[l], cmp_w1[l], cmp_b1[l], cmp_w2[l], cmp_b2[l])
    econst = _block_onehot(s)
    econst_t = _block_onehot(PAGES_PER_STEP * page_rows, transposed=True)

    xp = x_prompt.reshape(b * s, d)
    ma, sgb, qt, cmp_p, slc_p, win_p, kb, vt, gt, pst = _front(
        xp, g_mix, _rope_tables(jnp.arange(s, dtype=jnp.int32)), front_w, conv_mix_w[l], None, s)
    kcvc, vct = _compress_prompt(cmp_p, cw, b)
    ovlt_p = jnp.asarray(_overlap_matrix(s // CMP_STRIDE, LANES, np.arange(LANES)).T)
    o = _attend_prompt(qt, gt, kcvc, vct, kb, vt, econst, ovlt_p, b)
    y_p, ast = _post(xp, o, ma, sgb, post_w, None, s)
    kv_shape = (2, N_KV_HEADS, HEAD_DIM)
    out_prompt = (
        y_p.reshape(b, s, d),
        cmp_p.reshape((1, b, s) + kv_shape), slc_p.reshape((1, b, s) + kv_shape),
        win_p.reshape((1, b, WINDOW) + kv_shape),
        pst[None, :, SUBLANES - (CONV_W - 1):], ast[None, :, SUBLANES - (CONV_W - 1):])

    r8 = SAMPLE_ROWS
    pad_rows = lambda a, lo: jnp.pad(a, ((0, 0), (lo, r8 - lo - a.shape[1]), (0, 0))).reshape(db * r8, a.shape[2])
    xs = pad_rows(x_sample, r8 - t)
    st_mix = pad_rows(state_conv_mix[l], r8 - t - (CONV_W - 1))
    st_ffn = pad_rows(state_conv_ffn[l], r8 - t - (CONV_W - 1))
    pos_s = past_len - (r8 - t) + jnp.arange(r8, dtype=jnp.int32)
    tabs_s = tuple(jnp.tile(a, (db, 1)) for a in _rope_tables(pos_s))
    ma_s, sgb_s, q_s, cmp_s, slc_s, win_s, gates_s, p_s = _front(
        xs, g_mix, tabs_s, front_w, conv_mix_w[l], st_mix, db * r8)
    kcvc_s = _compress_sample(page_table, _feature_major_pages(cache_cmp_kv[l]), cw)
    keys_per_step = PAGES_PER_STEP * page_rows
    per = keys_per_step // SLC_LEN
    n_slots = past_len // keys_per_step + 1
    lane = np.arange(n_slots * LANES)
    col_block = np.where(lane % LANES < per, (lane // LANES) * per + lane % LANES, -1)
    col_block = np.where(col_block <= past_len // SLC_LEN, col_block, -1)
    ovl_s = jnp.asarray(_overlap_matrix(past_len // CMP_STRIDE, n_slots * LANES, col_block))
    o_s = _attend_sample(page_table, q_s, gates_s, kcvc_s, _feature_major_pages(cache_slc_kv[l]), slc_s,
                         _feature_major_pages(cache_win_kv[l]), win_s, econst_t, ovl_s, past_len)
    y_s, a_s = _post(xs, o_s, ma_s, sgb_s, post_w, st_ffn, db * r8)
    tok = lambda a: a.reshape(db, r8, -1)[:, r8 - t:]
    win_new = jnp.concatenate([cache_win_kv[l][:, t:], tok(win_s).reshape((db, t) + kv_shape)], axis=1)
    out_sample = (
        tok(y_s),
        tok(cmp_s).reshape((1, db, t) + kv_shape), tok(slc_s).reshape((1, db, t) + kv_shape),
        win_new.reshape((1, db, WINDOW) + kv_shape),
        p_s.reshape(db, r8, d)[None, :, r8 - (CONV_W - 1):], a_s.reshape(db, r8, dff)[None, :, r8 - (CONV_W - 1):])

    return (out_prompt[0], out_sample[0], out_prompt[1], out_prompt[2], out_prompt[3], out_prompt[4],
            out_prompt[5], out_sample[1], out_sample[2], out_sample[3], out_sample[4], out_sample[5])
```
